```python
import math
import jax
import jax.numpy as jnp
from jax import lax

D_MODEL = 1024
BATCH = 4
SEQ = 4096
DEPTH = 4
DEC_BATCH = 128
DEC_SEQ = 1
PAST_LEN = 2048
PAGE_SIZE = 128

HEAD_DIM = 64
N_BRANCH = 4
MIX_W = D_MODEL // N_BRANCH
NSA_HEADS = MIX_W // HEAD_DIM
CMP_LEN = 32
CMP_STRIDE = 16
SEL_BLOCK = 64
N_SEL = 16
WINDOW = 512
SEL_FORCE = 1000.0
HG_HEADS = 4
HG_DK = MIX_W // HG_HEADS
HG_DV = MIX_W // HG_HEADS
HG_CHUNK = 64
S5_GROUP = 16
S5_GROUPS = MIX_W // S5_GROUP
S5_STATE = 64
FOX_HEADS = MIX_W // HEAD_DIM
FOX_BIAS_INIT = 3.0
Q_BLOCK = 128
D_FF = ((8 * D_MODEL // 3 + 127) // 128) * 128
CONV_W = 3
ROPE_THETA = 10000.0
EPS = 1e-6

IN_SIZES = (NSA_HEADS * HEAD_DIM, 6 * HEAD_DIM, 3 * NSA_HEADS,
            HG_HEADS * HG_DK, HG_HEADS * HG_DK, HG_HEADS * HG_DV, HG_HEADS * HG_DV,
            MIX_W,
            FOX_HEADS * HEAD_DIM, FOX_HEADS * HEAD_DIM, FOX_HEADS * HEAD_DIM, FOX_HEADS,
            N_BRANCH * D_MODEL)
N_IN = sum(IN_SIZES)

kernel_name = 'nsa_hgrn2_s5_fox_parallel_hybrid_step'


def _split_points():
    pts, acc = [], 0
    for s in IN_SIZES[:-1]:
        acc += s
        pts.append(acc)
    return pts


def rmsnorm(x, g):
    xf = x.astype(jnp.float32)
    y = xf * lax.rsqrt(jnp.mean(xf * xf, axis=-1, keepdims=True) + EPS)
    return (y * g.astype(jnp.float32)).astype(x.dtype)


def rope(x, pos):
    half = HEAD_DIM // 2
    inv = ROPE_THETA ** (-jnp.arange(half, dtype=jnp.float32) / half)
    ang = pos.astype(jnp.float32)[:, None] * inv[None, :]
    shp = (pos.shape[0],) + (1,) * (x.ndim - 3) + (half,)
    cos, sin = jnp.cos(ang).reshape(shp), jnp.sin(ang).reshape(shp)
    xf = x.astype(jnp.float32)
    x1, x2 = xf[..., :half], xf[..., half:]
    return jnp.concatenate([x1 * cos - x2 * sin, x2 * cos + x1 * sin], axis=-1).astype(x.dtype)


def masked_softmax(logits, mask):
    lg = jnp.where(mask, logits.astype(jnp.float32), -jnp.inf)
    m = jnp.max(lg, axis=-1, keepdims=True)
    m = jnp.where(jnp.isfinite(m), m, 0.0)
    e = jnp.where(mask, jnp.exp(lg - m), 0.0)
    return e / jnp.maximum(jnp.sum(e, axis=-1, keepdims=True), 1e-30)


def _to_blocks(a, qb, nb):
    pad = nb * qb - a.shape[1]
    a = jnp.pad(a, [(0, 0), (0, pad)] + [(0, 0)] * (a.ndim - 2))
    return jnp.moveaxis(a.reshape((a.shape[0], nb, qb) + a.shape[2:]), 1, 0)


def _from_blocks(a, length):
    a = jnp.moveaxis(a, 0, 1)
    return a.reshape((a.shape[0], a.shape[1] * a.shape[2]) + a.shape[3:])[:, :length]


def gather_pages(pool, page_table):
    g = pool[page_table]
    return g.reshape((g.shape[0], g.shape[1] * g.shape[2]) + g.shape[3:])


def nsa_compress(rows, pe, w):
    B, T, _ = rows.shape
    n_c = (T - CMP_LEN) // CMP_STRIDE + 1
    idx = jnp.arange(n_c)[:, None] * CMP_STRIDE + jnp.arange(CMP_LEN)[None, :]
    blocks = rows[:, idx] + pe.astype(rows.dtype)
    return blocks.reshape(B, n_c, CMP_LEN * HEAD_DIM) @ w


def nsa_mixer(q, rows, win, gates, past_len, win_pos0, k_norm_c, pe, wk, wv):
    B, Lq, H, _ = q.shape
    T = rows.shape[1]
    Tw = win.shape[1]
    scale = HEAD_DIM ** -0.5
    q_rot = rope(q, past_len + jnp.arange(Lq))
    k_c = rmsnorm(nsa_compress(rows[:, :, 0], pe, wk), k_norm_c)
    v_c = nsa_compress(rows[:, :, 1], pe, wv).astype(jnp.float32)
    n_c = k_c.shape[1]
    c_start = jnp.arange(n_c) * CMP_STRIDE
    c_end = c_start + CMP_LEN - 1
    n_s = -(-T // SEL_BLOCK)
    s_start = jnp.arange(n_s) * SEL_BLOCK
    cover = ((c_start[:, None] < s_start[None, :] + SEL_BLOCK) & (c_end[:, None] >= s_start[None, :])).astype(jnp.float32)
    sel_rows = jnp.pad(rows[:, :, 2:4], ((0, 0), (0, n_s * SEL_BLOCK - T), (0, 0), (0, 0)))
    sel_rows = sel_rows.reshape(B, n_s, SEL_BLOCK, 2, HEAD_DIM)
    k_top = min(N_SEL, n_s)
    qb = min(Q_BLOCK, Lq)
    nb = -(-Lq // qb)
    kw_all = jnp.pad(win, ((0, 0), (WINDOW, nb * qb - Lq), (0, 0), (0, 0)))
    buf_len = past_len - win_pos0
    bidx = jnp.arange(B)[:, None, None]

    def block(args):
        qn, qr, bi = args
        pos = past_len + bi * qb + jnp.arange(qb)
        pc = masked_softmax(jnp.einsum('bqhd,bcd->bhqc', qn, k_c) * scale,
                            (c_end[None, :] <= pos[:, None])[None, None])
        o_c = jnp.einsum('bhqc,bcd->bqhd', pc, v_c)
        imp = jnp.einsum('bhqc,cs->bqs', pc, cover)
        cur = pos // SEL_BLOCK
        j = jnp.arange(n_s)[None, :]
        forced = (j == 0) | (j == cur[:, None]) | (j == cur[:, None] - 1)
        score = jnp.where(j > cur[:, None], -jnp.inf, imp + SEL_FORCE * forced)
        _, sel = lax.top_k(score, k_top)
        g = sel_rows[bidx, sel]
        gpos = sel[..., None] * SEL_BLOCK + jnp.arange(SEL_BLOCK)
        ls = jnp.einsum('bqhd,bqksd->bhqks', qr, g[..., 0, :]).reshape(B, H, qb, k_top * SEL_BLOCK) * scale
        ms = (gpos <= pos[None, :, None, None]).reshape(B, 1, qb, k_top * SEL_BLOCK)
        ps = masked_softmax(ls, ms).reshape(B, H, qb, k_top, SEL_BLOCK)
        o_s = jnp.einsum('bhqks,bqksd->bqhd', ps, g[..., 1, :].astype(jnp.float32))
        start = buf_len + bi * qb
        kwin = lax.dynamic_slice_in_dim(kw_all, start, WINDOW + qb, axis=1)
        wpos = win_pos0 - WINDOW + start + jnp.arange(WINDOW + qb)
        mw = ((wpos[None, :] >= win_pos0) & (wpos[None, :] < win_pos0 + Tw)
              & (wpos[None, :] <= pos[:, None]) & (pos[:, None] - wpos[None, :] < WINDOW))
        pw = masked_softmax(jnp.einsum('bqhd,bkd->bhqk', qr, kwin[:, :, 0]) * scale, mw[None, None])
        o_w = jnp.einsum('bhqk,bkd->bqhd', pw, kwin[:, :, 1].astype(jnp.float32))
        return o_c, o_s, o_w

    o_c, o_s, o_w = lax.map(block, (_to_blocks(q, qb, nb), _to_blocks(q_rot, qb, nb), jnp.arange(nb)))
    gs = jax.nn.sigmoid(gates.astype(jnp.float32))[..., None]
    o = (gs[:, :, :, 0] * _from_blocks(o_c, Lq) + gs[:, :, :, 1] * _from_blocks(o_s, Lq)
         + gs[:, :, :, 2] * _from_blocks(o_w, Lq))
    return o.reshape(B, Lq, H * HEAD_DIM)


def fox_mixer(q, k, v, logf, past_len):
    B, Lq, H, _ = q.shape
    T = k.shape[1]
    scale = HEAD_DIM ** -0.5
    c = jnp.cumsum(logf.astype(jnp.float32), axis=1)
    cT = jnp.moveaxis(c, 2, 1)
    vf = v.astype(jnp.float32)
    kpos = jnp.arange(T)
    qb = min(Q_BLOCK, Lq)
    nb = -(-Lq // qb)

    def block(args):
        qblk, cq, bi = args
        pos = past_len + bi * qb + jnp.arange(qb)
        bias = jnp.moveaxis(cq, 2, 1)[..., None] - cT[:, :, None, :]
        lg = jnp.einsum('bqhd,bkhd->bhqk', qblk, k).astype(jnp.float32) * scale + bias
        p = masked_softmax(lg, (kpos[None, :] <= pos[:, None])[None, None])
        return jnp.einsum('bhqk,bkhd->bqhd', p, vf)

    o = lax.map(block, (_to_blocks(q, qb, nb), _to_blocks(c[:, past_len:], qb, nb), jnp.arange(nb)))
    return _from_blocks(o, Lq)


def hgrn_mixer(q, logf, k, v, s0):
    B, L, H, _ = q.shape
    C = min(HG_CHUNK, L)
    n = -(-L // C)
    pad = n * C - L

    def prep(a):
        a = jnp.pad(a.astype(jnp.float32), ((0, 0), (0, pad), (0, 0), (0, 0)))
        return jnp.moveaxis(a.reshape((B, n, C) + a.shape[2:]), 1, 0)

    tril = jnp.tril(jnp.ones((C, C), dtype=bool))[None, :, :, None, None]

    def step(S, inp):
        qc, fc, kc, vc = inp
        b = jnp.cumsum(fc, axis=1)
        diff = b[:, :, None] - b[:, None, :]
        dec = jnp.where(tril, jnp.exp(jnp.where(tril, diff, 0.0)), 0.0)
        att = jnp.einsum('btshk,bthk,bshk->bhts', dec, qc, kc)
        o = jnp.einsum('bhts,bshv->bthv', att, vc) + jnp.einsum('bthk,bhkv->bthv', qc * jnp.exp(b), S)
        bl = b[:, -1]
        S = jnp.exp(bl)[..., None] * S + jnp.einsum('bshk,bshv->bhkv', kc * jnp.exp(bl[:, None] - b), vc)
        return S, o

    S, o = lax.scan(step, s0.astype(jnp.float32), (prep(q), prep(logf), prep(k), prep(v)))
    o = jnp.moveaxis(o, 0, 1).reshape(B, n * C, H, v.shape[-1])[:, :L]
    return o, S


def s5_mixer(u, h0, a_re, a_im, log_dt, b_re, b_im, c_re, c_im, d):
    B, L, _ = u.shape
    uf = u.astype(jnp.float32)
    ug = uf.reshape(B, L, S5_GROUPS, S5_GROUP)
    dt = jnp.exp(log_dt.astype(jnp.float32))[:, None]
    ar, ai = a_re.astype(jnp.float32), a_im.astype(jnp.float32)
    mag = jnp.exp(dt * ar)
    abr, abi = mag * jnp.cos(dt * ai), mag * jnp.sin(dt * ai)
    den = ar * ar + ai * ai
    zr = ((abr - 1.0) * ar + abi * ai) / den
    zi = (abi * ar - (abr - 1.0) * ai) / den
    br_, bi_ = b_re.astype(jnp.float32), b_im.astype(jnp.float32)
    bbr = zr[..., None] * br_ - zi[..., None] * bi_
    bbi = zr[..., None] * bi_ + zi[..., None] * br_
    xr = jnp.einsum('blgc,gpc->blgp', ug, bbr)
    xi = jnp.einsum('blgc,gpc->blgp', ug, bbi)
    h0r, h0i = h0[..., 0].astype(jnp.float32), h0[..., 1].astype(jnp.float32)
    xr = xr.at[:, 0].add(abr * h0r - abi * h0i)
    xi = xi.at[:, 0].add(abr * h0i + abi * h0r)
    Ar = jnp.broadcast_to(abr, xr.shape)
    Ai = jnp.broadcast_to(abi, xi.shape)

    def comb(e1, e2):
        a1r, a1i, b1r, b1i = e1
        a2r, a2i, b2r, b2i = e2
        return (a2r * a1r - a2i * a1i, a2r * a1i + a2i * a1r,
                a2r * b1r - a2i * b1i + b2r, a2r * b1i + a2i * b1r + b2i)

    _, _, hr, hi = lax.associative_scan(comb, (Ar, Ai, xr, xi), axis=1)
    y = (jnp.einsum('gcp,blgp->blgc', c_re.astype(jnp.float32), hr)
         - jnp.einsum('gcp,blgp->blgc', c_im.astype(jnp.float32), hi))
    y = y.reshape(B, L, MIX_W) + d.astype(jnp.float32) * uf
    return y, jnp.stack([hr[:, -1], hi[:, -1]], axis=-1)


def conv_ffn(h, buf, w_up, conv_w, conv_b, w_down):
    L = h.shape[1]
    a, b = jnp.split(h @ w_up, 2, axis=-1)
    ap = jnp.concatenate([buf.astype(a.dtype), a], axis=1)
    ac = conv_b + ap[:, 0:L] * conv_w[0]
    for j in range(1, CONV_W):
        ac = ac + ap[:, j:j + L] * conv_w[j]
    return (jax.nn.silu(ac) * b) @ w_down, ap[:, L:]


def trunk_layer(x, past_len, nsa_past, win_buf, fox_past_kv, fox_past_logf, hg_state, s5_state, conv_buf, lb, p):
    B, L, _ = x.shape
    dt = x.dtype
    pos = past_len + jnp.arange(L)
    h = rmsnorm(x, p['attn_norm'])
    (nq, nkv, ng, hq, hf, hi, hgate, su, fq, fk, fv, ff, bg) = jnp.split(h @ p['w_in'], _split_points(), axis=-1)
    nq = rmsnorm(nq.reshape(B, L, NSA_HEADS, HEAD_DIM), p['nsa_q_norm'])
    nkv = nkv.reshape(B, L, 6, HEAD_DIM)
    k_slc = rope(rmsnorm(nkv[:, :, 2], p['nsa_k_norm'][1]), pos)
    k_win = rope(rmsnorm(nkv[:, :, 4], p['nsa_k_norm'][2]), pos)
    nsa_rows = jnp.stack([nkv[:, :, 0], nkv[:, :, 1], k_slc, nkv[:, :, 3]], axis=2)
    win_rows = jnp.stack([k_win, nkv[:, :, 5]], axis=2)
    rows_full = jnp.concatenate([nsa_past.astype(dt), nsa_rows], axis=1)
    win_full = jnp.concatenate([win_buf.astype(dt), win_rows], axis=1)
    o_nsa = nsa_mixer(nq, rows_full, win_full, ng.reshape(B, L, NSA_HEADS, 3), past_len,
                      past_len - win_buf.shape[1], p['nsa_k_norm'][0], p['nsa_cmp_pe'],
                      p['nsa_cmp_wk'], p['nsa_cmp_wv'])
    new_win = win_full[:, win_full.shape[1] - min(WINDOW, win_full.shape[1]):]
    lbh = lb.reshape(HG_HEADS, HG_DK)
    f = lbh + (1.0 - lbh) * jax.nn.sigmoid(hf.reshape(B, L, HG_HEADS, HG_DK).astype(jnp.float32))
    o_h, hg_new = hgrn_mixer(hq.reshape(B, L, HG_HEADS, HG_DK), jnp.log(f), 1.0 - f,
                             hi.reshape(B, L, HG_HEADS, HG_DV), hg_state)
    o_h = rmsnorm(o_h, p['hgrn_o_norm']) * jax.nn.silu(hgate.reshape(B, L, HG_HEADS, HG_DV).astype(jnp.float32))
    y5, s5_new = s5_mixer(su, s5_state, p['s5_a_re'], p['s5_a_im'], p['s5_log_dt'], p['s5_b_re'],
                          p['s5_b_im'], p['s5_c_re'], p['s5_c_im'], p['s5_d'])
    z5 = jax.nn.gelu(y5)
    o_s5 = z5 * jax.nn.sigmoid(z5 @ p['s5_glu_w'].astype(jnp.float32) + p['s5_glu_b'].astype(jnp.float32))
    fq = rmsnorm(fq.reshape(B, L, FOX_HEADS, HEAD_DIM), p['fox_q_norm'])
    fk = rmsnorm(fk.reshape(B, L, FOX_HEADS, HEAD_DIM), p['fox_k_norm'])
    fv = fv.reshape(B, L, FOX_HEADS, HEAD_DIM)
    flog = jax.nn.log_sigmoid(ff.astype(jnp.float32) + p['fox_f_bias'].astype(jnp.float32))
    fox_rows = jnp.stack([fk, fv], axis=2)
    kv_full = jnp.concatenate([fox_past_kv.astype(dt), fox_rows], axis=1)
    lf_full = jnp.concatenate([fox_past_logf.astype(jnp.float32), flog], axis=1)
    o_fox = fox_mixer(fq, kv_full[:, :, 0], kv_full[:, :, 1], lf_full, past_len)
    branches = jnp.stack([o_nsa, o_h.reshape(B, L, MIX_W), o_s5, o_fox.reshape(B, L, MIX_W)], axis=2).astype(dt)
    gates = jax.nn.sigmoid(bg.reshape(B, L, N_BRANCH, D_MODEL).astype(jnp.float32)).astype(dt)
    proj = jnp.einsum('blnc,ncd->blnd', branches, p['w_branch'])
    merged = jnp.einsum('blnd,blnd->bld', gates, proj)
    x = x + (merged @ p['w_out']).astype(dt)
    f_out, conv_new = conv_ffn(rmsnorm(x, p['ffn_norm']), conv_buf, p['ffn_w_up'], p['ffn_conv_w'],
                               p['ffn_conv_b'], p['ffn_w_down'])
    x = x + f_out.astype(dt)
    return x, nsa_rows, new_win, fox_rows, flog, hg_new, s5_new, conv_new


def setup_inputs(seed: int = 0) -> dict:
    key = jax.random.key(seed)
    ks = iter(jax.random.split(key, 40))
    n_pages = PAST_LEN // PAGE_SIZE
    n_pool = (DEC_BATCH * n_pages * 5) // 4
    wb = min(WINDOW, PAST_LEN)
    nrm = lambda shape, s=1.0: s * jax.random.normal(next(ks), shape, dtype=jnp.float32)
    gain = lambda shape: 1.0 + 0.02 * jax.random.normal(next(ks), shape, dtype=jnp.float32)
    x_prompt = nrm((BATCH, SEQ, D_MODEL))
    x_sample = nrm((DEC_BATCH, DEC_SEQ, D_MODEL))
    cache_nsa_kv = nrm((DEPTH, n_pool, PAGE_SIZE, 4, HEAD_DIM))
    cache_nsa_win = nrm((DEPTH, DEC_BATCH, wb, 2, HEAD_DIM))
    cache_fox_kv = nrm((DEPTH, n_pool, PAGE_SIZE, 2, FOX_HEADS, HEAD_DIM))
    cache_fox_logf = jax.nn.log_sigmoid(FOX_BIAS_INIT + nrm((DEPTH, n_pool, PAGE_SIZE, FOX_HEADS), 0.5))
    state_hgrn = nrm((DEPTH, DEC_BATCH, HG_HEADS, HG_DK, HG_DV), 0.5)
    state_s5 = nrm((DEPTH, DEC_BATCH, S5_GROUPS, S5_STATE, 2), 0.1)
    state_ffn_conv = nrm((DEPTH, DEC_BATCH, CONV_W - 1, D_FF))
    page_table = jax.random.permutation(next(ks), n_pool)[:DEC_BATCH * n_pages].reshape(DEC_BATCH, n_pages).astype(jnp.int32)
    attn_norm = gain((DEPTH, D_MODEL))
    w_in = nrm((DEPTH, D_MODEL, N_IN), D_MODEL ** -0.5)
    nsa_q_norm = gain((DEPTH, HEAD_DIM))
    nsa_k_norm = gain((DEPTH, 3, HEAD_DIM))
    nsa_cmp_pe = nrm((DEPTH, CMP_LEN, HEAD_DIM), 0.1)
    nsa_cmp_wk = nrm((DEPTH, CMP_LEN * HEAD_DIM, HEAD_DIM), (CMP_LEN * HEAD_DIM) ** -0.5)
    nsa_cmp_wv = nrm((DEPTH, CMP_LEN * HEAD_DIM, HEAD_DIM), (CMP_LEN * HEAD_DIM) ** -0.5)
    hgrn_lb_logits = nrm((DEPTH, HG_HEADS * HG_DK), 0.1)
    hgrn_o_norm = gain((DEPTH, HG_DV))
    s5_a_re = -0.5 + nrm((DEPTH, S5_GROUPS, S5_STATE), 0.01)
    s5_a_im = math.pi * jnp.arange(S5_STATE, dtype=jnp.float32) + nrm((DEPTH, S5_GROUPS, S5_STATE), 0.01)
    s5_log_dt = jax.random.uniform(next(ks), (DEPTH, S5_GROUPS), dtype=jnp.float32,
                                   minval=math.log(1e-3), maxval=math.log(1e-1))
    s5_b_re = nrm((DEPTH, S5_GROUPS, S5_STATE, S5_GROUP), (2 * S5_GROUP) ** -0.5)
    s5_b_im = nrm((DEPTH, S5_GROUPS, S5_STATE, S5_GROUP), (2 * S5_GROUP) ** -0.5)
    s5_c_re = nrm((DEPTH, S5_GROUPS, S5_GROUP, S5_STATE), S5_STATE ** -0.5)
    s5_c_im = nrm((DEPTH, S5_GROUPS, S5_GROUP, S5_STATE), S5_STATE ** -0.5)
    s5_d = nrm((DEPTH, MIX_W))
    s5_glu_w = nrm((DEPTH, MIX_W, MIX_W), MIX_W ** -0.5)
    s5_glu_b = nrm((DEPTH, MIX_W), 0.01)
    fox_q_norm = gain((DEPTH, HEAD_DIM))
    fox_k_norm = gain((DEPTH, HEAD_DIM))
    fox_f_bias = FOX_BIAS_INIT + nrm((DEPTH, FOX_HEADS), 0.5)
    w_branch = nrm((DEPTH, N_BRANCH, MIX_W, D_MODEL), MIX_W ** -0.5)
    w_out = nrm((DEPTH, D_MODEL, D_MODEL), D_MODEL ** -0.5)
    ffn_norm = gain((DEPTH, D_MODEL))
    ffn_w_up = nrm((DEPTH, D_MODEL, 2 * D_FF), D_MODEL ** -0.5)
    ffn_conv_w = nrm((DEPTH, CONV_W, D_FF), CONV_W ** -0.5)
    ffn_conv_b = nrm((DEPTH, D_FF), 0.01)
    ffn_w_down = nrm((DEPTH, D_FF, D_MODEL), D_FF ** -0.5)
    return {'x_prompt': x_prompt, 'x_sample': x_sample, 'cache_nsa_kv': cache_nsa_kv,
            'cache_nsa_win': cache_nsa_win, 'cache_fox_kv': cache_fox_kv, 'cache_fox_logf': cache_fox_logf,
            'state_hgrn': state_hgrn, 'state_s5': state_s5, 'state_ffn_conv': state_ffn_conv,
            'page_table': page_table, 'attn_norm': attn_norm, 'w_in': w_in, 'nsa_q_norm': nsa_q_norm,
            'nsa_k_norm': nsa_k_norm, 'nsa_cmp_pe': nsa_cmp_pe, 'nsa_cmp_wk': nsa_cmp_wk,
            'nsa_cmp_wv': nsa_cmp_wv, 'hgrn_lb_logits': hgrn_lb_logits, 'hgrn_o_norm': hgrn_o_norm,
            's5_a_re': s5_a_re, 's5_a_im': s5_a_im, 's5_log_dt': s5_log_dt, 's5_b_re': s5_b_re,
            's5_b_im': s5_b_im, 's5_c_re': s5_c_re, 's5_c_im': s5_c_im, 's5_d': s5_d,
            's5_glu_w': s5_glu_w, 's5_glu_b': s5_glu_b, 'fox_q_norm': fox_q_norm, 'fox_k_norm': fox_k_norm,
            'fox_f_bias': fox_f_bias, 'w_branch': w_branch, 'w_out': w_out, 'ffn_norm': ffn_norm,
            'ffn_w_up': ffn_w_up, 'ffn_conv_w': ffn_conv_w, 'ffn_conv_b': ffn_conv_b, 'ffn_w_down': ffn_w_down}


def reference(x_prompt, x_sample, cache_nsa_kv, cache_nsa_win, cache_fox_kv, cache_fox_logf, state_hgrn,
              state_s5, state_ffn_conv, page_table, attn_norm, w_in, nsa_q_norm, nsa_k_norm, nsa_cmp_pe,
              nsa_cmp_wk, nsa_cmp_wv, hgrn_lb_logits, hgrn_o_norm, s5_a_re, s5_a_im, s5_log_dt, s5_b_re,
              s5_b_im, s5_c_re, s5_c_im, s5_d, s5_glu_w, s5_glu_b, fox_q_norm, fox_k_norm, fox_f_bias,
              w_branch, w_out, ffn_norm, ffn_w_up, ffn_conv_w, ffn_conv_b, ffn_w_down):
    sm = jax.nn.softmax(hgrn_lb_logits.astype(jnp.float32), axis=0)
    lb_all = jnp.cumsum(sm, axis=0) - sm[0]
    dt = x_prompt.dtype
    bp = x_prompt.shape[0]
    yp, ys = x_prompt, x_sample
    outs_p, outs_s = [], []
    for l in range(DEPTH):
        prm = {'attn_norm': attn_norm[l], 'w_in': w_in[l], 'nsa_q_norm': nsa_q_norm[l],
               'nsa_k_norm': nsa_k_norm[l], 'nsa_cmp_pe': nsa_cmp_pe[l], 'nsa_cmp_wk': nsa_cmp_wk[l],
               'nsa_cmp_wv': nsa_cmp_wv[l], 'hgrn_o_norm': hgrn_o_norm[l], 's5_a_re': s5_a_re[l],
               's5_a_im': s5_a_im[l], 's5_log_dt': s5_log_dt[l], 's5_b_re': s5_b_re[l], 's5_b_im': s5_b_im[l],
               's5_c_re': s5_c_re[l], 's5_c_im': s5_c_im[l], 's5_d': s5_d[l], 's5_glu_w': s5_glu_w[l],
               's5_glu_b': s5_glu_b[l], 'fox_q_norm': fox_q_norm[l], 'fox_k_norm': fox_k_norm[l],
               'fox_f_bias': fox_f_bias[l], 'w_branch': w_branch[l], 'w_out': w_out[l], 'ffn_norm': ffn_norm[l],
               'ffn_w_up': ffn_w_up[l], 'ffn_conv_w': ffn_conv_w[l], 'ffn_conv_b': ffn_conv_b[l],
               'ffn_w_down': ffn_w_down[l]}
        res_p = trunk_layer(yp, 0,
                            jnp.zeros((bp, 0, 4, HEAD_DIM), dt), jnp.zeros((bp, 0, 2, HEAD_DIM), dt),
                            jnp.zeros((bp, 0, 2, FOX_HEADS, HEAD_DIM), dt), jnp.zeros((bp, 0, FOX_HEADS), jnp.float32),
                            jnp.zeros((bp, HG_HEADS, HG_DK, HG_DV), jnp.float32),
                            jnp.zeros((bp, S5_GROUPS, S5_STATE, 2), jnp.float32),
                            jnp.zeros((bp, CONV_W - 1, D_FF), dt), lb_all[l], prm)
        nsa_past = gather_pages(cache_nsa_kv[l], page_table)
        res_s = trunk_layer(ys, nsa_past.shape[1], nsa_past, cache_nsa_win[l],
                            gather_pages(cache_fox_kv[l], page_table), gather_pages(cache_fox_logf[l], page_table),
                            state_hgrn[l], state_s5[l], state_ffn_conv[l], lb_all[l], prm)
        yp, ys = res_p[0], res_s[0]
        outs_p.append(res_p[1:])
        outs_s.append(res_s[1:])
    sp = [jnp.stack([o[i] for o in outs_p], axis=0) for i in range(7)]
    ss = [jnp.stack([o[i] for o in outs_s], axis=0) for i in range(7)]
    nsa_kv_p, nsa_win_p, fox_kv_p, fox_logf_p, hgrn_p, s5_p, conv_p = sp
    nsa_kv_s, nsa_win_s, fox_kv_s, fox_logf_s, hgrn_s, s5_s, conv_s = ss
    return (yp, ys, nsa_kv_p, nsa_kv_s, nsa_win_p, nsa_win_s, fox_kv_p, fox_kv_s, fox_logf_p, fox_logf_s,
            hgrn_p, hgrn_s, s5_p, s5_s, conv_p, conv_s)
```

```python
import math
from functools import partial

import jax
import jax.numpy as jnp
from jax import lax
from jax.experimental import pallas as pl
from jax.experimental.pallas import tpu as pltpu

D_MODEL = 1024
DEPTH = 4
PAGE_SIZE = 128
HEAD_DIM = 64
N_BRANCH = 4
MIX_W = D_MODEL // N_BRANCH
NSA_HEADS = MIX_W // HEAD_DIM
CMP_LEN = 32
CMP_STRIDE = 16
SEL_BLOCK = 64
N_SEL = 16
WINDOW = 512
SEL_FORCE = 1000.0
HG_HEADS = 4
HG_DK = MIX_W // HG_HEADS
HG_DV = MIX_W // HG_HEADS
HG_CHUNK = 64
S5_GROUP = 16
S5_GROUPS = MIX_W // S5_GROUP
S5_STATE = 64
FOX_HEADS = MIX_W // HEAD_DIM
Q_BLOCK = 128
D_FF = ((8 * D_MODEL // 3 + 127) // 128) * 128
CONV_W = 3
ROPE_THETA = 10000.0
EPS = 1e-6

IN_SIZES = (NSA_HEADS * HEAD_DIM, 6 * HEAD_DIM, 3 * NSA_HEADS,
            HG_HEADS * HG_DK, HG_HEADS * HG_DK, HG_HEADS * HG_DV, HG_HEADS * HG_DV,
            MIX_W,
            FOX_HEADS * HEAD_DIM, FOX_HEADS * HEAD_DIM, FOX_HEADS * HEAD_DIM, FOX_HEADS,
            N_BRANCH * D_MODEL)

VMEM_LIMIT_BYTES = 56 * 1024 * 1024


def _split_points():
    pts, acc = [], 0
    for s in IN_SIZES[:-1]:
        acc += s
        pts.append(acc)
    return pts


def _mm_kernel(x_ref, w_ref, o_ref):
    o_ref[...] = jnp.dot(x_ref[...].astype(jnp.bfloat16), w_ref[...],
                         preferred_element_type=jnp.float32)


def matmul(x, w, tm=256, tn=None):
    m, k = x.shape
    n = w.shape[1]
    tm = min(tm, m)
    tn = n if tn is None else tn
    return pl.pallas_call(
        _mm_kernel,
        grid=(n // tn, m // tm),
        in_specs=[pl.BlockSpec((tm, k), lambda j, i: (i, 0)),
                  pl.BlockSpec((k, tn), lambda j, i: (0, j))],
        out_specs=pl.BlockSpec((tm, tn), lambda j, i: (i, j)),
        out_shape=jax.ShapeDtypeStruct((m, n), jnp.float32),
        compiler_params=pltpu.CompilerParams(
            dimension_semantics=("arbitrary", "arbitrary"),
            vmem_limit_bytes=VMEM_LIMIT_BYTES),
    )(x, w.astype(jnp.bfloat16))


def mm3(x, w, **kw):
    lead = x.shape[:-1]
    return matmul(x.reshape(-1, x.shape[-1]), w, **kw).reshape(lead + (w.shape[1],))


def rmsnorm(x, g):
    xf = x.astype(jnp.float32)
    y = xf * lax.rsqrt(jnp.mean(xf * xf, axis=-1, keepdims=True) + EPS)
    return (y * g.astype(jnp.float32)).astype(x.dtype)


def rope(x, pos):
    half = HEAD_DIM // 2
    inv = ROPE_THETA ** (-jnp.arange(half, dtype=jnp.float32) / half)
    ang = pos.astype(jnp.float32)[:, None] * inv[None, :]
    shp = (pos.shape[0],) + (1,) * (x.ndim - 3) + (half,)
    cos, sin = jnp.cos(ang).reshape(shp), jnp.sin(ang).reshape(shp)
    xf = x.astype(jnp.float32)
    x1, x2 = xf[..., :half], xf[..., half:]
    return jnp.concatenate([x1 * cos - x2 * sin, x2 * cos + x1 * sin], axis=-1).astype(x.dtype)


def masked_softmax(logits, mask):
    lg = jnp.where(mask, logits.astype(jnp.float32), -jnp.inf)
    m = jnp.max(lg, axis=-1, keepdims=True)
    m = jnp.where(jnp.isfinite(m), m, 0.0)
    e = jnp.where(mask, jnp.exp(lg - m), 0.0)
    return e / jnp.maximum(jnp.sum(e, axis=-1, keepdims=True), 1e-30)


def _to_blocks(a, qb, nb):
    pad = nb * qb - a.shape[1]
    a = jnp.pad(a, [(0, 0), (0, pad)] + [(0, 0)] * (a.ndim - 2))
    return jnp.moveaxis(a.reshape((a.shape[0], nb, qb) + a.shape[2:]), 1, 0)


def _from_blocks(a, length):
    a = jnp.moveaxis(a, 0, 1)
    return a.reshape((a.shape[0], a.shape[1] * a.shape[2]) + a.shape[3:])[:, :length]


def gather_pages(pool, page_table):
    g = pool[page_table]
    return g.reshape((g.shape[0], g.shape[1] * g.shape[2]) + g.shape[3:])


def nsa_compress(rows, pe, w):
    B, T, _ = rows.shape
    n_c = (T - CMP_LEN) // CMP_STRIDE + 1
    idx = jnp.arange(n_c)[:, None] * CMP_STRIDE + jnp.arange(CMP_LEN)[None, :]
    blocks = rows[:, idx] + pe.astype(rows.dtype)
    return blocks.reshape(B, n_c, CMP_LEN * HEAD_DIM) @ w


def nsa_mixer(q, rows, win, gates, past_len, win_pos0, k_norm_c, pe, wk, wv):
    B, Lq, H, _ = q.shape
    T = rows.shape[1]
    Tw = win.shape[1]
    scale = HEAD_DIM ** -0.5
    q_rot = rope(q, past_len + jnp.arange(Lq))
    k_c = rmsnorm(nsa_compress(rows[:, :, 0], pe, wk), k_norm_c)
    v_c = nsa_compress(rows[:, :, 1], pe, wv).astype(jnp.float32)
    n_c = k_c.shape[1]
    c_start = jnp.arange(n_c) * CMP_STRIDE
    c_end = c_start + CMP_LEN - 1
    n_s = -(-T // SEL_BLOCK)
    s_start = jnp.arange(n_s) * SEL_BLOCK
    cover = ((c_start[:, None] < s_start[None, :] + SEL_BLOCK) & (c_end[:, None] >= s_start[None, :])).astype(jnp.float32)
    sel_rows = jnp.pad(rows[:, :, 2:4], ((0, 0), (0, n_s * SEL_BLOCK - T), (0, 0), (0, 0)))
    sel_rows = sel_rows.reshape(B, n_s, SEL_BLOCK, 2, HEAD_DIM)
    k_top = min(N_SEL, n_s)
    qb = min(Q_BLOCK, Lq)
    nb = -(-Lq // qb)
    kw_all = jnp.pad(win, ((0, 0), (WINDOW, nb * qb - Lq), (0, 0), (0, 0)))
    buf_len = past_len - win_pos0
    bidx = jnp.arange(B)[:, None, None]

    def block(args):
        qn, qr, bi = args
        pos = past_len + bi * qb + jnp.arange(qb)
        pc = masked_softmax(jnp.einsum('bqhd,bcd->bhqc', qn, k_c) * scale,
                            (c_end[None, :] <= pos[:, None])[None, None])
        o_c = jnp.einsum('bhqc,bcd->bqhd', pc, v_c)
        imp = jnp.einsum('bhqc,cs->bqs', pc, cover)
        cur = pos // SEL_BLOCK
        j = jnp.arange(n_s)[None, :]
        forced = (j == 0) | (j == cur[:, None]) | (j == cur[:, None] - 1)
        score = jnp.where(j > cur[:, None], -jnp.inf, imp + SEL_FORCE * forced)
        _, sel = lax.top_k(score, k_top)
        g = sel_rows[bidx, sel]
        gpos = sel[..., None] * SEL_BLOCK + jnp.arange(SEL_BLOCK)
        ls = jnp.einsum('bqhd,bqksd->bhqks', qr, g[..., 0, :]).reshape(B, H, qb, k_top * SEL_BLOCK) * scale
        ms = (gpos <= pos[None, :, None, None]).reshape(B, 1, qb, k_top * SEL_BLOCK)
        ps = masked_softmax(ls, ms).reshape(B, H, qb, k_top, SEL_BLOCK)
        o_s = jnp.einsum('bhqks,bqksd->bqhd', ps, g[..., 1, :].astype(jnp.float32))
        start = buf_len + bi * qb
        kwin = lax.dynamic_slice_in_dim(kw_all, start, WINDOW + qb, axis=1)
        wpos = win_pos0 - WINDOW + start + jnp.arange(WINDOW + qb)
        mw = ((wpos[None, :] >= win_pos0) & (wpos[None, :] < win_pos0 + Tw)
              & (wpos[None, :] <= pos[:, None]) & (pos[:, None] - wpos[None, :] < WINDOW))
        pw = masked_softmax(jnp.einsum('bqhd,bkd->bhqk', qr, kwin[:, :, 0]) * scale, mw[None, None])
        o_w = jnp.einsum('bhqk,bkd->bqhd', pw, kwin[:, :, 1].astype(jnp.float32))
        return o_c, o_s, o_w

    o_c, o_s, o_w = lax.map(block, (_to_blocks(q, qb, nb), _to_blocks(q_rot, qb, nb), jnp.arange(nb)))
    gs = jax.nn.sigmoid(gates.astype(jnp.float32))[..., None]
    o = (gs[:, :, :, 0] * _from_blocks(o_c, Lq) + gs[:, :, :, 1] * _from_blocks(o_s, Lq)
         + gs[:, :, :, 2] * _from_blocks(o_w, Lq))
    return o.reshape(B, Lq, H * HEAD_DIM)


def fox_mixer(q, k, v, logf, past_len):
    B, Lq, H, _ = q.shape
    T = k.shape[1]
    scale = HEAD_DIM ** -0.5
    c = jnp.cumsum(logf.astype(jnp.float32), axis=1)
    cT = jnp.moveaxis(c, 2, 1)
    vf = v.astype(jnp.float32)
    kpos = jnp.arange(T)
    qb = min(Q_BLOCK, Lq)
    nb = -(-Lq // qb)

    def block(args):
        qblk, cq, bi = args
        pos = past_len + bi * qb + jnp.arange(qb)
        bias = jnp.moveaxis(cq, 2, 1)[..., None] - cT[:, :, None, :]
        lg = jnp.einsum('bqhd,bkhd->bhqk', qblk, k).astype(jnp.float32) * scale + bias
        p = masked_softmax(lg, (kpos[None, :] <= pos[:, None])[None, None])
        return jnp.einsum('bhqk,bkhd->bqhd', p, vf)

    o = lax.map(block, (_to_blocks(q, qb, nb), _to_blocks(c[:, past_len:], qb, nb), jnp.arange(nb)))
    return _from_blocks(o, Lq)


def hgrn_mixer(q, logf, k, v, s0):
    B, L, H, _ = q.shape
    C = min(HG_CHUNK, L)
    n = -(-L // C)
    pad = n * C - L

    def prep(a):
        a = jnp.pad(a.astype(jnp.float32), ((0, 0), (0, pad), (0, 0), (0, 0)))
        return jnp.moveaxis(a.reshape((B, n, C) + a.shape[2:]), 1, 0)

    tril = jnp.tril(jnp.ones((C, C), dtype=bool))[None, :, :, None, None]

    def step(S, inp):
        qc, fc, kc, vc = inp
        b = jnp.cumsum(fc, axis=1)
        diff = b[:, :, None] - b[:, None, :]
        dec = jnp.where(tril, jnp.exp(jnp.where(tril, diff, 0.0)), 0.0)
        att = jnp.einsum('btshk,bthk,bshk->bhts', dec, qc, kc)
        o = jnp.einsum('bhts,bshv->bthv', att, vc) + jnp.einsum('bthk,bhkv->bthv', qc * jnp.exp(b), S)
        bl = b[:, -1]
        S = jnp.exp(bl)[..., None] * S + jnp.einsum('bshk,bshv->bhkv', kc * jnp.exp(bl[:, None] - b), vc)
        return S, o

    S, o = lax.scan(step, s0.astype(jnp.float32), (prep(q), prep(logf), prep(k), prep(v)))
    o = jnp.moveaxis(o, 0, 1).reshape(B, n * C, H, v.shape[-1])[:, :L]
    return o, S


def s5_mixer(u, h0, a_re, a_im, log_dt, b_re, b_im, c_re, c_im, d):
    B, L, _ = u.shape
    uf = u.astype(jnp.float32)
    ug = uf.reshape(B, L, S5_GROUPS, S5_GROUP)
    dt = jnp.exp(log_dt.astype(jnp.float32))[:, None]
    ar, ai = a_re.astype(jnp.float32), a_im.astype(jnp.float32)
    mag = jnp.exp(dt * ar)
    abr, abi = mag * jnp.cos(dt * ai), mag * jnp.sin(dt * ai)
    den = ar * ar + ai * ai
    zr = ((abr - 1.0) * ar + abi * ai) / den
    zi = (abi * ar - (abr - 1.0) * ai) / den
    br_, bi_ = b_re.astype(jnp.float32), b_im.astype(jnp.float32)
    bbr = zr[..., None] * br_ - zi[..., None] * bi_
    bbi = zr[..., None] * bi_ + zi[..., None] * br_
    xr = jnp.einsum('blgc,gpc->blgp', ug, bbr)
    xi = jnp.einsum('blgc,gpc->blgp', ug, bbi)
    h0r, h0i = h0[..., 0].astype(jnp.float32), h0[..., 1].astype(jnp.float32)
    xr = xr.at[:, 0].add(abr * h0r - abi * h0i)
    xi = xi.at[:, 0].add(abr * h0i + abi * h0r)
    Ar = jnp.broadcast_to(abr, xr.shape)
    Ai = jnp.broadcast_to(abi, xi.shape)

    def comb(e1, e2):
        a1r, a1i, b1r, b1i = e1
        a2r, a2i, b2r, b2i = e2
        return (a2r * a1r - a2i * a1i, a2r * a1i + a2i * a1r,
                a2r * b1r - a2i * b1i + b2r, a2r * b1i + a2i * b1r + b2i)

    _, _, hr, hi = lax.associative_scan(comb, (Ar, Ai, xr, xi), axis=1)
    y = (jnp.einsum('gcp,blgp->blgc', c_re.astype(jnp.float32), hr)
         - jnp.einsum('gcp,blgp->blgc', c_im.astype(jnp.float32), hi))
    y = y.reshape(B, L, MIX_W) + d.astype(jnp.float32) * uf
    return y, jnp.stack([hr[:, -1], hi[:, -1]], axis=-1)


def conv_ffn(h, buf, w_up, conv_w, conv_b, w_down):
    L = h.shape[1]
    a, b = jnp.split(mm3(h, w_up), 2, axis=-1)
    ap = jnp.concatenate([buf.astype(a.dtype), a], axis=1)
    ac = conv_b + ap[:, 0:L] * conv_w[0]
    for j in range(1, CONV_W):
        ac = ac + ap[:, j:j + L] * conv_w[j]
    return mm3(jax.nn.silu(ac) * b, w_down), ap[:, L:]


def trunk_layer(x, past_len, nsa_past, win_buf, fox_past_kv, fox_past_logf, hg_state, s5_state, conv_buf, lb, p):
    B, L, _ = x.shape
    dt = x.dtype
    pos = past_len + jnp.arange(L)
    h = rmsnorm(x, p['attn_norm'])
    (nq, nkv, ng, hq, hf, hi, hgate, su, fq, fk, fv, ff, bg) = jnp.split(mm3(h, p['w_in']), _split_points(), axis=-1)
    nq = rmsnorm(nq.reshape(B, L, NSA_HEADS, HEAD_DIM), p['nsa_q_norm'])
    nkv = nkv.reshape(B, L, 6, HEAD_DIM)
    k_slc = rope(rmsnorm(nkv[:, :, 2], p['nsa_k_norm'][1]), pos)
    k_win = rope(rmsnorm(nkv[:, :, 4], p['nsa_k_norm'][2]), pos)
    nsa_rows = jnp.stack([nkv[:, :, 0], nkv[:, :, 1], k_slc, nkv[:, :, 3]], axis=2)
    win_rows = jnp.stack([k_win, nkv[:, :, 5]], axis=2)
    rows_full = jnp.concatenate([nsa_past.astype(dt), nsa_rows], axis=1)
    win_full = jnp.concatenate([win_buf.astype(dt), win_rows], axis=1)
    o_nsa = nsa_mixer(nq, rows_full, win_full, ng.reshape(B, L, NSA_HEADS, 3), past_len,
                      past_len - win_buf.shape[1], p['nsa_k_norm'][0], p['nsa_cmp_pe'],
                      p['nsa_cmp_wk'], p['nsa_cmp_wv'])
    new_win = win_full[:, win_full.shape[1] - min(WINDOW, win_full.shape[1]):]
    lbh = lb.reshape(HG_HEADS, HG_DK)
    f = lbh + (1.0 - lbh) * jax.nn.sigmoid(hf.reshape(B, L, HG_HEADS, HG_DK).astype(jnp.float32))
    o_h, hg_new = hgrn_mixer(hq.reshape(B, L, HG_HEADS, HG_DK), jnp.log(f), 1.0 - f,
                             hi.reshape(B, L, HG_HEADS, HG_DV), hg_state)
    o_h = rmsnorm(o_h, p['hgrn_o_norm']) * jax.nn.silu(hgate.reshape(B, L, HG_HEADS, HG_DV).astype(jnp.float32))
    y5, s5_new = s5_mixer(su, s5_state, p['s5_a_re'], p['s5_a_im'], p['s5_log_dt'], p['s5_b_re'],
                          p['s5_b_im'], p['s5_c_re'], p['s5_c_im'], p['s5_d'])
    z5 = jax.nn.gelu(y5)
    o_s5 = z5 * jax.nn.sigmoid(z5 @ p['s5_glu_w'].astype(jnp.float32) + p['s5_glu_b'].astype(jnp.float32))
    fq = rmsnorm(fq.reshape(B, L, FOX_HEADS, HEAD_DIM), p['fox_q_norm'])
    fk = rmsnorm(fk.reshape(B, L, FOX_HEADS, HEAD_DIM), p['fox_k_norm'])
    fv = fv.reshape(B, L, FOX_HEADS, HEAD_DIM)
    flog = jax.nn.log_sigmoid(ff.astype(jnp.float32) + p['fox_f_bias'].astype(jnp.float32))
    fox_rows = jnp.stack([fk, fv], axis=2)
    kv_full = jnp.concatenate([fox_past_kv.astype(dt), fox_rows], axis=1)
    lf_full = jnp.concatenate([fox_past_logf.astype(jnp.float32), flog], axis=1)
    o_fox = fox_mixer(fq, kv_full[:, :, 0], kv_full[:, :, 1], lf_full, past_len)
    branches = jnp.stack([o_nsa, o_h.reshape(B, L, MIX_W), o_s5, o_fox.reshape(B, L, MIX_W)], axis=2).astype(dt)
    gates = jax.nn.sigmoid(bg.reshape(B, L, N_BRANCH, D_MODEL).astype(jnp.float32)).astype(dt)
    proj = jnp.einsum('blnc,ncd->blnd', branches, p['w_branch'])
    merged = jnp.einsum('blnd,blnd->bld', gates, proj)
    x = x + mm3(merged, p['w_out']).astype(dt)
    f_out, conv_new = conv_ffn(rmsnorm(x, p['ffn_norm']), conv_buf, p['ffn_w_up'], p['ffn_conv_w'],
                               p['ffn_conv_b'], p['ffn_w_down'])
    x = x + f_out.astype(dt)
    return x, nsa_rows, new_win, fox_rows, flog, hg_new, s5_new, conv_new


PARAM_NAMES = ('attn_norm', 'w_in', 'nsa_q_norm', 'nsa_k_norm', 'nsa_cmp_pe', 'nsa_cmp_wk', 'nsa_cmp_wv',
               'hgrn_o_norm', 's5_a_re', 's5_a_im', 's5_log_dt', 's5_b_re', 's5_b_im', 's5_c_re', 's5_c_im',
               's5_d', 's5_glu_w', 's5_glu_b', 'fox_q_norm', 'fox_k_norm', 'fox_f_bias', 'w_branch', 'w_out',
               'ffn_norm', 'ffn_w_up', 'ffn_conv_w', 'ffn_conv_b', 'ffn_w_down')


def kernel(x_prompt, x_sample, cache_nsa_kv, cache_nsa_win, cache_fox_kv, cache_fox_logf, state_hgrn, state_s5, state_ffn_conv, page_table, attn_norm, w_in, nsa_q_norm, nsa_k_norm, nsa_cmp_pe, nsa_cmp_wk, nsa_cmp_wv, hgrn_lb_logits, hgrn_o_norm, s5_a_re, s5_a_im, s5_log_dt, s5_b_re, s5_b_im, s5_c_re, s5_c_im, s5_d, s5_glu_w, s5_glu_b, fox_q_norm, fox_k_norm, fox_f_bias, w_branch, w_out, ffn_norm, ffn_w_up, ffn_conv_w, ffn_conv_b, ffn_w_down):
    params = dict(attn_norm=attn_norm, w_in=w_in, nsa_q_norm=nsa_q_norm, nsa_k_norm=nsa_k_norm,
                  nsa_cmp_pe=nsa_cmp_pe, nsa_cmp_wk=nsa_cmp_wk, nsa_cmp_wv=nsa_cmp_wv, hgrn_o_norm=hgrn_o_norm,
                  s5_a_re=s5_a_re, s5_a_im=s5_a_im, s5_log_dt=s5_log_dt, s5_b_re=s5_b_re, s5_b_im=s5_b_im,
                  s5_c_re=s5_c_re, s5_c_im=s5_c_im, s5_d=s5_d, s5_glu_w=s5_glu_w, s5_glu_b=s5_glu_b,
                  fox_q_norm=fox_q_norm, fox_k_norm=fox_k_norm, fox_f_bias=fox_f_bias, w_branch=w_branch,
                  w_out=w_out, ffn_norm=ffn_norm, ffn_w_up=ffn_w_up, ffn_conv_w=ffn_conv_w,
                  ffn_conv_b=ffn_conv_b, ffn_w_down=ffn_w_down)
    sm = jax.nn.softmax(hgrn_lb_logits.astype(jnp.float32), axis=0)
    lb_all = jnp.cumsum(sm, axis=0) - sm[0]
    dt = x_prompt.dtype
    bp = x_prompt.shape[0]
    yp, ys = x_prompt, x_sample
    outs_p, outs_s = [], []
    for l in range(DEPTH):
        prm = {k: v[l] for k, v in params.items()}
        res_p = trunk_layer(yp, 0,
                            jnp.zeros((bp, 0, 4, HEAD_DIM), dt), jnp.zeros((bp, 0, 2, HEAD_DIM), dt),
                            jnp.zeros((bp, 0, 2, FOX_HEADS, HEAD_DIM), dt), jnp.zeros((bp, 0, FOX_HEADS), jnp.float32),
                            jnp.zeros((bp, HG_HEADS, HG_DK, HG_DV), jnp.float32),
                            jnp.zeros((bp, S5_GROUPS, S5_STATE, 2), jnp.float32),
                            jnp.zeros((bp, CONV_W - 1, D_FF), dt), lb_all[l], prm)
        nsa_past = gather_pages(cache_nsa_kv[l], page_table)
        res_s = trunk_layer(ys, nsa_past.shape[1], nsa_past, cache_nsa_win[l],
                            gather_pages(cache_fox_kv[l], page_table), gather_pages(cache_fox_logf[l], page_table),
                            state_hgrn[l], state_s5[l], state_ffn_conv[l], lb_all[l], prm)
        yp, ys = res_p[0], res_s[0]
        outs_p.append(res_p[1:])
        outs_s.append(res_s[1:])
    sp = [jnp.stack([o[i] for o in outs_p], axis=0) for i in range(7)]
    ss = [jnp.stack([o[i] for o in outs_s], axis=0) for i in range(7)]
    nsa_kv_p, nsa_win_p, fox_kv_p, fox_logf_p, hgrn_p, s5_p, conv_p = sp
    nsa_kv_s, nsa_win_s, fox_kv_s, fox_logf_s, hgrn_s, s5_s, conv_s = ss
    return (yp, ys, nsa_kv_p, nsa_kv_s, nsa_win_p, nsa_win_s, fox_kv_p, fox_kv_s, fox_logf_p, fox_logf_s,
            hgrn_p, hgrn_s, s5_p, s5_s, conv_p, conv_s)
```

```python
import math
from functools import partial

import numpy as np
import jax
import jax.numpy as jnp
from jax import lax
from jax.experimental import pallas as pl
from jax.experimental.pallas import tpu as pltpu

D_MODEL = 1024
DEPTH = 4
PAGE_SIZE = 128
HEAD_DIM = 64
N_BRANCH = 4
MIX_W = D_MODEL // N_BRANCH
NSA_HEADS = MIX_W // HEAD_DIM
CMP_LEN = 32
CMP_STRIDE = 16
SEL_BLOCK = 64
N_SEL = 16
WINDOW = 512
SEL_FORCE = 1000.0
HG_HEADS = 4
HG_DK = MIX_W // HG_HEADS
HG_DV = MIX_W // HG_HEADS
HG_CHUNK = 64
S5_GROUP = 16
S5_GROUPS = MIX_W // S5_GROUP
S5_STATE = 64
S5_LANES = S5_GROUPS * S5_STATE
FOX_HEADS = MIX_W // HEAD_DIM
Q_BLOCK = 128
D_FF = ((8 * D_MODEL // 3 + 127) // 128) * 128
CONV_W = 3
ROPE_THETA = 10000.0
EPS = 1e-6
SCALE = HEAD_DIM ** -0.5
NEG = -1e30

IN_SIZES = (NSA_HEADS * HEAD_DIM, 6 * HEAD_DIM, 3 * NSA_HEADS,
            HG_HEADS * HG_DK, HG_HEADS * HG_DK, HG_HEADS * HG_DV, HG_HEADS * HG_DV,
            MIX_W,
            FOX_HEADS * HEAD_DIM, FOX_HEADS * HEAD_DIM, FOX_HEADS * HEAD_DIM, FOX_HEADS,
            N_BRANCH * D_MODEL)

LANES = 128
SUBLANES = 8
VMEM_LIMIT_BYTES = 56 * 1024 * 1024

F32 = jnp.float32
BF16 = jnp.bfloat16

C_NQ, C_NKV, C_NG, C_HG, C_SU, C_FQ, C_FK, C_FV, C_FF, C_END = 0, 256, 640, 768, 1792, 2048, 2304, 2560, 2816, 2944


def _cparams(sem):
    return pltpu.CompilerParams(dimension_semantics=sem, vmem_limit_bytes=VMEM_LIMIT_BYTES)


def _split2(x):
    hi = x.astype(BF16)
    lo = (x - hi.astype(F32)).astype(BF16)
    return hi, lo


def _split3(x):
    hi = x.astype(BF16)
    r = x - hi.astype(F32)
    mid = r.astype(BF16)
    lo = (r - mid.astype(F32)).astype(BF16)
    return hi, mid, lo


def _dot(a, b):
    return jnp.dot(a, b, preferred_element_type=F32)


def _dot_nt(a, b):
    return lax.dot_general(a, b, (((1,), (1,)), ((), ())), preferred_element_type=F32)


def _dot_tn(a, b):
    return lax.dot_general(a, b, (((0,), (0,)), ((), ())), preferred_element_type=F32)


def _dot_exact_lhs(a01, x):
    hi, mid, lo = _split3(x)
    return _dot(a01, hi) + _dot(a01, mid) + _dot(a01, lo)


def _dot_exact_rhs(x, b01):
    hi, mid, lo = _split3(x)
    return _dot(hi, b01) + _dot(mid, b01) + _dot(lo, b01)


def _seg_rms(x, bd, gain):
    ms = _dot_exact_rhs(x * x, bd) * (1.0 / HEAD_DIM)
    return x * lax.rsqrt(ms + EPS) * gain


def _rope(x, cos, sin_signed):
    w = x.shape[-1]
    half = HEAD_DIM // 2
    lane = lax.broadcasted_iota(jnp.int32, x.shape, 1)
    first = (lane % HEAD_DIM) < half
    swapped = jnp.where(first, pltpu.roll(x, w - half, 1), pltpu.roll(x, half, 1))
    return x * cos + swapped * sin_signed


def _log_sigmoid(x):
    return jnp.minimum(x, 0.0) - jnp.log(1.0 + jnp.exp(-jnp.abs(x)))


def _sigmoid(x):
    return 1.0 / (1.0 + jnp.exp(-x))


def _silu(x):
    return x * _sigmoid(x)


def _mm_kernel(x_ref, w_ref, o_ref):
    o_ref[...] = _dot(x_ref[...].astype(BF16), w_ref[...])


def matmul(x, w, tm=256):
    m, k = x.shape
    n = w.shape[1]
    tm = min(tm, m)
    return pl.pallas_call(
        _mm_kernel,
        grid=(m // tm,),
        in_specs=[pl.BlockSpec((tm, k), lambda i: (i, 0)),
                  pl.BlockSpec((k, n), lambda i: (0, 0))],
        out_specs=pl.BlockSpec((tm, n), lambda i: (i, 0)),
        out_shape=jax.ShapeDtypeStruct((m, n), F32),
        compiler_params=_cparams(("arbitrary",)),
        name="matmul",
    )(x, w.astype(BF16))


def _in_proj_kernel(x_ref, g_ref, w_ref, cos_ref, sin_ref, bd_ref, gq_ref, gk_ref, gfq_ref, gfk_ref, fb_ref,
                    h_ref, qn_ref, qr_ref, rows_ref, win_ref, ng_ref, hg_ref, su_ref, fq_ref, fox_ref, flog_ref):
    x = x_ref[...]
    h = x * lax.rsqrt(jnp.mean(x * x, axis=-1, keepdims=True) + EPS) * g_ref[...]
    hb = h.astype(BF16)
    h_ref[...] = hb
    p = _dot(hb, w_ref[...])
    bd = bd_ref[...]
    cos, sin = cos_ref[...], sin_ref[...]
    qn = _seg_rms(p[:, C_NQ:C_NKV], bd, gq_ref[...])
    qn_ref[...] = qn
    qr_ref[...] = _rope(qn, cos, sin)
    raw = p[:, C_NKV:C_NKV + 256]
    lane = lax.broadcasted_iota(jnp.int32, raw.shape, 1)
    nr = _rope(_seg_rms(raw, bd, gk_ref[:, 0:256]), cos, sin)
    rows_ref[...] = jnp.where((lane >= 2 * HEAD_DIM) & (lane < 3 * HEAD_DIM), nr, raw)
    raww = p[:, C_NKV + 256:C_NG]
    nw = _rope(_seg_rms(raww, bd[0:128, 0:128], gk_ref[:, 256:384]), cos[:, 0:128], sin[:, 0:128])
    lane_w = lax.broadcasted_iota(jnp.int32, raww.shape, 1)
    win_ref[...] = jnp.where(lane_w < HEAD_DIM, nw, raww)
    ng_ref[...] = p[:, C_NG:C_HG]
    hg_ref[...] = p[:, C_HG:C_SU]
    su_ref[...] = p[:, C_SU:C_FQ]
    fq_ref[...] = _seg_rms(p[:, C_FQ:C_FK], bd, gfq_ref[...])
    fox_ref[:, 0:256] = _seg_rms(p[:, C_FK:C_FV], bd, gfk_ref[...])
    fox_ref[:, 256:512] = p[:, C_FV:C_FF]
    flog_ref[...] = _log_sigmoid(p[:, C_FF:C_END] + fb_ref[...])


def in_proj(x, lw, bd256, cos_t, sin_t, tm):
    t = x.shape[0]
    nt = cos_t.shape[0] // tm
    row = lambda i: (i, 0)
    const = lambda i: (0, 0)
    tab = lambda i: (i % nt, 0)
    widths = (256, 256, 256, 128, 128, 1024, 256, 256, 512, 128)
    out_shape = [jax.ShapeDtypeStruct((t, D_MODEL), BF16)] + [jax.ShapeDtypeStruct((t, w), F32) for w in widths]
    out_specs = [pl.BlockSpec((tm, D_MODEL), row)] + [pl.BlockSpec((tm, w), row) for w in widths]
    return pl.pallas_call(
        _in_proj_kernel,
        grid=(t // tm,),
        in_specs=[pl.BlockSpec((tm, D_MODEL), row), pl.BlockSpec((1, D_MODEL), const),
                  pl.BlockSpec((D_MODEL, C_END), const), pl.BlockSpec((tm, 256), tab), pl.BlockSpec((tm, 256), tab),
                  pl.BlockSpec((256, 256), const), pl.BlockSpec((1, 256), const), pl.BlockSpec((1, 384), const),
                  pl.BlockSpec((1, 256), const), pl.BlockSpec((1, 256), const), pl.BlockSpec((1, 128), const)],
        out_specs=out_specs,
        out_shape=out_shape,
        compiler_params=_cparams(("arbitrary",)),
        name="in_proj",
    )(x, lw['attn_norm'], lw['w_in'], cos_t, sin_t, bd256, lw['gq'], lw['gk'], lw['gfq'], lw['gfk'], lw['fbias'])


def _topk_mask(score, k):
    lane = lax.broadcasted_iota(jnp.int32, score.shape, 1)
    sel = jnp.zeros(score.shape, F32)
    s = score
    for _ in range(k):
        m = jnp.max(s, axis=1, keepdims=True)
        idx = jnp.min(jnp.where(s == m, lane, LANES), axis=1, keepdims=True)
        pick = lane == idx
        sel = jnp.where(pick, 1.0, sel)
        s = jnp.where(pick, NEG, s)
    return sel


def _cmp_tokens(p, pew, gain_c, n_c):
    n16 = p.shape[0]
    nxt = pltpu.roll(p, n16 - 1, 0)
    kc = p[:, 0:64] + nxt[:, 128:192] + pew[0:1, 0:64]
    vc = p[:, 64:128] + nxt[:, 192:256] + pew[0:1, 64:128]
    kc = kc * lax.rsqrt(jnp.mean(kc * kc, axis=-1, keepdims=True) + EPS) * gain_c
    return kc, vc


def _cmp_attend(q, kc, vc, valid):
    qh, ql = _split2(q)
    kh, kl = _split2(kc)
    s = _dot_nt(qh, kh) + _dot_nt(ql, kh) + _dot_nt(qh, kl)
    s = jnp.where(valid, s, NEG)
    m = jnp.max(s, axis=1, keepdims=True)
    e = jnp.where(valid, jnp.exp(s - m), 0.0)
    pc = e / jnp.maximum(jnp.sum(e, axis=1, keepdims=True), 1e-30)
    return pc, _dot(pc.astype(BF16), vc.astype(BF16))


def _cmp_kernel(q_ref, p_ref, pe_ref, wkv_ref, gc_ref, cover_ref, oc_ref, mask_ref, kc_ref, vc_ref, *, tq, n_c):
    qi = pl.program_id(1)

    @pl.when(qi == 0)
    def _():
        pew = _dot(pe_ref[...], wkv_ref[...])
        kc, vc = _cmp_tokens(p_ref[...], pew, gc_ref[...], n_c)
        kc_ref[...] = kc
        vc_ref[...] = vc

    kc, vc = kc_ref[...], vc_ref[...]
    n16 = kc.shape[0]
    pos = qi * tq + lax.broadcasted_iota(jnp.int32, (tq, n16), 0)
    ci = lax.broadcasted_iota(jnp.int32, (tq, n16), 1)
    valid = (ci * CMP_STRIDE + CMP_LEN - 1 <= pos) & (ci < n_c)
    q = q_ref[...]
    pcs = jnp.zeros((tq, n16), F32)
    outs = []
    for h in range(NSA_HEADS):
        pc, o = _cmp_attend(q[:, h * HEAD_DIM:(h + 1) * HEAD_DIM] * SCALE, kc, vc, valid)
        pcs = pcs + pc
        outs.append(o)
    oc_ref[...] = jnp.concatenate(outs, axis=1)
    imp = _dot_exact_rhs(pcs, cover_ref[...])
    j = lax.broadcasted_iota(jnp.int32, (tq, LANES), 1)
    cur = (qi * tq + lax.broadcasted_iota(jnp.int32, (tq, LANES), 0)) // SEL_BLOCK
    forced = (j == 0) | (j == cur) | (j == cur - 1)
    score = jnp.where(j > cur, NEG, imp + jnp.where(forced, SEL_FORCE, 0.0))
    mask_ref[...] = _topk_mask(score, N_SEL)


def nsa_cmp_prompt(qn, pparts, lw, cover, tq=256):
    b, l, _ = qn.shape
    n16 = l // CMP_STRIDE
    n_c = n16 - 1
    return pl.pallas_call(
        partial(_cmp_kernel, tq=tq, n_c=n_c),
        grid=(b, l // tq),
        in_specs=[pl.BlockSpec((None, tq, 256), lambda bi, qi: (bi, qi, 0)),
                  pl.BlockSpec((None, n16, 256), lambda bi, qi: (bi, 0, 0)),
                  pl.BlockSpec((SUBLANES, CMP_LEN * HEAD_DIM), lambda bi, qi: (0, 0)),
                  pl.BlockSpec((CMP_LEN * HEAD_DIM, 128), lambda bi, qi: (0, 0)),
                  pl.BlockSpec((1, HEAD_DIM), lambda bi, qi: (0, 0)),
                  pl.BlockSpec((n16, LANES), lambda bi, qi: (0, 0))],
        out_specs=[pl.BlockSpec((None, tq, 256), lambda bi, qi: (bi, qi, 0)),
                   pl.BlockSpec((None, tq, LANES), lambda bi, qi: (bi, qi, 0))],
        out_shape=[jax.ShapeDtypeStruct((b, l, 256), F32), jax.ShapeDtypeStruct((b, l, LANES), F32)],
        scratch_shapes=[pltpu.VMEM((n16, HEAD_DIM), F32), pltpu.VMEM((n16, HEAD_DIM), F32)],
        compiler_params=_cparams(("arbitrary", "arbitrary")),
        name="nsa_cmp",
    )(qn, pparts, lw['pe8'], lw['wkv'], lw['gc'], cover)


def _online_update(s, mask, v, m_ref, l_ref, acc_ref, h):
    s = jnp.where(mask, s, NEG)
    m_old = m_ref[h]
    m_new = jnp.maximum(m_old, jnp.max(s, axis=1, keepdims=True))
    alpha = jnp.exp(m_old - m_new)
    p = jnp.where(mask, jnp.exp(s - m_new), 0.0)
    l_ref[h] = alpha * l_ref[h] + jnp.sum(p, axis=1, keepdims=True)
    acc_ref[h] = alpha * acc_ref[h] + _dot(p.astype(BF16), v)
    m_ref[h] = m_new


def _flash_init(m_refs, l_refs, acc_refs):
    for r in m_refs:
        r[...] = jnp.full(r.shape, NEG, F32)
    for r in l_refs + acc_refs:
        r[...] = jnp.zeros(r.shape, F32)


def _nsa_flash_kernel(q_ref, rows_ref, win_ref, mask_ref, e_ref, ng_ref, oc_ref, o_ref,
                      qs_ref, ms_ref, ls_ref, as_ref, mw_ref, lw_ref, aw_ref, *, tq, tk):
    qi, kj = pl.program_id(1), pl.program_id(2)
    nk = pl.num_programs(2)

    @pl.when(kj == 0)
    def _():
        _flash_init([ms_ref, mw_ref], [ls_ref, lw_ref], [as_ref, aw_ref])
        q = q_ref[...]
        for h in range(NSA_HEADS):
            qs_ref[h] = (q[:, h * HEAD_DIM:(h + 1) * HEAD_DIM] * SCALE).astype(BF16)

    q_lo = qi * tq
    k_lo = kj * tk
    qpos = q_lo + lax.broadcasted_iota(jnp.int32, (tq, tk), 0)
    kpos = k_lo + lax.broadcasted_iota(jnp.int32, (tq, tk), 1)
    causal = kpos <= qpos

    @pl.when(k_lo <= q_lo + tq - 1)
    def _():
        rows = rows_ref[...]
        ks = rows[:, 2 * HEAD_DIM:3 * HEAD_DIM].astype(BF16)
        vs = rows[:, 3 * HEAD_DIM:4 * HEAD_DIM].astype(BF16)
        sel = _dot(mask_ref[...].astype(BF16), e_ref[...]) > 0.5
        mask = sel & causal
        for h in range(NSA_HEADS):
            _online_update(_dot_nt(qs_ref[h], ks), mask, vs, ms_ref, ls_ref, as_ref, h)

    @pl.when((k_lo <= q_lo + tq - 1) & (k_lo + tk - 1 > q_lo - WINDOW))
    def _():
        win = win_ref[...]
        kw = win[:, 0:HEAD_DIM].astype(BF16)
        vw = win[:, HEAD_DIM:2 * HEAD_DIM].astype(BF16)
        mask = causal & (qpos - kpos < WINDOW)
        for h in range(NSA_HEADS):
            _online_update(_dot_nt(qs_ref[h], kw), mask, vw, mw_ref, lw_ref, aw_ref, h)

    @pl.when(kj == nk - 1)
    def _():
        gs = _sigmoid(ng_ref[...])
        oc = oc_ref[...]
        outs = []
        for h in range(NSA_HEADS):
            o_s = as_ref[h] / jnp.maximum(ls_ref[h], 1e-30)
            o_w = aw_ref[h] / jnp.maximum(lw_ref[h], 1e-30)
            outs.append(gs[:, 3 * h:3 * h + 1] * oc[:, h * HEAD_DIM:(h + 1) * HEAD_DIM]
                        + gs[:, 3 * h + 1:3 * h + 2] * o_s + gs[:, 3 * h + 2:3 * h + 3] * o_w)
        o_ref[...] = jnp.concatenate(outs, axis=1)


def nsa_flash_prompt(qr, rows, win, mask, expand, ng, oc, tq=256, tk=256):
    b, l, _ = qr.shape
    nq, nk = l // tq, l // tk
    back = (WINDOW + tk - 1) // tk

    def kv_idx(bi, qi, kj):
        return (bi, jnp.minimum(kj, (qi * tq + tq - 1) // tk), 0)

    def win_idx(bi, qi, kj):
        hi = (qi * tq + tq - 1) // tk
        lo = jnp.maximum((qi * tq) // tk - back, 0)
        return (bi, jnp.clip(kj, lo, hi), 0)

    qmap = lambda bi, qi, kj: (bi, qi, 0)
    hshape = (NSA_HEADS, tq, 1)
    return pl.pallas_call(
        partial(_nsa_flash_kernel, tq=tq, tk=tk),
        grid=(b, nq, nk),
        in_specs=[pl.BlockSpec((None, tq, 256), qmap),
                  pl.BlockSpec((None, tk, 256), kv_idx),
                  pl.BlockSpec((None, tk, 128), win_idx),
                  pl.BlockSpec((None, tq, LANES), qmap),
                  pl.BlockSpec((LANES, tk), lambda bi, qi, kj: (0, jnp.minimum(kj, (qi * tq + tq - 1) // tk))),
                  pl.BlockSpec((None, tq, LANES), qmap),
                  pl.BlockSpec((None, tq, 256), qmap)],
        out_specs=pl.BlockSpec((None, tq, 256), qmap),
        out_shape=jax.ShapeDtypeStruct((b, l, 256), F32),
        scratch_shapes=[pltpu.VMEM((NSA_HEADS, tq, HEAD_DIM), BF16),
                        pltpu.VMEM(hshape, F32), pltpu.VMEM(hshape, F32), pltpu.VMEM((NSA_HEADS, tq, HEAD_DIM), F32),
                        pltpu.VMEM(hshape, F32), pltpu.VMEM(hshape, F32), pltpu.VMEM((NSA_HEADS, tq, HEAD_DIM), F32)],
        compiler_params=_cparams(("arbitrary", "arbitrary", "arbitrary")),
        name="nsa_flash",
    )(qr, rows, win, mask, expand, ng, oc)


def _cumsum_kernel(x_ref, tri_ref, o_ref, carry_ref):
    @pl.when(pl.program_id(1) == 0)
    def _():
        carry_ref[...] = jnp.zeros(carry_ref.shape, F32)

    c = _dot_exact_lhs(tri_ref[...], x_ref[...]) + carry_ref[0:1, :]
    o_ref[...] = c
    carry_ref[0:1, :] = c[c.shape[0] - 1:c.shape[0], :]


def cumsum_time(x, tri):
    b, l, w = x.shape
    tc = tri.shape[0]
    return pl.pallas_call(
        _cumsum_kernel,
        grid=(b, l // tc),
        in_specs=[pl.BlockSpec((None, tc, w), lambda bi, i: (bi, i, 0)),
                  pl.BlockSpec((tc, tc), lambda bi, i: (0, 0))],
        out_specs=pl.BlockSpec((None, tc, w), lambda bi, i: (bi, i, 0)),
        out_shape=jax.ShapeDtypeStruct((b, l, w), F32),
        scratch_shapes=[pltpu.VMEM((SUBLANES, w), F32)],
        compiler_params=_cparams(("arbitrary", "arbitrary")),
        name="cumsum_time",
    )(x, tri)


def _fox_flash_kernel(q_ref, kv_ref, cq_ref, ck_ref, o_ref, qs_ref, m_ref, l_ref, acc_ref, *, tq, tk):
    qi, kj = pl.program_id(1), pl.program_id(2)
    nk = pl.num_programs(2)

    @pl.when(kj == 0)
    def _():
        _flash_init([m_ref], [l_ref], [acc_ref])
        q = q_ref[...]
        for h in range(FOX_HEADS):
            qs_ref[h] = (q[:, h * HEAD_DIM:(h + 1) * HEAD_DIM] * SCALE).astype(BF16)

    q_lo = qi * tq
    k_lo = kj * tk

    @pl.when(k_lo <= q_lo + tq - 1)
    def _():
        qpos = q_lo + lax.broadcasted_iota(jnp.int32, (tq, tk), 0)
        kpos = k_lo + lax.broadcasted_iota(jnp.int32, (tq, tk), 1)
        causal = kpos <= qpos
        kv = kv_ref[...]
        cq = cq_ref[...]
        ck = ck_ref[...]
        for h in range(FOX_HEADS):
            k = kv[:, h * HEAD_DIM:(h + 1) * HEAD_DIM].astype(BF16)
            v = kv[:, 256 + h * HEAD_DIM:256 + (h + 1) * HEAD_DIM].astype(BF16)
            s = _dot_nt(qs_ref[h], k) + (cq[:, h:h + 1] - ck[h:h + 1, :])
            _online_update(s, causal, v, m_ref, l_ref, acc_ref, h)

    @pl.when(kj == nk - 1)
    def _():
        o_ref[...] = jnp.concatenate(
            [acc_ref[h] / jnp.maximum(l_ref[h], 1e-30) for h in range(FOX_HEADS)], axis=1)


def fox_flash_prompt(fq, fox_rows, c, c_t, tq=256, tk=256):
    b, l, _ = fq.shape
    nq, nk = l // tq, l // tk
    qmap = lambda bi, qi, kj: (bi, qi, 0)
    last = lambda qi: (qi * tq + tq - 1) // tk
    hshape = (FOX_HEADS, tq, 1)
    return pl.pallas_call(
        partial(_fox_flash_kernel, tq=tq, tk=tk),
        grid=(b, nq, nk),
        in_specs=[pl.BlockSpec((None, tq, 256), qmap),
                  pl.BlockSpec((None, tk, 512), lambda bi, qi, kj: (bi, jnp.minimum(kj, last(qi)), 0)),
                  pl.BlockSpec((None, tq, LANES), qmap),
                  pl.BlockSpec((None, SUBLANES, tk), lambda bi, qi, kj: (bi, 0, jnp.minimum(kj, last(qi))))],
        out_specs=pl.BlockSpec((None, tq, 256), qmap),
        out_shape=jax.ShapeDtypeStruct((b, l, 256), F32),
        scratch_shapes=[pltpu.VMEM((FOX_HEADS, tq, HEAD_DIM), BF16),
                        pltpu.VMEM(hshape, F32), pltpu.VMEM(hshape, F32), pltpu.VMEM((FOX_HEADS, tq, HEAD_DIM), F32)],
        compiler_params=_cparams(("arbitrary", "arbitrary", "arbitrary")),
        name="fox_flash",
    )(fq, fox_rows, c, c_t)


HG_LEVELS = int(math.log2(HG_CHUNK))


def _hgrn_constants():
    c = HG_CHUNK
    t = np.arange(c)[:, None]
    u = np.arange(c)[None, :]
    mats = [(u <= t), (u > t)]
    masks = []
    for lv in range(HG_LEVELS):
        bit = (t >> lv) & 1
        p_t = (t >> lv) << lv
        mats.append((bit == 1) & (u >= p_t) & (u <= t))
    for lv in range(HG_LEVELS):
        bit = (t >> lv) & 1
        p_s = ((t >> lv) + 1) << lv
        mats.append((bit == 0) & (u > t) & (u <= p_s - 1))
    for lv in range(HG_LEVELS):
        masks.append(((t >> (lv + 1)) == (u >> (lv + 1))) & (((t >> lv) & 1) == 1) & (((u >> lv) & 1) == 0))
    masks.append(t == u)
    cst = np.concatenate([m.astype(np.float32) for m in mats], axis=0)
    msk = np.stack([m.astype(np.float32) for m in masks], axis=0)
    return cst, msk


def _hgrn_kernel(hg_ref, lb_ref, cst_ref, msk_ref, eye_ref, bd_ref, go_ref, o_ref, st_ref, s_ref, *, tt):
    i = pl.program_id(1)
    c = HG_CHUNK

    @pl.when(i == 0)
    def _():
        s_ref[...] = jnp.zeros(s_ref.shape, F32)

    lb = lb_ref[...]
    for ch in range(tt // c):
        r = ch * c
        hq = hg_ref[r:r + c, 0:256]
        hf = hg_ref[r:r + c, 256:512]
        hv = hg_ref[r:r + c, 512:768]
        hgate = hg_ref[r:r + c, 768:1024]
        f = lb + (1.0 - lb) * _sigmoid(hf)
        lf = jnp.log(f)
        kk = 1.0 - f
        e = _dot_exact_lhs(cst_ref[...], lf)
        eb = jnp.exp(e[0:c])
        esu = jnp.exp(e[c:2 * c])
        ebl = eb[c - 1:c, :]
        qf = [(hq * jnp.exp(e[(2 + lv) * c:(3 + lv) * c])).astype(BF16) for lv in range(HG_LEVELS)]
        kg = [(kk * jnp.exp(e[(2 + HG_LEVELS + lv) * c:(3 + HG_LEVELS + lv) * c])).astype(BF16)
              for lv in range(HG_LEVELS)]
        qb, kb, vb = hq.astype(BF16), kk.astype(BF16), hv.astype(BF16)
        qe, ke = (hq * eb).astype(BF16), (kk * esu).astype(BF16)
        outs = []
        for h in range(HG_HEADS):
            sl = slice(h * HG_DK, (h + 1) * HG_DK)
            att = msk_ref[HG_LEVELS] * _dot_nt(qb[:, sl], kb[:, sl])
            for lv in range(HG_LEVELS):
                att = att + msk_ref[lv] * _dot_nt(qf[lv][:, sl], kg[lv][:, sl])
            st = s_ref[h]
            outs.append(_dot(att.astype(BF16), vb[:, sl]) + _dot_nt(qe[:, sl], st.astype(BF16)))
            s_ref[h] = st * ebl[:, sl] + _dot_tn(vb[:, sl], ke[:, sl])
        o = jnp.concatenate(outs, axis=1)
        o_ref[r:r + c, :] = _seg_rms(o, bd_ref[...], go_ref[...]) * _silu(hgate)

    @pl.when(i == pl.num_programs(1) - 1)
    def _():
        for h in range(HG_HEADS):
            hi, mid, lo = _split3(s_ref[h])
            eye = eye_ref[...]
            st_ref[h] = _dot_nt(eye, hi) + _dot_nt(eye, mid) + _dot_nt(eye, lo)


def hgrn_prompt(hg, lw, cn, tt=256):
    b, l, _ = hg.shape
    const2 = lambda bi, i: (0, 0)
    return pl.pallas_call(
        partial(_hgrn_kernel, tt=tt),
        grid=(b, l // tt),
        in_specs=[pl.BlockSpec((None, tt, 1024), lambda bi, i: (bi, i, 0)),
                  pl.BlockSpec((1, 256), const2),
                  pl.BlockSpec(cn['hg_cst'].shape, const2),
                  pl.BlockSpec(cn['hg_msk'].shape, lambda bi, i: (0, 0, 0)),
                  pl.BlockSpec((HG_DK, HG_DK), const2),
                  pl.BlockSpec((256, 256), const2),
                  pl.BlockSpec((1, 256), const2)],
        out_specs=[pl.BlockSpec((None, tt, 256), lambda bi, i: (bi, i, 0)),
                   pl.BlockSpec((None, HG_HEADS, HG_DK, HG_DV), lambda bi, i: (bi, 0, 0, 0))],
        out_shape=[jax.ShapeDtypeStruct((b, l, 256), F32),
                   jax.ShapeDtypeStruct((b, HG_HEADS, HG_DK, HG_DV), F32)],
        scratch_shapes=[pltpu.VMEM((HG_HEADS, HG_DV, HG_DK), F32)],
        compiler_params=_cparams(("arbitrary", "arbitrary")),
        name="hgrn",
    )(hg, lw['lb'], cn['hg_cst'], cn['hg_msk'], cn['eye64'], cn['bd256'], lw['go'])


S5_BATCH = 4


def _gelu_tanh(x):
    return 0.5 * x * (1.0 + jnp.tanh(math.sqrt(2.0 / math.pi) * (x + 0.044715 * (x * x * x))))


def _s5_kernel(u_ref, bre_ref, bim_ref, ar_ref, ai_ref, cre_ref, cim_ref, d_ref, gw_ref, gb_ref,
               o_ref, hout_ref, xs_ref, h_ref, *, tc):
    i = pl.program_id(0)

    @pl.when(i == 0)
    def _():
        h_ref[...] = jnp.zeros(h_ref.shape, F32)

    nlt = S5_LANES // LANES
    for b in range(S5_BATCH):
        ub = u_ref[b].astype(BF16)
        xre, xim = _dot(ub, bre_ref[...]), _dot(ub, bim_ref[...])
        for j in range(nlt):
            xs_ref[j, pl.ds(b, tc, stride=SUBLANES), :] = xre[:, j * LANES:(j + 1) * LANES]
            xs_ref[j, pl.ds(S5_BATCH + b, tc, stride=SUBLANES), :] = xim[:, j * LANES:(j + 1) * LANES]
    ar = [ar_ref[:, j * LANES:(j + 1) * LANES] for j in range(nlt)]
    ai = [ai_ref[:, j * LANES:(j + 1) * LANES] for j in range(nlt)]

    def body(t, hcur):
        off = pl.multiple_of(t * SUBLANES, SUBLANES)
        hnew = []
        for j in range(nlt):
            hj = ar[j] * hcur[j] + ai[j] * pltpu.roll(hcur[j], S5_BATCH, 0) + xs_ref[j, pl.ds(off, SUBLANES), :]
            xs_ref[j, pl.ds(off, SUBLANES), :] = hj
            hnew.append(hj)
        return tuple(hnew)

    h0 = tuple(h_ref[:, j * LANES:(j + 1) * LANES] for j in range(nlt))
    hfin = lax.fori_loop(0, tc, body, h0, unroll=8)
    for j in range(nlt):
        h_ref[:, j * LANES:(j + 1) * LANES] = hfin[j]
    for b in range(S5_BATCH):
        hre = jnp.concatenate([xs_ref[j, pl.ds(b, tc, stride=SUBLANES), :] for j in range(nlt)], axis=1)
        him = jnp.concatenate([xs_ref[j, pl.ds(S5_BATCH + b, tc, stride=SUBLANES), :] for j in range(nlt)], axis=1)
        y = _dot(hre.astype(BF16), cre_ref[...]) - _dot(him.astype(BF16), cim_ref[...]) + d_ref[...] * u_ref[b]
        z = _gelu_tanh(y)
        o_ref[b] = z * _sigmoid(_dot(z.astype(BF16), gw_ref[...]) + gb_ref[...])

    @pl.when(i == pl.num_programs(0) - 1)
    def _():
        hout_ref[...] = h_ref[...]


def s5_prompt(u, lw, tc=256):
    b, l, _ = u.shape
    assert b == S5_BATCH
    const = lambda i: (0, 0)
    return pl.pallas_call(
        partial(_s5_kernel, tc=tc),
        grid=(l // tc,),
        in_specs=[pl.BlockSpec((b, tc, 256), lambda i: (0, i, 0)),
                  pl.BlockSpec((256, S5_LANES), const), pl.BlockSpec((256, S5_LANES), const),
                  pl.BlockSpec((SUBLANES, S5_LANES), const), pl.BlockSpec((SUBLANES, S5_LANES), const),
                  pl.BlockSpec((S5_LANES, 256), const), pl.BlockSpec((S5_LANES, 256), const),
                  pl.BlockSpec((1, 256), const), pl.BlockSpec((256, 256), const), pl.BlockSpec((1, 256), const)],
        out_specs=[pl.BlockSpec((b, tc, 256), lambda i: (0, i, 0)),
                   pl.BlockSpec((SUBLANES, S5_LANES), const)],
        out_shape=[jax.ShapeDtypeStruct((b, l, 256), F32), jax.ShapeDtypeStruct((SUBLANES, S5_LANES), F32)],
        scratch_shapes=[pltpu.VMEM((S5_LANES // LANES, tc * SUBLANES, LANES), F32),
                        pltpu.VMEM((SUBLANES, S5_LANES), F32)],
        compiler_params=_cparams(("arbitrary",)),
        name="s5_scan",
    )(u, lw['s5_bre'], lw['s5_bim'], lw['s5_ar8'], lw['s5_ai8'], lw['s5_cre'], lw['s5_cim'],
      lw['s5_d'], lw['s5_gw'], lw['s5_gb'])


def _merge_kernel(x_ref, h_ref, b0_ref, b1_ref, b2_ref, b3_ref, wg_ref, wb_ref, wo_ref, o_ref):
    hb = h_ref[...]
    merged = None
    for n, br in enumerate((b0_ref, b1_ref, b2_ref, b3_ref)):
        g = _sigmoid(_dot(hb, wg_ref[:, n * D_MODEL:(n + 1) * D_MODEL]))
        term = g * _dot(br[...].astype(BF16), wb_ref[n])
        merged = term if merged is None else merged + term
    o_ref[...] = x_ref[...] + _dot(merged.astype(BF16), wo_ref[...])


def merge(x, hb, branches, lw, tm):
    t = x.shape[0]
    row = lambda i: (i, 0)
    const = lambda i: (0, 0)
    return pl.pallas_call(
        _merge_kernel,
        grid=(t // tm,),
        in_specs=[pl.BlockSpec((tm, D_MODEL), row), pl.BlockSpec((tm, D_MODEL), row)]
                 + [pl.BlockSpec((tm, MIX_W), row)] * N_BRANCH
                 + [pl.BlockSpec((D_MODEL, N_BRANCH * D_MODEL), const),
                    pl.BlockSpec((N_BRANCH, MIX_W, D_MODEL), lambda i: (0, 0, 0)),
                    pl.BlockSpec((D_MODEL, D_MODEL), const)],
        out_specs=pl.BlockSpec((tm, D_MODEL), row),
        out_shape=jax.ShapeDtypeStruct((t, D_MODEL), F32),
        compiler_params=_cparams(("arbitrary",)),
        name="merge",
    )(x, hb, *branches, lw['wg'], lw['wb'], lw['wo'])


def _ffn_core(x, g, wu, cw, cb, wd, a_m2, a_m1):
    h = x * lax.rsqrt(jnp.mean(x * x, axis=-1, keepdims=True) + EPS) * g
    up = _dot(h.astype(BF16), wu)
    a, bb = up[:, 0:D_FF], up[:, D_FF:2 * D_FF]
    ac = cb + a_m2(a) * cw[0:1, :] + a_m1(a) * cw[1:2, :] + a * cw[2:3, :]
    return a, x + _dot((_silu(ac) * bb).astype(BF16), wd)


def _ffn_seq_kernel(x_ref, g_ref, wu_ref, cw_ref, cb_ref, wd_ref, o_ref, cs_ref, prev_ref, *, tm):
    @pl.when(pl.program_id(1) == 0)
    def _():
        prev_ref[...] = jnp.zeros(prev_ref.shape, F32)

    prev = prev_ref[...]
    row = lax.broadcasted_iota(jnp.int32, (tm, D_FF), 0)
    p1, p2 = prev[SUBLANES - 1:SUBLANES, :], prev[SUBLANES - 2:SUBLANES - 1, :]
    a_m1 = lambda a: jnp.where(row == 0, p1, pltpu.roll(a, 1, 0))
    a_m2 = lambda a: jnp.where(row == 0, p2, jnp.where(row == 1, p1, pltpu.roll(a, 2, 0)))
    a, o = _ffn_core(x_ref[...], g_ref[...], wu_ref[...], cw_ref[...], cb_ref[...], wd_ref[...], a_m2, a_m1)
    o_ref[...] = o
    tail = a[tm - SUBLANES:tm, :]
    prev_ref[...] = tail
    cs_ref[...] = tail[SUBLANES - (CONV_W - 1):SUBLANES, :]


def _ffn_step_kernel(x_ref, b0_ref, b1_ref, g_ref, wu_ref, cw_ref, cb_ref, wd_ref, o_ref, a_ref):
    a, o = _ffn_core(x_ref[...], g_ref[...], wu_ref[...], cw_ref[...], cb_ref[...], wd_ref[...],
                     lambda a: b0_ref[...], lambda a: b1_ref[...])
    o_ref[...] = o
    a_ref[...] = a


def _ffn_weight_specs(const):
    return [pl.BlockSpec((1, D_MODEL), const), pl.BlockSpec((D_MODEL, 2 * D_FF), const),
            pl.BlockSpec((CONV_W, D_FF), const), pl.BlockSpec((1, D_FF), const), pl.BlockSpec((D_FF, D_MODEL), const)]


def ffn_prompt(x, lw, b, l, tm=256):
    nt = l // tm
    const = lambda bi, i: (0, 0)
    row = lambda bi, i: (bi * nt + i, 0)
    return pl.pallas_call(
        partial(_ffn_seq_kernel, tm=tm),
        grid=(b, nt),
        in_specs=[pl.BlockSpec((tm, D_MODEL), row)] + _ffn_weight_specs(const),
        out_specs=[pl.BlockSpec((tm, D_MODEL), row),
                   pl.BlockSpec((None, CONV_W - 1, D_FF), lambda bi, i: (bi, 0, 0))],
        out_shape=[jax.ShapeDtypeStruct((b * l, D_MODEL), F32), jax.ShapeDtypeStruct((b, CONV_W - 1, D_FF), F32)],
        scratch_shapes=[pltpu.VMEM((SUBLANES, D_FF), F32)],
        compiler_params=_cparams(("arbitrary", "arbitrary")),
        name="ffn_seq",
    )(x, lw['ffn_g'], lw['ffn_wu'], lw['ffn_cw'], lw['ffn_cb'], lw['ffn_wd'])


def ffn_step(x, buf0, buf1, lw):
    n = x.shape[0]
    const = lambda i: (0, 0)
    full = lambda w: pl.BlockSpec((n, w), const)
    return pl.pallas_call(
        _ffn_step_kernel,
        grid=(1,),
        in_specs=[full(D_MODEL), full(D_FF), full(D_FF)] + _ffn_weight_specs(const),
        out_specs=[full(D_MODEL), full(D_FF)],
        out_shape=[jax.ShapeDtypeStruct((n, D_MODEL), F32), jax.ShapeDtypeStruct((n, D_FF), F32)],
        compiler_params=_cparams(("arbitrary",)),
        name="ffn_step",
    )(x, buf0, buf1, lw['ffn_g'], lw['ffn_wu'], lw['ffn_cw'], lw['ffn_cb'], lw['ffn_wd'])


def _rope_tables(pos, reps):
    half = HEAD_DIM // 2
    inv = ROPE_THETA ** (-jnp.arange(half, dtype=F32) / half)
    ang = pos.astype(F32)[:, None] * inv[None, :]
    cos, sin = jnp.cos(ang), jnp.sin(ang)
    return (jnp.tile(jnp.concatenate([cos, cos], axis=1), (1, reps)),
            jnp.tile(jnp.concatenate([-sin, sin], axis=1), (1, reps)))


def make_consts(l_prompt, n_sample, past_len):
    cn = {}
    seg = np.arange(256) // HEAD_DIM
    cn['bd256'] = jnp.asarray(seg[:, None] == seg[None, :], BF16)
    cn['eye64'] = jnp.asarray(np.eye(HG_DK), BF16)
    cst, msk = _hgrn_constants()
    cn['hg_cst'] = jnp.asarray(cst, BF16)
    cn['hg_msk'] = jnp.asarray(msk, F32)
    tc = min(512, l_prompt)
    cn['tri'] = jnp.asarray(np.tril(np.ones((tc, tc))), BF16)
    n16 = l_prompt // CMP_STRIDE
    ci = np.arange(n16)[:, None]
    sj = np.arange(LANES)[None, :]
    cover = ((ci * CMP_STRIDE < sj * SEL_BLOCK + SEL_BLOCK) & (ci * CMP_STRIDE + CMP_LEN - 1 >= sj * SEL_BLOCK)
             & (ci < n16 - 1) & (sj < -(-l_prompt // SEL_BLOCK)))
    cn['cover'] = jnp.asarray(cover, BF16)
    cn['expand'] = jnp.asarray(np.arange(LANES)[:, None] == (np.arange(l_prompt)[None, :] // SEL_BLOCK), BF16)
    cn['cos_p'], cn['sin_p'] = _rope_tables(jnp.arange(l_prompt), 4)
    cn['cos_s'], cn['sin_s'] = _rope_tables(jnp.full((n_sample,), past_len), 4)
    return cn


def prep_layer(l, P, lb_all):
    lw = {}
    w = P['w_in'][l]
    pts = np.concatenate([[0], np.cumsum(IN_SIZES)])
    seg = lambda i: w[:, pts[i]:pts[i + 1]]
    z = lambda n: jnp.zeros((D_MODEL, n), w.dtype)
    lw['w_in'] = jnp.concatenate([seg(0), seg(1), seg(2), z(LANES - IN_SIZES[2])] + [seg(i) for i in range(3, 12)]
                                 + [z(LANES - IN_SIZES[11])], axis=1).astype(BF16)
    lw['wg'] = seg(12).astype(BF16)
    lw['attn_norm'] = P['attn_norm'][l].reshape(1, D_MODEL)
    lw['gq'] = jnp.tile(P['nsa_q_norm'][l], 4).reshape(1, 256)
    kn = P['nsa_k_norm'][l]
    lw['gk'] = jnp.concatenate([jnp.tile(kn[1], 4), jnp.tile(kn[2], 2)]).reshape(1, 384)
    lw['gc'] = kn[0].reshape(1, HEAD_DIM)
    lw['gfq'] = jnp.tile(P['fox_q_norm'][l], 4).reshape(1, 256)
    lw['gfk'] = jnp.tile(P['fox_k_norm'][l], 4).reshape(1, 256)
    lw['fbias'] = jnp.pad(P['fox_f_bias'][l], (0, LANES - FOX_HEADS)).reshape(1, LANES)
    wk = P['nsa_cmp_wk'][l].reshape(2, CMP_STRIDE, HEAD_DIM, HEAD_DIM)
    wv = P['nsa_cmp_wv'][l].reshape(2, CMP_STRIDE, HEAD_DIM, HEAD_DIM)
    we = jnp.zeros((CMP_STRIDE, 4, HEAD_DIM, 4, HEAD_DIM), F32)
    we = we.at[:, 0, :, 0, :].set(wk[0]).at[:, 1, :, 1, :].set(wv[0])
    we = we.at[:, 0, :, 2, :].set(wk[1]).at[:, 1, :, 3, :].set(wv[1])
    lw['we'] = we.reshape(CMP_STRIDE * 256, 256).astype(BF16)
    pe = P['nsa_cmp_pe'][l].reshape(1, CMP_LEN * HEAD_DIM)
    pe_hi = pe.astype(BF16)
    pe_lo = (pe - pe_hi.astype(F32)).astype(BF16)
    lw['pe8'] = jnp.concatenate([pe_hi, pe_lo, jnp.zeros((SUBLANES - 2, CMP_LEN * HEAD_DIM), BF16)], axis=0)
    lw['wkv'] = jnp.concatenate([P['nsa_cmp_wk'][l], P['nsa_cmp_wv'][l]], axis=1).astype(BF16)
    lw['lb'] = lb_all[l].reshape(1, 256)
    lw['go'] = jnp.tile(P['hgrn_o_norm'][l], 4).reshape(1, 256)
    dt = jnp.exp(P['s5_log_dt'][l].astype(F32))[:, None]
    ar, ai = P['s5_a_re'][l].astype(F32), P['s5_a_im'][l].astype(F32)
    mag = jnp.exp(dt * ar)
    abr, abi = mag * jnp.cos(dt * ai), mag * jnp.sin(dt * ai)
    den = ar * ar + ai * ai
    zr = ((abr - 1.0) * ar + abi * ai) / den
    zi = (abi * ar - (abr - 1.0) * ai) / den
    br_, bi_ = P['s5_b_re'][l].astype(F32), P['s5_b_im'][l].astype(F32)
    bbr = zr[..., None] * br_ - zi[..., None] * bi_
    bbi = zr[..., None] * bi_ + zi[..., None] * br_
    eye_g = jnp.eye(S5_GROUPS, dtype=F32)
    lw['s5_bre'] = jnp.einsum('gpc,gh->gchp', bbr, eye_g).reshape(MIX_W, S5_LANES).astype(BF16)
    lw['s5_bim'] = jnp.einsum('gpc,gh->gchp', bbi, eye_g).reshape(MIX_W, S5_LANES).astype(BF16)
    lw['s5_cre'] = jnp.einsum('gcp,gh->gphc', P['s5_c_re'][l].astype(F32), eye_g).reshape(S5_LANES, MIX_W).astype(BF16)
    lw['s5_cim'] = jnp.einsum('gcp,gh->gphc', P['s5_c_im'][l].astype(F32), eye_g).reshape(S5_LANES, MIX_W).astype(BF16)
    abr_f, abi_f = abr.reshape(1, S5_LANES), abi.reshape(1, S5_LANES)
    lw['s5_abr'], lw['s5_abi'] = abr_f, abi_f
    lw['s5_ar8'] = jnp.tile(abr_f, (SUBLANES, 1))
    lw['s5_ai8'] = jnp.concatenate([jnp.tile(-abi_f, (S5_BATCH, 1)), jnp.tile(abi_f, (S5_BATCH, 1))], axis=0)
    lw['s5_d'] = P['s5_d'][l].reshape(1, MIX_W)
    lw['s5_gw'] = P['s5_glu_w'][l].astype(BF16)
    lw['s5_gb'] = P['s5_glu_b'][l].reshape(1, MIX_W)
    lw['wb'] = P['w_branch'][l].astype(BF16)
    lw['wo'] = P['w_out'][l].astype(BF16)
    lw['ffn_g'] = P['ffn_norm'][l].reshape(1, D_MODEL)
    lw['ffn_wu'] = P['ffn_w_up'][l].astype(BF16)
    lw['ffn_cw'] = P['ffn_conv_w'][l]
    lw['ffn_cb'] = P['ffn_conv_b'][l].reshape(1, D_FF)
    lw['ffn_wd'] = P['ffn_w_down'][l].astype(BF16)
    return lw


def prompt_layer(x, lw, cn, b, l):
    r3 = lambda a: a.reshape(b, l, a.shape[-1])
    hb, qn, qr, rows, win, ng, hg, su, fq, foxr, flog = in_proj(x, lw, cn['bd256'], cn['cos_p'], cn['sin_p'], 256)
    pparts = matmul(rows.reshape(b * l // CMP_STRIDE, CMP_STRIDE * 256), lw['we']).reshape(b, l // CMP_STRIDE, 256)
    oc, mask = nsa_cmp_prompt(r3(qn), pparts, lw, cn['cover'])
    o_nsa = nsa_flash_prompt(r3(qr), r3(rows), r3(win), mask, cn['expand'], r3(ng), oc)
    c = cumsum_time(r3(flog), cn['tri'])
    c_t = jnp.swapaxes(c[:, :, 0:SUBLANES], 1, 2)
    o_fox = fox_flash_prompt(r3(fq), r3(foxr), c, c_t)
    o_h, hg_new = hgrn_prompt(r3(hg), lw, cn)
    o_s5, s5h = s5_prompt(r3(su), lw)
    flat = lambda a: a.reshape(b * l, a.shape[-1])
    x1 = merge(x, hb, (flat(o_nsa), flat(o_h), flat(o_s5), flat(o_fox)), lw, 256)
    x2, conv_new = ffn_prompt(x1, lw, b, l)
    nw = min(WINDOW, l)
    s5_new = jnp.moveaxis(s5h.reshape(2, S5_BATCH, S5_GROUPS, S5_STATE), 0, -1)
    return (x2, rows.reshape(b, l, 4, HEAD_DIM), win.reshape(b, l, 2, HEAD_DIM)[:, l - nw:],
            foxr.reshape(b, l, 2, FOX_HEADS, HEAD_DIM), flog[:, 0:FOX_HEADS].reshape(b, l, FOX_HEADS),
            hg_new, s5_new, conv_new)


def rmsnorm(x, g):
    xf = x.astype(F32)
    y = xf * lax.rsqrt(jnp.mean(xf * xf, axis=-1, keepdims=True) + EPS)
    return (y * g.astype(F32)).astype(x.dtype)


def rope(x, pos):
    half = HEAD_DIM // 2
    inv = ROPE_THETA ** (-jnp.arange(half, dtype=F32) / half)
    ang = pos.astype(F32)[:, None] * inv[None, :]
    shp = (pos.shape[0],) + (1,) * (x.ndim - 3) + (half,)
    cos, sin = jnp.cos(ang).reshape(shp), jnp.sin(ang).reshape(shp)
    xf = x.astype(F32)
    x1, x2 = xf[..., :half], xf[..., half:]
    return jnp.concatenate([x1 * cos - x2 * sin, x2 * cos + x1 * sin], axis=-1).astype(x.dtype)


def masked_softmax(logits, mask):
    lg = jnp.where(mask, logits.astype(F32), -jnp.inf)
    m = jnp.max(lg, axis=-1, keepdims=True)
    m = jnp.where(jnp.isfinite(m), m, 0.0)
    e = jnp.where(mask, jnp.exp(lg - m), 0.0)
    return e / jnp.maximum(jnp.sum(e, axis=-1, keepdims=True), 1e-30)


def gather_pages(pool, page_table):
    g = pool[page_table]
    return g.reshape((g.shape[0], g.shape[1] * g.shape[2]) + g.shape[3:])


def nsa_compress(rows, pe, w):
    B, T, _ = rows.shape
    n_c = (T - CMP_LEN) // CMP_STRIDE + 1
    idx = jnp.arange(n_c)[:, None] * CMP_STRIDE + jnp.arange(CMP_LEN)[None, :]
    blocks = rows[:, idx] + pe.astype(rows.dtype)
    return blocks.reshape(B, n_c, CMP_LEN * HEAD_DIM) @ w


def nsa_step_jax(q, rows, win, gates, past_len, win_pos0, k_norm_c, pe, wk, wv):
    B, Lq, H, _ = q.shape
    T = rows.shape[1]
    Tw = win.shape[1]
    q_rot = rope(q, past_len + jnp.arange(Lq))
    k_c = rmsnorm(nsa_compress(rows[:, :, 0], pe, wk), k_norm_c)
    v_c = nsa_compress(rows[:, :, 1], pe, wv).astype(F32)
    n_c = k_c.shape[1]
    c_start = jnp.arange(n_c) * CMP_STRIDE
    c_end = c_start + CMP_LEN - 1
    n_s = -(-T // SEL_BLOCK)
    s_start = jnp.arange(n_s) * SEL_BLOCK
    cover = ((c_start[:, None] < s_start[None, :] + SEL_BLOCK) & (c_end[:, None] >= s_start[None, :])).astype(F32)
    sel_rows = jnp.pad(rows[:, :, 2:4], ((0, 0), (0, n_s * SEL_BLOCK - T), (0, 0), (0, 0)))
    sel_rows = sel_rows.reshape(B, n_s, SEL_BLOCK, 2, HEAD_DIM)
    k_top = min(N_SEL, n_s)
    qb = Lq
    kw_all = jnp.pad(win, ((0, 0), (WINDOW, 0), (0, 0), (0, 0)))
    buf_len = past_len - win_pos0
    bidx = jnp.arange(B)[:, None, None]
    pos = past_len + jnp.arange(qb)
    pc = masked_softmax(jnp.einsum('bqhd,bcd->bhqc', q, k_c) * SCALE, (c_end[None, :] <= pos[:, None])[None, None])
    o_c = jnp.einsum('bhqc,bcd->bqhd', pc, v_c)
    imp = jnp.einsum('bhqc,cs->bqs', pc, cover)
    cur = pos // SEL_BLOCK
    j = jnp.arange(n_s)[None, :]
    forced = (j == 0) | (j == cur[:, None]) | (j == cur[:, None] - 1)
    score = jnp.where(j > cur[:, None], -jnp.inf, imp + SEL_FORCE * forced)
    _, sel = lax.top_k(score, k_top)
    chosen = jnp.any(sel[..., None] == jnp.arange(n_s), axis=2)
    tpos = jnp.arange(T)
    ms = chosen[:, :, tpos // SEL_BLOCK] & (tpos[None, None, :] <= pos[None, :, None])
    ls = jnp.einsum('bqhd,btd->bhqt', q_rot, rows[:, :, 2]) * SCALE
    ps = masked_softmax(ls, ms[:, None])
    o_s = jnp.einsum('bhqt,btd->bqhd', ps, rows[:, :, 3].astype(F32))
    start = buf_len
    kwin = lax.dynamic_slice_in_dim(kw_all, start, WINDOW + qb, axis=1)
    wpos = win_pos0 - WINDOW + start + jnp.arange(WINDOW + qb)
    mw = ((wpos[None, :] >= win_pos0) & (wpos[None, :] < win_pos0 + Tw)
          & (wpos[None, :] <= pos[:, None]) & (pos[:, None] - wpos[None, :] < WINDOW))
    pw = masked_softmax(jnp.einsum('bqhd,bkd->bhqk', q_rot, kwin[:, :, 0]) * SCALE, mw[None, None])
    o_w = jnp.einsum('bhqk,bkd->bqhd', pw, kwin[:, :, 1].astype(F32))
    gs = jax.nn.sigmoid(gates.astype(F32))[..., None]
    o = gs[:, :, :, 0] * o_c + gs[:, :, :, 1] * o_s + gs[:, :, :, 2] * o_w
    return o.reshape(B, Lq, H * HEAD_DIM)


def fox_step_jax(q, k, v, logf, past_len):
    B, Lq, H, _ = q.shape
    T = k.shape[1]
    c = jnp.cumsum(logf.astype(F32), axis=1)
    cT = jnp.moveaxis(c, 2, 1)
    pos = past_len + jnp.arange(Lq)
    bias = jnp.moveaxis(c[:, past_len:], 2, 1)[..., None] - cT[:, :, None, :]
    lg = jnp.einsum('bqhd,bkhd->bhqk', q, k).astype(F32) * SCALE + bias
    p = masked_softmax(lg, (jnp.arange(T)[None, :] <= pos[:, None])[None, None])
    return jnp.einsum('bhqk,bkhd->bqhd', p, v.astype(F32))


def sample_layer(x, lw, cn, P, l, caches, page_table, lb):
    n = x.shape[0]
    cache_nsa_kv, cache_nsa_win, cache_fox_kv, cache_fox_logf, state_hgrn, state_s5, state_conv = caches
    hb, qn, qr, rows, win, ng, hg, su, fq, foxr, flog = in_proj(x, lw, cn['bd256'], cn['cos_s'], cn['sin_s'], n)
    nsa_past = gather_pages(cache_nsa_kv, page_table)
    past_len = nsa_past.shape[1]
    rows_full = jnp.concatenate([nsa_past, rows.reshape(n, 1, 4, HEAD_DIM)], axis=1)
    win_full = jnp.concatenate([cache_nsa_win, win.reshape(n, 1, 2, HEAD_DIM)], axis=1)
    o_nsa = nsa_step_jax(qn.reshape(n, 1, NSA_HEADS, HEAD_DIM), rows_full, win_full,
                         ng[:, 0:3 * NSA_HEADS].reshape(n, 1, NSA_HEADS, 3), past_len,
                         past_len - cache_nsa_win.shape[1], P['nsa_k_norm'][l][0], P['nsa_cmp_pe'][l],
                         P['nsa_cmp_wk'][l], P['nsa_cmp_wv'][l]).reshape(n, MIX_W)
    new_win = win_full[:, win_full.shape[1] - min(WINDOW, win_full.shape[1]):]
    hq, hf, hv, hgate = (hg[:, i * 256:(i + 1) * 256].reshape(n, HG_HEADS, HG_DK) for i in range(4))
    lbh = lb.reshape(HG_HEADS, HG_DK)
    f = lbh + (1.0 - lbh) * jax.nn.sigmoid(hf)
    hg_new = f[..., None] * state_hgrn + (1.0 - f)[..., None] * hv[:, :, None, :]
    o_h = jnp.einsum('bhk,bhkv->bhv', hq, hg_new)
    o_h = (rmsnorm(o_h, P['hgrn_o_norm'][l]) * jax.nn.silu(hgate)).reshape(n, MIX_W)
    abr, abi = lw['s5_abr'], lw['s5_abi']
    h0r, h0i = state_s5[..., 0].reshape(n, S5_LANES), state_s5[..., 1].reshape(n, S5_LANES)
    hr = abr * h0r - abi * h0i + su @ lw['s5_bre'].astype(F32)
    hi = abr * h0i + abi * h0r + su @ lw['s5_bim'].astype(F32)
    y5 = hr @ lw['s5_cre'].astype(F32) - hi @ lw['s5_cim'].astype(F32) + lw['s5_d'] * su
    z5 = jax.nn.gelu(y5)
    o_s5 = z5 * jax.nn.sigmoid(z5 @ lw['s5_gw'].astype(F32) + lw['s5_gb'])
    s5_new = jnp.stack([hr.reshape(n, S5_GROUPS, S5_STATE), hi.reshape(n, S5_GROUPS, S5_STATE)], axis=-1)
    fox_rows = foxr.reshape(n, 1, 2, FOX_HEADS, HEAD_DIM)
    kv_full = jnp.concatenate([gather_pages(cache_fox_kv, page_table), fox_rows], axis=1)
    flog4 = flog[:, 0:FOX_HEADS].reshape(n, 1, FOX_HEADS)
    lf_full = jnp.concatenate([gather_pages(cache_fox_logf, page_table), flog4], axis=1)
    o_fox = fox_step_jax(fq.reshape(n, 1, FOX_HEADS, HEAD_DIM), kv_full[:, :, 0], kv_full[:, :, 1], lf_full,
                         past_len).reshape(n, MIX_W)
    x1 = merge(x, hb, (o_nsa, o_h, o_s5, o_fox), lw, n)
    x2, a_new = ffn_step(x1, state_conv[:, 0], state_conv[:, 1], lw)
    conv_new = jnp.stack([state_conv[:, 1], a_new], axis=1)
    return (x2, rows.reshape(n, 1, 4, HEAD_DIM), new_win, fox_rows, flog4, hg_new, s5_new, conv_new)


PARAM_NAMES = ('attn_norm', 'w_in', 'nsa_q_norm', 'nsa_k_norm', 'nsa_cmp_pe', 'nsa_cmp_wk', 'nsa_cmp_wv',
               'hgrn_o_norm', 's5_a_re', 's5_a_im', 's5_log_dt', 's5_b_re', 's5_b_im', 's5_c_re', 's5_c_im',
               's5_d', 's5_glu_w', 's5_glu_b', 'fox_q_norm', 'fox_k_norm', 'fox_f_bias', 'w_branch', 'w_out',
               'ffn_norm', 'ffn_w_up', 'ffn_conv_w', 'ffn_conv_b', 'ffn_w_down')


def trunk(x_prompt, x_sample, caches, page_table, hgrn_lb_logits, P):
    depth = P['w_in'].shape[0]
    bp, lp, _ = x_prompt.shape
    ns = x_sample.shape[0]
    past_len = page_table.shape[1] * PAGE_SIZE
    sm = jax.nn.softmax(hgrn_lb_logits.astype(F32), axis=0)
    lb_all = jnp.cumsum(sm, axis=0) - sm[0]
    cn = make_consts(lp, ns, past_len)
    yp = x_prompt.reshape(bp * lp, D_MODEL)
    ys = x_sample.reshape(ns, D_MODEL)
    outs_p, outs_s = [], []
    for l in range(depth):
        lw = prep_layer(l, P, lb_all)
        res_p = prompt_layer(yp, lw, cn, bp, lp)
        res_s = sample_layer(ys, lw, cn, P, l, tuple(c[l] for c in caches), page_table, lb_all[l])
        yp, ys = res_p[0], res_s[0]
        outs_p.append(res_p[1:])
        outs_s.append(res_s[1:])
    sp = [jnp.stack([o[i] for o in outs_p], axis=0) for i in range(7)]
    ss = [jnp.stack([o[i] for o in outs_s], axis=0) for i in range(7)]
    nsa_kv_p, nsa_win_p, fox_kv_p, fox_logf_p, hgrn_p, s5_p, conv_p = sp
    nsa_kv_s, nsa_win_s, fox_kv_s, fox_logf_s, hgrn_s, s5_s, conv_s = ss
    return (yp.reshape(bp, lp, D_MODEL), ys.reshape(ns, 1, D_MODEL), nsa_kv_p, nsa_kv_s, nsa_win_p, nsa_win_s,
            fox_kv_p, fox_kv_s, fox_logf_p, fox_logf_s, hgrn_p, hgrn_s, s5_p, s5_s, conv_p, conv_s)


def kernel(x_prompt, x_sample, cache_nsa_kv, cache_nsa_win, cache_fox_kv, cache_fox_logf, state_hgrn, state_s5, state_ffn_conv, page_table, attn_norm, w_in, nsa_q_norm, nsa_k_norm, nsa_cmp_pe, nsa_cmp_wk, nsa_cmp_wv, hgrn_lb_logits, hgrn_o_norm, s5_a_re, s5_a_im, s5_log_dt, s5_b_re, s5_b_im, s5_c_re, s5_c_im, s5_d, s5_glu_w, s5_glu_b, fox_q_norm, fox_k_norm, fox_f_bias, w_branch, w_out, ffn_norm, ffn_w_up, ffn_conv_w, ffn_conv_b, ffn_w_down):
    P = dict(attn_norm=attn_norm, w_in=w_in, nsa_q_norm=nsa_q_norm, nsa_k_norm=nsa_k_norm,
             nsa_cmp_pe=nsa_cmp_pe, nsa_cmp_wk=nsa_cmp_wk, nsa_cmp_wv=nsa_cmp_wv, hgrn_o_norm=hgrn_o_norm,
             s5_a_re=s5_a_re, s5_a_im=s5_a_im, s5_log_dt=s5_log_dt, s5_b_re=s5_b_re, s5_b_im=s5_b_im,
             s5_c_re=s5_c_re, s5_c_im=s5_c_im, s5_d=s5_d, s5_glu_w=s5_glu_w, s5_glu_b=s5_glu_b,
             fox_q_norm=fox_q_norm, fox_k_norm=fox_k_norm, fox_f_bias=fox_f_bias, w_branch=w_branch,
             w_out=w_out, ffn_norm=ffn_norm, ffn_w_up=ffn_w_up, ffn_conv_w=ffn_conv_w,
             ffn_conv_b=ffn_conv_b, ffn_w_down=ffn_w_down)
    caches = (cache_nsa_kv, cache_nsa_win, cache_fox_kv, cache_fox_logf, state_hgrn, state_s5, state_ffn_conv)
    return trunk(x_prompt, x_sample, caches, page_table, hgrn_lb_logits, P)
```

```python
import math
from functools import partial

import numpy as np
import jax
import jax.numpy as jnp
from jax import lax
from jax.experimental import pallas as pl
from jax.experimental.pallas import tpu as pltpu

D_MODEL = 1024
DEPTH = 4
PAGE_SIZE = 128
HEAD_DIM = 64
N_BRANCH = 4
MIX_W = D_MODEL // N_BRANCH
NSA_HEADS = MIX_W // HEAD_DIM
CMP_LEN = 32
CMP_STRIDE = 16
SEL_BLOCK = 64
N_SEL = 16
WINDOW = 512
SEL_FORCE = 1000.0
HG_HEADS = 4
HG_DK = MIX_W // HG_HEADS
HG_DV = MIX_W // HG_HEADS
HG_CHUNK = 64
S5_GROUP = 16
S5_GROUPS = MIX_W // S5_GROUP
S5_STATE = 64
S5_LANES = S5_GROUPS * S5_STATE
FOX_HEADS = MIX_W // HEAD_DIM
Q_BLOCK = 128
D_FF = ((8 * D_MODEL // 3 + 127) // 128) * 128
CONV_W = 3
ROPE_THETA = 10000.0
EPS = 1e-6
SCALE = HEAD_DIM ** -0.5
NEG = -1e30

IN_SIZES = (NSA_HEADS * HEAD_DIM, 6 * HEAD_DIM, 3 * NSA_HEADS,
            HG_HEADS * HG_DK, HG_HEADS * HG_DK, HG_HEADS * HG_DV, HG_HEADS * HG_DV,
            MIX_W,
            FOX_HEADS * HEAD_DIM, FOX_HEADS * HEAD_DIM, FOX_HEADS * HEAD_DIM, FOX_HEADS,
            N_BRANCH * D_MODEL)

LANES = 128
SUBLANES = 8
VMEM_LIMIT_BYTES = 56 * 1024 * 1024

F32 = jnp.float32
BF16 = jnp.bfloat16

C_NQ, C_NKV, C_NG, C_HG, C_SU, C_FQ, C_FK, C_FV, C_FF, C_END = 0, 256, 640, 768, 1792, 2048, 2304, 2560, 2816, 2944


def _cparams(sem):
    return pltpu.CompilerParams(dimension_semantics=sem, vmem_limit_bytes=VMEM_LIMIT_BYTES)


def _split2(x):
    hi = x.astype(BF16)
    lo = (x - hi.astype(F32)).astype(BF16)
    return hi, lo


def _split3(x):
    hi = x.astype(BF16)
    r = x - hi.astype(F32)
    mid = r.astype(BF16)
    lo = (r - mid.astype(F32)).astype(BF16)
    return hi, mid, lo


def _dot(a, b):
    return jnp.dot(a, b, preferred_element_type=F32)


def _dot_nt(a, b):
    return lax.dot_general(a, b, (((1,), (1,)), ((), ())), preferred_element_type=F32)


def _dot_tn(a, b):
    return lax.dot_general(a, b, (((0,), (0,)), ((), ())), preferred_element_type=F32)


def _dot_exact_lhs(a01, x):
    hi, mid, lo = _split3(x)
    return _dot(a01, hi) + _dot(a01, mid) + _dot(a01, lo)


def _dot_exact_rhs(x, b01):
    hi, mid, lo = _split3(x)
    return _dot(hi, b01) + _dot(mid, b01) + _dot(lo, b01)


def _seg_rms(x, bd, gain):
    ms = _dot_exact_rhs(x * x, bd) * (1.0 / HEAD_DIM)
    return x * lax.rsqrt(ms + EPS) * gain


def _rope(x, cos, sin_signed):
    w = x.shape[-1]
    half = HEAD_DIM // 2
    lane = lax.broadcasted_iota(jnp.int32, x.shape, 1)
    first = (lane % HEAD_DIM) < half
    swapped = jnp.where(first, pltpu.roll(x, w - half, 1), pltpu.roll(x, half, 1))
    return x * cos + swapped * sin_signed


def _log_sigmoid(x):
    return jnp.minimum(x, 0.0) - jnp.log(1.0 + jnp.exp(-jnp.abs(x)))


def _sigmoid(x):
    return 1.0 / (1.0 + jnp.exp(-x))


def _silu(x):
    return x * _sigmoid(x)


def _mm_kernel(x_ref, w_ref, o_ref):
    o_ref[...] = _dot(x_ref[...].astype(BF16), w_ref[...])


def matmul(x, w, tm=256):
    m, k = x.shape
    n = w.shape[1]
    tm = min(tm, m)
    return pl.pallas_call(
        _mm_kernel,
        grid=(m // tm,),
        in_specs=[pl.BlockSpec((tm, k), lambda i: (i, 0)),
                  pl.BlockSpec((k, n), lambda i: (0, 0))],
        out_specs=pl.BlockSpec((tm, n), lambda i: (i, 0)),
        out_shape=jax.ShapeDtypeStruct((m, n), F32),
        compiler_params=_cparams(("arbitrary",)),
        name="matmul",
    )(x, w.astype(BF16))


def _in_proj_kernel(x_ref, g_ref, w_ref, cos_ref, sin_ref, bd_ref, gq_ref, gk_ref, gfq_ref, gfk_ref, fb_ref,
                    h_ref, qn_ref, qr_ref, rows_ref, win_ref, ng_ref, hg_ref, su_ref, fq_ref, fox_ref, flog_ref,
                    rows_b_ref, win_b_ref, fox_b_ref, rows_t_ref, win_t_ref, fox_t_ref, flog_t_ref):
    x = x_ref[...]
    h = x * lax.rsqrt(jnp.mean(x * x, axis=-1, keepdims=True) + EPS) * g_ref[...]
    hb = h.astype(BF16)
    h_ref[...] = hb
    p = _dot(hb, w_ref[...])
    bd = bd_ref[...]
    cos, sin = cos_ref[...], sin_ref[...]
    qn = _seg_rms(p[:, C_NQ:C_NKV], bd, gq_ref[...])
    qn_ref[...] = qn
    qr_ref[...] = _rope(qn, cos, sin)
    raw = p[:, C_NKV:C_NKV + 256]
    lane = lax.broadcasted_iota(jnp.int32, raw.shape, 1)
    nr = _rope(_seg_rms(raw, bd, gk_ref[:, 0:256]), cos, sin)
    rows = jnp.where((lane >= 2 * HEAD_DIM) & (lane < 3 * HEAD_DIM), nr, raw)
    raww = p[:, C_NKV + 256:C_NG]
    nw = _rope(_seg_rms(raww, bd[0:128, 0:128], gk_ref[:, 256:384]), cos[:, 0:128], sin[:, 0:128])
    lane_w = lax.broadcasted_iota(jnp.int32, raww.shape, 1)
    win = jnp.where(lane_w < HEAD_DIM, nw, raww)
    ng_ref[...] = p[:, C_NG:C_HG]
    hg_ref[...] = p[:, C_HG:C_SU]
    su_ref[...] = p[:, C_SU:C_FQ]
    fq_ref[...] = _seg_rms(p[:, C_FQ:C_FK], bd, gfq_ref[...])
    fk = _seg_rms(p[:, C_FK:C_FV], bd, gfk_ref[...])
    fv = p[:, C_FV:C_FF]
    flog = _log_sigmoid(p[:, C_FF:C_END] + fb_ref[...])
    rows_ref[...] = rows
    win_ref[...] = win
    fox_ref[:, 0:256] = fk
    fox_ref[:, 256:512] = fv
    flog_ref[...] = flog
    rows_b_ref[...] = rows.astype(BF16)
    win_b_ref[...] = win.astype(BF16)
    fox_b_ref[:, 0:256] = fk.astype(BF16)
    fox_b_ref[:, 256:512] = fv.astype(BF16)
    rows_t_ref[...] = rows.T
    win_t_ref[...] = win.T
    fox_t_ref[0:256, :] = fk.T
    fox_t_ref[256:512, :] = fv.T
    flog_t_ref[...] = flog.T[0:SUBLANES, :]


def in_proj(x, lw, bd256, cos_t, sin_t, tm, nb):
    t = x.shape[0]
    seg = t // nb
    nt = seg // tm
    row = lambda i: (i, 0)
    const = lambda i: (0, 0)
    tab = lambda i: (i % nt, 0)
    tmap = lambda i: (i // nt, 0, i % nt)
    widths = (256, 256, 256, 128, 128, 1024, 256, 256, 512, 128)
    bwidths = (256, 128, 512)
    twidths = (256, 128, 512, SUBLANES)
    out_shape = ([jax.ShapeDtypeStruct((t, D_MODEL), BF16)] + [jax.ShapeDtypeStruct((t, w), F32) for w in widths]
                 + [jax.ShapeDtypeStruct((t, w), BF16) for w in bwidths]
                 + [jax.ShapeDtypeStruct((nb, w, seg), F32) for w in twidths])
    out_specs = ([pl.BlockSpec((tm, D_MODEL), row)] + [pl.BlockSpec((tm, w), row) for w in widths]
                 + [pl.BlockSpec((tm, w), row) for w in bwidths]
                 + [pl.BlockSpec((None, w, tm), tmap) for w in twidths])
    return pl.pallas_call(
        _in_proj_kernel,
        grid=(t // tm,),
        in_specs=[pl.BlockSpec((tm, D_MODEL), row), pl.BlockSpec((1, D_MODEL), const),
                  pl.BlockSpec((D_MODEL, C_END), const), pl.BlockSpec((tm, 256), tab), pl.BlockSpec((tm, 256), tab),
                  pl.BlockSpec((256, 256), const), pl.BlockSpec((1, 256), const), pl.BlockSpec((1, 384), const),
                  pl.BlockSpec((1, 256), const), pl.BlockSpec((1, 256), const), pl.BlockSpec((1, 128), const)],
        out_specs=out_specs,
        out_shape=out_shape,
        compiler_params=_cparams(("arbitrary",)),
        name="in_proj",
    )(x, lw['attn_norm'], lw['w_in'], cos_t, sin_t, bd256, lw['gq'], lw['gk'], lw['gfq'], lw['gfk'], lw['fbias'])


def _topk_mask(score, k):
    lane = lax.broadcasted_iota(jnp.int32, score.shape, 1)
    sel = jnp.zeros(score.shape, F32)
    s = score
    for _ in range(k):
        m = jnp.max(s, axis=1, keepdims=True)
        idx = jnp.min(jnp.where(s == m, lane, LANES), axis=1, keepdims=True)
        pick = lane == idx
        sel = jnp.where(pick, 1.0, sel)
        s = jnp.where(pick, NEG, s)
    return sel


def _pe_term(pe_ref, wkv_ref):
    pew = _dot(pe_ref[...], wkv_ref[...])
    return pew[0:1, :] + pew[1:2, :]


def _cmp_tokens(p, pew, gain_c, n_c):
    n16 = p.shape[0]
    nxt = pltpu.roll(p, n16 - 1, 0)
    kc = p[:, 0:64] + nxt[:, 128:192] + pew[0:1, 0:64]
    vc = p[:, 64:128] + nxt[:, 192:256] + pew[0:1, 64:128]
    kc = kc * lax.rsqrt(jnp.mean(kc * kc, axis=-1, keepdims=True) + EPS) * gain_c
    return kc, vc


def _cmp_attend(q, kc, vc, valid):
    qh, ql = _split2(q)
    kh, kl = _split2(kc)
    s = _dot_nt(qh, kh) + _dot_nt(ql, kh) + _dot_nt(qh, kl)
    s = jnp.where(valid, s, NEG)
    m = jnp.max(s, axis=1, keepdims=True)
    e = jnp.where(valid, jnp.exp(s - m), 0.0)
    pc = e / jnp.maximum(jnp.sum(e, axis=1, keepdims=True), 1e-30)
    return pc, _dot(pc.astype(BF16), vc.astype(BF16))


def _cmp_kernel(q_ref, p_ref, pe_ref, wkv_ref, gc_ref, cover_ref, oc_ref, mask_ref, kc_ref, vc_ref, *, tq, n_c):
    qi = pl.program_id(1)

    @pl.when(qi == 0)
    def _():
        kc, vc = _cmp_tokens(p_ref[...], _pe_term(pe_ref, wkv_ref), gc_ref[...], n_c)
        kc_ref[...] = kc
        vc_ref[...] = vc

    kc, vc = kc_ref[...], vc_ref[...]
    n16 = kc.shape[0]
    pos = qi * tq + lax.broadcasted_iota(jnp.int32, (tq, n16), 0)
    ci = lax.broadcasted_iota(jnp.int32, (tq, n16), 1)
    valid = (ci * CMP_STRIDE + CMP_LEN - 1 <= pos) & (ci < n_c)
    q = q_ref[...]
    pcs = jnp.zeros((tq, n16), F32)
    outs = []
    for h in range(NSA_HEADS):
        pc, o = _cmp_attend(q[:, h * HEAD_DIM:(h + 1) * HEAD_DIM] * SCALE, kc, vc, valid)
        pcs = pcs + pc
        outs.append(o)
    oc_ref[...] = jnp.concatenate(outs, axis=1)
    imp = _dot_exact_rhs(pcs, cover_ref[...])
    j = lax.broadcasted_iota(jnp.int32, (tq, LANES), 1)
    cur = (qi * tq + lax.broadcasted_iota(jnp.int32, (tq, LANES), 0)) // SEL_BLOCK
    forced = (j == 0) | (j == cur) | (j == cur - 1)
    score = jnp.where(j > cur, NEG, imp + jnp.where(forced, SEL_FORCE, 0.0))
    mask_ref[...] = _topk_mask(score, N_SEL)


def nsa_cmp_prompt(qn, pparts, lw, cover, tq=256):
    b, l, _ = qn.shape
    n16 = l // CMP_STRIDE
    n_c = n16 - 1
    return pl.pallas_call(
        partial(_cmp_kernel, tq=tq, n_c=n_c),
        grid=(b, l // tq),
        in_specs=[pl.BlockSpec((None, tq, 256), lambda bi, qi: (bi, qi, 0)),
                  pl.BlockSpec((None, n16, 256), lambda bi, qi: (bi, 0, 0)),
                  pl.BlockSpec((SUBLANES, CMP_LEN * HEAD_DIM), lambda bi, qi: (0, 0)),
                  pl.BlockSpec((CMP_LEN * HEAD_DIM, 128), lambda bi, qi: (0, 0)),
                  pl.BlockSpec((1, HEAD_DIM), lambda bi, qi: (0, 0)),
                  pl.BlockSpec((n16, LANES), lambda bi, qi: (0, 0))],
        out_specs=[pl.BlockSpec((None, tq, 256), lambda bi, qi: (bi, qi, 0)),
                   pl.BlockSpec((None, tq, LANES), lambda bi, qi: (bi, qi, 0))],
        out_shape=[jax.ShapeDtypeStruct((b, l, 256), F32), jax.ShapeDtypeStruct((b, l, LANES), F32)],
        scratch_shapes=[pltpu.VMEM((n16, HEAD_DIM), F32), pltpu.VMEM((n16, HEAD_DIM), F32)],
        compiler_params=_cparams(("arbitrary", "arbitrary")),
        name="nsa_cmp",
    )(qn, pparts, lw['pe8'], lw['wkv'], lw['gc'], cover)


def _online_update(s, v, m_ref, l_ref, acc_ref, h):
    m_old = m_ref[h]
    m_new = jnp.maximum(m_old, jnp.max(s, axis=1, keepdims=True))
    alpha = jnp.exp(m_old - m_new)
    p = jnp.exp(s - jnp.concatenate([m_new] * (s.shape[1] // LANES), axis=1))
    l_ref[h] = alpha * l_ref[h] + jnp.sum(p, axis=1, keepdims=True)
    acc_ref[h] = alpha[:, 0:HEAD_DIM] * acc_ref[h] + _dot(p.astype(BF16), v)
    m_ref[h] = m_new


def _flash_init(m_refs, l_refs, acc_refs):
    for r in m_refs:
        r[...] = jnp.full(r.shape, NEG, F32)
    for r in l_refs + acc_refs:
        r[...] = jnp.zeros(r.shape, F32)


def _nsa_flash_kernel(q_ref, rows_ref, win_ref, mask_ref, e_ref, ng_ref, oc_ref, o_ref,
                      qs_ref, ms_ref, ls_ref, as_ref, mw_ref, lw_ref, aw_ref, *, tq, tk):
    qi, kj = pl.program_id(1), pl.program_id(2)
    nk = pl.num_programs(2)

    @pl.when(kj == 0)
    def _():
        _flash_init([ms_ref, mw_ref], [ls_ref, lw_ref], [as_ref, aw_ref])
        q = q_ref[...]
        for h in range(NSA_HEADS):
            qs_ref[h] = (q[:, h * HEAD_DIM:(h + 1) * HEAD_DIM] * SCALE).astype(BF16)

    q_lo = qi * tq
    k_lo = kj * tk
    qpos = q_lo + lax.broadcasted_iota(jnp.int32, (tq, tk), 0)
    kpos = k_lo + lax.broadcasted_iota(jnp.int32, (tq, tk), 1)
    causal = kpos <= qpos

    @pl.when(k_lo <= q_lo + tq - 1)
    def _():
        rows = rows_ref[...]
        ks = rows[:, 2 * HEAD_DIM:3 * HEAD_DIM]
        vs = rows[:, 3 * HEAD_DIM:4 * HEAD_DIM]
        sel = _dot(mask_ref[...].astype(BF16), e_ref[...]) > 0.5
        bias = jnp.where(sel & causal, 0.0, NEG)
        for h in range(NSA_HEADS):
            _online_update(_dot_nt(qs_ref[h], ks) + bias, vs, ms_ref, ls_ref, as_ref, h)

    @pl.when((k_lo <= q_lo + tq - 1) & (k_lo + tk - 1 > q_lo - WINDOW))
    def _():
        win = win_ref[...]
        kw = win[:, 0:HEAD_DIM]
        vw = win[:, HEAD_DIM:2 * HEAD_DIM]
        bias = jnp.where(causal & (qpos - kpos < WINDOW), 0.0, NEG)
        for h in range(NSA_HEADS):
            _online_update(_dot_nt(qs_ref[h], kw) + bias, vw, mw_ref, lw_ref, aw_ref, h)

    @pl.when(kj == nk - 1)
    def _():
        gs = _sigmoid(ng_ref[...])
        oc = oc_ref[...]
        outs = []
        for h in range(NSA_HEADS):
            o_s = as_ref[h] / ls_ref[h][:, 0:HEAD_DIM]
            o_w = aw_ref[h] / lw_ref[h][:, 0:HEAD_DIM]
            outs.append(gs[:, 3 * h:3 * h + 1] * oc[:, h * HEAD_DIM:(h + 1) * HEAD_DIM]
                        + gs[:, 3 * h + 1:3 * h + 2] * o_s + gs[:, 3 * h + 2:3 * h + 3] * o_w)
        o_ref[...] = jnp.concatenate(outs, axis=1)


def nsa_flash_prompt(qr, rows, win, mask, expand, ng, oc, tq=256, tk=512):
    b, l, _ = qr.shape
    tk = min(tk, l)
    nq, nk = l // tq, l // tk

    def kv_idx(bi, qi, kj):
        return (bi, jnp.minimum(kj, (qi * tq + tq - 1) // tk), 0)

    def win_idx(bi, qi, kj):
        hi = (qi * tq + tq - 1) // tk
        lo = jnp.maximum((qi * tq - WINDOW + 1) // tk, 0)
        return (bi, jnp.clip(kj, lo, hi), 0)

    qmap = lambda bi, qi, kj: (bi, qi, 0)
    hshape = (NSA_HEADS, tq, LANES)
    return pl.pallas_call(
        partial(_nsa_flash_kernel, tq=tq, tk=tk),
        grid=(b, nq, nk),
        in_specs=[pl.BlockSpec((None, tq, 256), qmap),
                  pl.BlockSpec((None, tk, 256), kv_idx),
                  pl.BlockSpec((None, tk, 128), win_idx),
                  pl.BlockSpec((None, tq, LANES), qmap),
                  pl.BlockSpec((LANES, tk), lambda bi, qi, kj: (0, jnp.minimum(kj, (qi * tq + tq - 1) // tk))),
                  pl.BlockSpec((None, tq, LANES), qmap),
                  pl.BlockSpec((None, tq, 256), qmap)],
        out_specs=pl.BlockSpec((None, tq, 256), qmap),
        out_shape=jax.ShapeDtypeStruct((b, l, 256), F32),
        scratch_shapes=[pltpu.VMEM((NSA_HEADS, tq, HEAD_DIM), BF16),
                        pltpu.VMEM(hshape, F32), pltpu.VMEM(hshape, F32), pltpu.VMEM((NSA_HEADS, tq, HEAD_DIM), F32),
                        pltpu.VMEM(hshape, F32), pltpu.VMEM(hshape, F32), pltpu.VMEM((NSA_HEADS, tq, HEAD_DIM), F32)],
        compiler_params=_cparams(("arbitrary", "arbitrary", "arbitrary")),
        name="nsa_flash",
    )(qr, rows, win, mask, expand, ng, oc)


def _cumsum_kernel(x_ref, tri_ref, o_ref, carry_ref):
    @pl.when(pl.program_id(1) == 0)
    def _():
        carry_ref[...] = jnp.zeros(carry_ref.shape, F32)

    c = _dot_exact_lhs(tri_ref[...], x_ref[...]) + carry_ref[0:1, :]
    o_ref[...] = c
    carry_ref[0:1, :] = c[c.shape[0] - 1:c.shape[0], :]


def cumsum_time(x, tri):
    b, l, w = x.shape
    tc = tri.shape[0]
    return pl.pallas_call(
        _cumsum_kernel,
        grid=(b, l // tc),
        in_specs=[pl.BlockSpec((None, tc, w), lambda bi, i: (bi, i, 0)),
                  pl.BlockSpec((tc, tc), lambda bi, i: (0, 0))],
        out_specs=pl.BlockSpec((None, tc, w), lambda bi, i: (bi, i, 0)),
        out_shape=jax.ShapeDtypeStruct((b, l, w), F32),
        scratch_shapes=[pltpu.VMEM((SUBLANES, w), F32)],
        compiler_params=_cparams(("arbitrary", "arbitrary")),
        name="cumsum_time",
    )(x, tri)


def _fox_flash_kernel(q_ref, kv_ref, cq_ref, ck_ref, o_ref, qs_ref, m_ref, l_ref, acc_ref, *, tq, tk):
    qi, kj = pl.program_id(1), pl.program_id(2)
    nk = pl.num_programs(2)

    @pl.when(kj == 0)
    def _():
        _flash_init([m_ref], [l_ref], [acc_ref])
        q = q_ref[...]
        for h in range(FOX_HEADS):
            qs_ref[h] = (q[:, h * HEAD_DIM:(h + 1) * HEAD_DIM] * SCALE).astype(BF16)

    q_lo = qi * tq
    k_lo = kj * tk

    @pl.when(k_lo <= q_lo + tq - 1)
    def _():
        qpos = q_lo + lax.broadcasted_iota(jnp.int32, (tq, tk), 0)
        kpos = k_lo + lax.broadcasted_iota(jnp.int32, (tq, tk), 1)
        bias = jnp.where(kpos <= qpos, 0.0, NEG)
        kv = kv_ref[...]
        cq = cq_ref[...]
        ck = ck_ref[...]
        for h in range(FOX_HEADS):
            k = kv[:, h * HEAD_DIM:(h + 1) * HEAD_DIM]
            v = kv[:, 256 + h * HEAD_DIM:256 + (h + 1) * HEAD_DIM]
            s = _dot_nt(qs_ref[h], k) + ((cq[:, h:h + 1] - ck[h:h + 1, :]) + bias)
            _online_update(s, v, m_ref, l_ref, acc_ref, h)

    @pl.when(kj == nk - 1)
    def _():
        o_ref[...] = jnp.concatenate(
            [acc_ref[h] / l_ref[h][:, 0:HEAD_DIM] for h in range(FOX_HEADS)], axis=1)


def fox_flash_prompt(fq, fox_rows, c, c_t, tq=256, tk=512):
    b, l, _ = fq.shape
    tk = min(tk, l)
    nq, nk = l // tq, l // tk
    qmap = lambda bi, qi, kj: (bi, qi, 0)
    last = lambda qi: (qi * tq + tq - 1) // tk
    hshape = (FOX_HEADS, tq, LANES)
    return pl.pallas_call(
        partial(_fox_flash_kernel, tq=tq, tk=tk),
        grid=(b, nq, nk),
        in_specs=[pl.BlockSpec((None, tq, 256), qmap),
                  pl.BlockSpec((None, tk, 512), lambda bi, qi, kj: (bi, jnp.minimum(kj, last(qi)), 0)),
                  pl.BlockSpec((None, tq, LANES), qmap),
                  pl.BlockSpec((None, SUBLANES, tk), lambda bi, qi, kj: (bi, 0, jnp.minimum(kj, last(qi))))],
        out_specs=pl.BlockSpec((None, tq, 256), qmap),
        out_shape=jax.ShapeDtypeStruct((b, l, 256), F32),
        scratch_shapes=[pltpu.VMEM((FOX_HEADS, tq, HEAD_DIM), BF16),
                        pltpu.VMEM(hshape, F32), pltpu.VMEM(hshape, F32), pltpu.VMEM((FOX_HEADS, tq, HEAD_DIM), F32)],
        compiler_params=_cparams(("arbitrary", "arbitrary", "arbitrary")),
        name="fox_flash",
    )(fq, fox_rows, c, c_t)


HG_LEVELS = int(math.log2(HG_CHUNK))


def _hgrn_constants():
    c = HG_CHUNK
    t = np.arange(c)[:, None]
    u = np.arange(c)[None, :]
    mats = [(u <= t), (u > t)]
    masks = []
    for lv in range(HG_LEVELS):
        bit = (t >> lv) & 1
        p_t = (t >> lv) << lv
        mats.append((bit == 1) & (u >= p_t) & (u <= t))
    for lv in range(HG_LEVELS):
        bit = (t >> lv) & 1
        p_s = ((t >> lv) + 1) << lv
        mats.append((bit == 0) & (u > t) & (u <= p_s - 1))
    for lv in range(HG_LEVELS):
        masks.append(((t >> (lv + 1)) == (u >> (lv + 1))) & (((t >> lv) & 1) == 1) & (((u >> lv) & 1) == 0))
    masks.append(t == u)
    cst = np.concatenate([m.astype(np.float32) for m in mats], axis=0)
    msk = np.stack([m.astype(np.float32) for m in masks], axis=0)
    return cst, msk


def _hgrn_kernel(hg_ref, lb_ref, cst_ref, msk_ref, eye_ref, bd_ref, go_ref, o_ref, st_ref, s_ref, *, tt):
    i = pl.program_id(1)
    c = HG_CHUNK

    @pl.when(i == 0)
    def _():
        s_ref[...] = jnp.zeros(s_ref.shape, F32)

    lb = lb_ref[...]
    for ch in range(tt // c):
        r = ch * c
        hq = hg_ref[r:r + c, 0:256]
        hf = hg_ref[r:r + c, 256:512]
        hv = hg_ref[r:r + c, 512:768]
        hgate = hg_ref[r:r + c, 768:1024]
        f = lb + (1.0 - lb) * _sigmoid(hf)
        lf = jnp.log(f)
        kk = 1.0 - f
        e = _dot_exact_lhs(cst_ref[...], lf)
        eb = jnp.exp(e[0:c])
        esu = jnp.exp(e[c:2 * c])
        ebl = eb[c - 1:c, :]
        qf = [(hq * jnp.exp(e[(2 + lv) * c:(3 + lv) * c])).astype(BF16) for lv in range(HG_LEVELS)]
        kg = [(kk * jnp.exp(e[(2 + HG_LEVELS + lv) * c:(3 + HG_LEVELS + lv) * c])).astype(BF16)
              for lv in range(HG_LEVELS)]
        qb, kb, vb = hq.astype(BF16), kk.astype(BF16), hv.astype(BF16)
        qe, ke = (hq * eb).astype(BF16), (kk * esu).astype(BF16)
        outs = []
        for h in range(HG_HEADS):
            sl = slice(h * HG_DK, (h + 1) * HG_DK)
            att = msk_ref[HG_LEVELS] * _dot_nt(qb[:, sl], kb[:, sl])
            for lv in range(HG_LEVELS):
                att = att + msk_ref[lv] * _dot_nt(qf[lv][:, sl], kg[lv][:, sl])
            st = s_ref[h]
            outs.append(_dot(att.astype(BF16), vb[:, sl]) + _dot_nt(qe[:, sl], st.astype(BF16)))
            s_ref[h] = st * ebl[:, sl] + _dot_tn(vb[:, sl], ke[:, sl])
        o = jnp.concatenate(outs, axis=1)
        o_ref[r:r + c, :] = _seg_rms(o, bd_ref[...], go_ref[...]) * _silu(hgate)

    @pl.when(i == pl.num_programs(1) - 1)
    def _():
        for h in range(HG_HEADS):
            hi, mid, lo = _split3(s_ref[h])
            eye = eye_ref[...]
            st_ref[h] = _dot_nt(eye, hi) + _dot_nt(eye, mid) + _dot_nt(eye, lo)


def hgrn_prompt(hg, lw, cn, tt=256):
    b, l, _ = hg.shape
    const2 = lambda bi, i: (0, 0)
    return pl.pallas_call(
        partial(_hgrn_kernel, tt=tt),
        grid=(b, l // tt),
        in_specs=[pl.BlockSpec((None, tt, 1024), lambda bi, i: (bi, i, 0)),
                  pl.BlockSpec((1, 256), const2),
                  pl.BlockSpec(cn['hg_cst'].shape, const2),
                  pl.BlockSpec(cn['hg_msk'].shape, lambda bi, i: (0, 0, 0)),
                  pl.BlockSpec((HG_DK, HG_DK), const2),
                  pl.BlockSpec((256, 256), const2),
                  pl.BlockSpec((1, 256), const2)],
        out_specs=[pl.BlockSpec((None, tt, 256), lambda bi, i: (bi, i, 0)),
                   pl.BlockSpec((None, HG_HEADS, HG_DK, HG_DV), lambda bi, i: (bi, 0, 0, 0))],
        out_shape=[jax.ShapeDtypeStruct((b, l, 256), F32),
                   jax.ShapeDtypeStruct((b, HG_HEADS, HG_DK, HG_DV), F32)],
        scratch_shapes=[pltpu.VMEM((HG_HEADS, HG_DV, HG_DK), F32)],
        compiler_params=_cparams(("arbitrary", "arbitrary")),
        name="hgrn",
    )(hg, lw['lb'], cn['hg_cst'], cn['hg_msk'], cn['eye64'], cn['bd256'], lw['go'])


S5_BATCH = 4


def _gelu_tanh(x):
    return 0.5 * x * (1.0 + jnp.tanh(math.sqrt(2.0 / math.pi) * (x + 0.044715 * (x * x * x))))


def _s5_kernel(u_ref, bre_ref, bim_ref, ar_ref, ai_ref, cre_ref, cim_ref, d_ref, gw_ref, gb_ref,
               o_ref, hout_ref, xs_ref, h_ref, *, tc):
    i = pl.program_id(0)

    @pl.when(i == 0)
    def _():
        h_ref[...] = jnp.zeros(h_ref.shape, F32)

    nlt = S5_LANES // LANES
    for b in range(S5_BATCH):
        ub = u_ref[b].astype(BF16)
        xre, xim = _dot(ub, bre_ref[...]), _dot(ub, bim_ref[...])
        for j in range(nlt):
            xs_ref[j, pl.ds(b, tc, stride=SUBLANES), :] = xre[:, j * LANES:(j + 1) * LANES]
            xs_ref[j, pl.ds(S5_BATCH + b, tc, stride=SUBLANES), :] = xim[:, j * LANES:(j + 1) * LANES]
    ar = [ar_ref[:, j * LANES:(j + 1) * LANES] for j in range(nlt)]
    ai = [ai_ref[:, j * LANES:(j + 1) * LANES] for j in range(nlt)]

    def body(t, hcur):
        off = pl.multiple_of(t * SUBLANES, SUBLANES)
        hnew = []
        for j in range(nlt):
            hj = ar[j] * hcur[j] + ai[j] * pltpu.roll(hcur[j], S5_BATCH, 0) + xs_ref[j, pl.ds(off, SUBLANES), :]
            xs_ref[j, pl.ds(off, SUBLANES), :] = hj
            hnew.append(hj)
        return tuple(hnew)

    h0 = tuple(h_ref[:, j * LANES:(j + 1) * LANES] for j in range(nlt))
    hfin = lax.fori_loop(0, tc, body, h0, unroll=8)
    for j in range(nlt):
        h_ref[:, j * LANES:(j + 1) * LANES] = hfin[j]
    for b in range(S5_BATCH):
        hre = jnp.concatenate([xs_ref[j, pl.ds(b, tc, stride=SUBLANES), :] for j in range(nlt)], axis=1)
        him = jnp.concatenate([xs_ref[j, pl.ds(S5_BATCH + b, tc, stride=SUBLANES), :] for j in range(nlt)], axis=1)
        y = _dot(hre.astype(BF16), cre_ref[...]) - _dot(him.astype(BF16), cim_ref[...]) + d_ref[...] * u_ref[b]
        z = _gelu_tanh(y)
        o_ref[b] = z * _sigmoid(_dot(z.astype(BF16), gw_ref[...]) + gb_ref[...])

    @pl.when(i == pl.num_programs(0) - 1)
    def _():
        hout_ref[...] = h_ref[...]


def s5_prompt(u, lw, tc=256):
    b, l, _ = u.shape
    assert b == S5_BATCH
    const = lambda i: (0, 0)
    return pl.pallas_call(
        partial(_s5_kernel, tc=tc),
        grid=(l // tc,),
        in_specs=[pl.BlockSpec((b, tc, 256), lambda i: (0, i, 0)),
                  pl.BlockSpec((256, S5_LANES), const), pl.BlockSpec((256, S5_LANES), const),
                  pl.BlockSpec((SUBLANES, S5_LANES), const), pl.BlockSpec((SUBLANES, S5_LANES), const),
                  pl.BlockSpec((S5_LANES, 256), const), pl.BlockSpec((S5_LANES, 256), const),
                  pl.BlockSpec((1, 256), const), pl.BlockSpec((256, 256), const), pl.BlockSpec((1, 256), const)],
        out_specs=[pl.BlockSpec((b, tc, 256), lambda i: (0, i, 0)),
                   pl.BlockSpec((SUBLANES, S5_LANES), const)],
        out_shape=[jax.ShapeDtypeStruct((b, l, 256), F32), jax.ShapeDtypeStruct((SUBLANES, S5_LANES), F32)],
        scratch_shapes=[pltpu.VMEM((S5_LANES // LANES, tc * SUBLANES, LANES), F32),
                        pltpu.VMEM((SUBLANES, S5_LANES), F32)],
        compiler_params=_cparams(("arbitrary",)),
        name="s5_scan",
    )(u, lw['s5_bre'], lw['s5_bim'], lw['s5_ar8'], lw['s5_ai8'], lw['s5_cre'], lw['s5_cim'],
      lw['s5_d'], lw['s5_gw'], lw['s5_gb'])


def _merge_kernel(x_ref, h_ref, b0_ref, b1_ref, b2_ref, b3_ref, wg_ref, wb_ref, wo_ref, o_ref):
    hb = h_ref[...]
    merged = None
    for n, br in enumerate((b0_ref, b1_ref, b2_ref, b3_ref)):
        g = _sigmoid(_dot(hb, wg_ref[:, n * D_MODEL:(n + 1) * D_MODEL]))
        term = g * _dot(br[...].astype(BF16), wb_ref[n])
        merged = term if merged is None else merged + term
    o_ref[...] = x_ref[...] + _dot(merged.astype(BF16), wo_ref[...])


def merge(x, hb, branches, lw, tm):
    t = x.shape[0]
    row = lambda i: (i, 0)
    const = lambda i: (0, 0)
    return pl.pallas_call(
        _merge_kernel,
        grid=(t // tm,),
        in_specs=[pl.BlockSpec((tm, D_MODEL), row), pl.BlockSpec((tm, D_MODEL), row)]
                 + [pl.BlockSpec((tm, MIX_W), row)] * N_BRANCH
                 + [pl.BlockSpec((D_MODEL, N_BRANCH * D_MODEL), const),
                    pl.BlockSpec((N_BRANCH, MIX_W, D_MODEL), lambda i: (0, 0, 0)),
                    pl.BlockSpec((D_MODEL, D_MODEL), const)],
        out_specs=pl.BlockSpec((tm, D_MODEL), row),
        out_shape=jax.ShapeDtypeStruct((t, D_MODEL), F32),
        compiler_params=_cparams(("arbitrary",)),
        name="merge",
    )(x, hb, *branches, lw['wg'], lw['wb'], lw['wo'])


def _ffn_core(x, g, wu, cw, cb, wd, a_m2, a_m1):
    h = x * lax.rsqrt(jnp.mean(x * x, axis=-1, keepdims=True) + EPS) * g
    up = _dot(h.astype(BF16), wu)
    a, bb = up[:, 0:D_FF], up[:, D_FF:2 * D_FF]
    ac = cb + a_m2(a) * cw[0:1, :] + a_m1(a) * cw[1:2, :] + a * cw[2:3, :]
    return a, x + _dot((_silu(ac) * bb).astype(BF16), wd)


def _ffn_seq_kernel(x_ref, g_ref, wu_ref, cw_ref, cb_ref, wd_ref, o_ref, cs_ref, prev_ref, *, tm):
    @pl.when(pl.program_id(1) == 0)
    def _():
        prev_ref[...] = jnp.zeros(prev_ref.shape, F32)

    prev = prev_ref[...]
    row = lax.broadcasted_iota(jnp.int32, (tm, D_FF), 0)
    p1, p2 = prev[SUBLANES - 1:SUBLANES, :], prev[SUBLANES - 2:SUBLANES - 1, :]
    a_m1 = lambda a: jnp.where(row == 0, p1, pltpu.roll(a, 1, 0))
    a_m2 = lambda a: jnp.where(row == 0, p2, jnp.where(row == 1, p1, pltpu.roll(a, 2, 0)))
    a, o = _ffn_core(x_ref[...], g_ref[...], wu_ref[...], cw_ref[...], cb_ref[...], wd_ref[...], a_m2, a_m1)
    o_ref[...] = o
    tail = a[tm - SUBLANES:tm, :]
    prev_ref[...] = tail
    cs_ref[...] = tail[SUBLANES - (CONV_W - 1):SUBLANES, :]


def _ffn_step_kernel(x_ref, cs_ref, g_ref, wu_ref, cw_ref, cb_ref, wd_ref, o_ref, cso_ref):
    b0, b1 = cs_ref[:, 0, :], cs_ref[:, 1, :]
    a, o = _ffn_core(x_ref[...], g_ref[...], wu_ref[...], cw_ref[...], cb_ref[...], wd_ref[...],
                     lambda a: b0, lambda a: b1)
    o_ref[...] = o
    cso_ref[:, 0, :] = b1
    cso_ref[:, 1, :] = a


def _ffn_weight_specs(const):
    return [pl.BlockSpec((1, D_MODEL), const), pl.BlockSpec((D_MODEL, 2 * D_FF), const),
            pl.BlockSpec((CONV_W, D_FF), const), pl.BlockSpec((1, D_FF), const), pl.BlockSpec((D_FF, D_MODEL), const)]


def ffn_prompt(x, lw, b, l, tm=256):
    nt = l // tm
    const = lambda bi, i: (0, 0)
    row = lambda bi, i: (bi * nt + i, 0)
    return pl.pallas_call(
        partial(_ffn_seq_kernel, tm=tm),
        grid=(b, nt),
        in_specs=[pl.BlockSpec((tm, D_MODEL), row)] + _ffn_weight_specs(const),
        out_specs=[pl.BlockSpec((tm, D_MODEL), row),
                   pl.BlockSpec((None, CONV_W - 1, D_FF), lambda bi, i: (bi, 0, 0))],
        out_shape=[jax.ShapeDtypeStruct((b * l, D_MODEL), F32), jax.ShapeDtypeStruct((b, CONV_W - 1, D_FF), F32)],
        scratch_shapes=[pltpu.VMEM((SUBLANES, D_FF), F32)],
        compiler_params=_cparams(("arbitrary", "arbitrary")),
        name="ffn_seq",
    )(x, lw['ffn_g'], lw['ffn_wu'], lw['ffn_cw'], lw['ffn_cb'], lw['ffn_wd'])


def ffn_step(x, conv_state, l, lw):
    n = x.shape[0]
    const = lambda i: (0, 0)
    return pl.pallas_call(
        _ffn_step_kernel,
        grid=(1,),
        in_specs=[pl.BlockSpec((n, D_MODEL), const),
                  pl.BlockSpec((None, n, CONV_W - 1, D_FF), lambda i: (l, 0, 0, 0))] + _ffn_weight_specs(const),
        out_specs=[pl.BlockSpec((n, D_MODEL), const), pl.BlockSpec((n, CONV_W - 1, D_FF), lambda i: (0, 0, 0))],
        out_shape=[jax.ShapeDtypeStruct((n, D_MODEL), F32), jax.ShapeDtypeStruct((n, CONV_W - 1, D_FF), F32)],
        compiler_params=_cparams(("arbitrary",)),
        name="ffn_step",
    )(x, conv_state, lw['ffn_g'], lw['ffn_wu'], lw['ffn_cw'], lw['ffn_cb'], lw['ffn_wd'])


def _heads_to_rows(row):
    return jnp.concatenate([row[:, h * HEAD_DIM:(h + 1) * HEAD_DIM] for h in range(NSA_HEADS)]
                           + [jnp.zeros((SUBLANES - NSA_HEADS, HEAD_DIM), F32)], axis=0)


def _rows_to_heads(o8):
    return jnp.concatenate([o8[h:h + 1, :] for h in range(NSA_HEADS)], axis=1)


def _col8(cells):
    return jnp.concatenate(list(cells) + [jnp.zeros((SUBLANES - len(cells), 1), F32)], axis=0)


def _all_max(x):
    return jnp.max(jnp.max(x, axis=1, keepdims=True), axis=0, keepdims=True)


def _all_sum(x):
    return jnp.sum(jnp.sum(x, axis=1, keepdims=True), axis=0, keepdims=True)


def _paged_softmax(s, valid, s_new):
    m = jnp.maximum(_all_max(jnp.where(valid, s, NEG)), s_new)
    e = jnp.where(valid, jnp.exp(s - m), 0.0)
    e_new = jnp.exp(s_new - m)
    inv = 1.0 / (_all_sum(e) + e_new)
    return e * inv, e_new * inv


def _nsa_step_kernel(pt_ref, *refs, n_pages, wb, past_len):
    kv = refs[0:n_pages]
    (winc_ref, qn_ref, qr_ref, rows_ref, win_ref, ng_ref, wcmp_ref, pe_ref, wkv_ref, gc_ref, cover_ref,
     amask_ref, ex_ref, eye_ref, o_ref, nwin_ref, xc_ref, sl_ref, pl_ref) = refs[n_pages:]
    b = pl.program_id(0)
    row = lambda ref: ref[pl.ds(b, 1), :]
    rnew, wnew = row(rows_ref), row(win_ref)
    gs = _sigmoid(row(ng_ref))
    for j in range(n_pages):
        xc_ref[j * PAGE_SIZE:(j + 1) * PAGE_SIZE, :] = kv[j][0:2 * HEAD_DIM, :].T
    n16 = n_pages * PAGE_SIZE // CMP_STRIDE
    p = jnp.zeros((n16, 256), F32)
    for r in range(CMP_STRIDE):
        p = p + _dot(xc_ref[pl.ds(r, n16, stride=CMP_STRIDE), :].astype(BF16), wcmp_ref[r])
    n_c = n16 - 1
    kc, vc = _cmp_tokens(p, _pe_term(pe_ref, wkv_ref), gc_ref[...], n_c)
    ci = lax.broadcasted_iota(jnp.int32, (SUBLANES, n16), 1)
    ri = lax.broadcasted_iota(jnp.int32, (SUBLANES, n16), 0)
    pc, o_c = _cmp_attend(_heads_to_rows(row(qn_ref)) * SCALE, kc, vc, ci < n_c)
    pcs = jnp.sum(jnp.where(ri < NSA_HEADS, pc, 0.0), axis=0, keepdims=True)
    imp = _dot_exact_rhs(jnp.broadcast_to(pcs, (SUBLANES, n16)), cover_ref[...])
    j_l = lax.broadcasted_iota(jnp.int32, (SUBLANES, LANES), 1)
    cur = past_len // SEL_BLOCK
    forced = (j_l == 0) | (j_l == cur) | (j_l == cur - 1)
    sel = _topk_mask(jnp.where(j_l > cur, NEG, imp + jnp.where(forced, SEL_FORCE, 0.0)), N_SEL)
    selx = _dot((amask_ref[...] * sel[0:1, :]).astype(BF16), ex_ref[...]) > 0.5
    qr8 = _heads_to_rows(row(qr_ref)) * SCALE
    qrb = qr8.astype(BF16)
    for j in range(n_pages):
        sl_ref[:, j, :] = _dot(qrb, kv[j][2 * HEAD_DIM:3 * HEAD_DIM, :].astype(BF16))
    s_new = jnp.sum(qr8 * rnew[:, 2 * HEAD_DIM:3 * HEAD_DIM], axis=1, keepdims=True)
    p_new = []
    for h in range(NSA_HEADS):
        ph, pn = _paged_softmax(sl_ref[h], selx, s_new[h:h + 1, :])
        pl_ref[h] = ph
        p_new.append(pn)
    pl_ref[NSA_HEADS:SUBLANES] = jnp.zeros((SUBLANES - NSA_HEADS, n_pages, PAGE_SIZE), F32)
    o_s = _col8(p_new) * rnew[:, 3 * HEAD_DIM:4 * HEAD_DIM]
    for j in range(n_pages):
        o_s = o_s + _dot_nt(pl_ref[:, j, :].astype(BF16), kv[j][3 * HEAD_DIM:4 * HEAD_DIM, :].astype(BF16))
    wc = winc_ref[...]
    s_w = _dot(qrb, wc[0:HEAD_DIM, :].astype(BF16))
    valid_w = lax.broadcasted_iota(jnp.int32, s_w.shape, 1) > wb - WINDOW
    s_wn = jnp.sum(qr8 * wnew[:, 0:HEAD_DIM], axis=1, keepdims=True)
    m_w = jnp.maximum(jnp.max(jnp.where(valid_w, s_w, NEG), axis=1, keepdims=True), s_wn)
    e_w = jnp.where(valid_w, jnp.exp(s_w - m_w), 0.0)
    e_wn = jnp.exp(s_wn - m_w)
    o_w = (_dot_nt(e_w.astype(BF16), wc[HEAD_DIM:2 * HEAD_DIM, :].astype(BF16)) + e_wn * wnew[:, HEAD_DIM:2 * HEAD_DIM]
           ) / (jnp.sum(e_w, axis=1, keepdims=True) + e_wn)
    gate = lambda k: _col8([gs[:, 3 * h + k:3 * h + k + 1] for h in range(NSA_HEADS)])
    o_ref[pl.ds(b, 1), :] = _rows_to_heads(gate(0) * o_c + gate(1) * o_s + gate(2) * o_w)
    wn8 = jnp.broadcast_to(wnew, (SUBLANES, 2 * HEAD_DIM))
    hi, mid, lo = _split3(wn8)
    eye = eye_ref[...]
    col = (_dot_nt(eye, hi) + _dot_nt(eye, mid) + _dot_nt(eye, lo))[:, 0:1]
    lane = lax.broadcasted_iota(jnp.int32, wc.shape, 1)
    nwin_ref[...] = jnp.where(lane == wb - 1, col, pltpu.roll(wc, wb - 1, 1))


def nsa_step(pt, kv_t, win_t, l, qn, qr, rows, win, ng, lw, cn):
    n = qn.shape[0]
    wb = win_t.shape[-1]
    n_pages = pt.shape[0] // n
    past_len = n_pages * PAGE_SIZE
    assert wb == WINDOW
    res = lambda w: pl.BlockSpec((n, w), lambda b, pt: (0, 0))
    c2 = lambda shape: pl.BlockSpec(shape, lambda b, pt: (0,) * len(shape))
    page = lambda j: pl.BlockSpec((None, None, 256, PAGE_SIZE), lambda b, pt: (l, pt[j * n + b], 0, 0))
    return pl.pallas_call(
        partial(_nsa_step_kernel, n_pages=n_pages, wb=wb, past_len=past_len),
        grid_spec=pltpu.PrefetchScalarGridSpec(
            num_scalar_prefetch=1,
            grid=(n,),
            in_specs=[page(j) for j in range(n_pages)]
                     + [pl.BlockSpec((None, None, 2 * HEAD_DIM, wb), lambda b, pt: (l, b, 0, 0)),
                        res(256), res(256), res(256), res(128), res(128),
                        c2((CMP_STRIDE, 128, 256)), c2((SUBLANES, CMP_LEN * HEAD_DIM)),
                        c2((CMP_LEN * HEAD_DIM, 128)), c2((1, HEAD_DIM)), c2((past_len // CMP_STRIDE, LANES)),
                        c2((n_pages, LANES)), c2((LANES, PAGE_SIZE)), c2((LANES, LANES))],
            out_specs=[pl.BlockSpec((n, 256), lambda b, pt: (0, 0)),
                       pl.BlockSpec((None, 2 * HEAD_DIM, wb), lambda b, pt: (b, 0, 0))],
            scratch_shapes=[pltpu.VMEM((past_len, 2 * HEAD_DIM), F32),
                            pltpu.VMEM((SUBLANES, n_pages, PAGE_SIZE), F32),
                            pltpu.VMEM((SUBLANES, n_pages, PAGE_SIZE), F32)]),
        out_shape=[jax.ShapeDtypeStruct((n, 256), F32), jax.ShapeDtypeStruct((n, 2 * HEAD_DIM, wb), F32)],
        compiler_params=_cparams(("arbitrary",)),
        name="nsa_step",
    )(pt, *([kv_t] * n_pages), win_t, qn, qr, rows, win, ng, lw['wcmp'], lw['pe8'], lw['wkv'], lw['gc'],
      cn['cover_s'], cn['amask'], cn['ex'], cn['eye128'])


def _fox_step_kernel(pt_ref, *refs, n_pages):
    kv = refs[0:n_pages]
    lf = refs[n_pages:2 * n_pages]
    fq_ref, fox_ref, flog_ref, usuf_ref, ones_ref, lsu_ref, o_ref, lfs_ref, sl_ref, pl_ref = refs[2 * n_pages:]
    b = pl.program_id(0)
    fq = fq_ref[pl.ds(b, 1), :]
    kvnew = fox_ref[pl.ds(b, 1), :]
    fl = flog_ref[pl.ds(b, 1), :]
    ri = lax.broadcasted_iota(jnp.int32, (SUBLANES, 256), 0)
    li = lax.broadcasted_iota(jnp.int32, (SUBLANES, 256), 1)
    bdm = (li // HEAD_DIM) == ri
    qbd = jnp.where(bdm, jnp.broadcast_to(fq * SCALE, (SUBLANES, 256)), 0.0)
    qb = qbd.astype(BF16)
    for j in range(n_pages):
        lfs_ref[:, j, :] = lf[j][...]
        sl_ref[:, j, :] = _dot(qb, kv[j][0:256, :].astype(BF16))
    s_new = jnp.sum(qbd * kvnew[:, 0:256], axis=1, keepdims=True)
    always = jnp.full((n_pages, PAGE_SIZE), True)
    p_new = []
    for h in range(FOX_HEADS):
        lf_h = lfs_ref[h]
        suf = _dot_exact_rhs(lf_h, usuf_ref[...])
        tail = _dot_exact_lhs(lsu_ref[...], _dot_exact_rhs(lf_h, ones_ref[...]))
        ph, pn = _paged_softmax(sl_ref[h] + suf + tail + fl[:, h:h + 1], always, s_new[h:h + 1, :])
        pl_ref[h] = ph
        p_new.append(pn)
    pl_ref[FOX_HEADS:SUBLANES] = jnp.zeros((SUBLANES - FOX_HEADS, n_pages, PAGE_SIZE), F32)
    acc = _col8(p_new) * kvnew[:, 256:512]
    for j in range(n_pages):
        acc = acc + _dot_nt(pl_ref[:, j, :].astype(BF16), kv[j][256:512, :].astype(BF16))
    o_ref[pl.ds(b, 1), :] = jnp.sum(jnp.where(bdm, acc, 0.0), axis=0, keepdims=True)


def fox_step(pt, kv_t, lf_t, l, fq, foxr, flog, cn):
    n = fq.shape[0]
    n_pages = pt.shape[0] // n
    res = lambda w: pl.BlockSpec((n, w), lambda b, pt: (0, 0))
    c2 = lambda shape: pl.BlockSpec(shape, lambda b, pt: (0,) * len(shape))
    page = lambda j: pl.BlockSpec((None, None, 512, PAGE_SIZE), lambda b, pt: (l, pt[j * n + b], 0, 0))
    lpage = lambda j: pl.BlockSpec((None, None, FOX_HEADS, PAGE_SIZE), lambda b, pt: (l, pt[j * n + b], 0, 0))
    scr = lambda r: pltpu.VMEM((r, n_pages, PAGE_SIZE), F32)
    return pl.pallas_call(
        partial(_fox_step_kernel, n_pages=n_pages),
        grid_spec=pltpu.PrefetchScalarGridSpec(
            num_scalar_prefetch=1,
            grid=(n,),
            in_specs=[page(j) for j in range(n_pages)] + [lpage(j) for j in range(n_pages)]
                     + [res(256), res(512), res(128), c2((PAGE_SIZE, PAGE_SIZE)), c2((PAGE_SIZE, PAGE_SIZE)),
                        c2((n_pages, n_pages))],
            out_specs=pl.BlockSpec((n, 256), lambda b, pt: (0, 0)),
            scratch_shapes=[scr(FOX_HEADS), scr(SUBLANES), scr(SUBLANES)]),
        out_shape=jax.ShapeDtypeStruct((n, 256), F32),
        compiler_params=_cparams(("arbitrary",)),
        name="fox_step",
    )(pt, *([kv_t] * n_pages), *([lf_t] * n_pages), fq, foxr, flog, cn['usuf'], cn['ones128'], cn['lsu'])


def _hgrn_step_kernel(q_ref, f_ref, v_ref, g_ref, lb_ref, go_ref, s_ref, o_ref, so_ref, t_ref):
    off = pl.multiple_of((pl.program_id(0) % 2) * HG_DK, HG_DK)

    def head_t(ref):
        t_ref[...] = ref[...].T
        return t_ref[pl.ds(off, HG_DK), :]

    q, hf, v, hgate = head_t(q_ref), head_t(f_ref), head_t(v_ref), head_t(g_ref)
    lb = lb_ref[...]
    f = lb + (1.0 - lb) * _sigmoid(hf)
    sn = f[:, None, :] * s_ref[...] + (1.0 - f)[:, None, :] * v[None, :, :]
    so_ref[...] = sn
    o = jnp.sum(q[:, None, :] * sn, axis=0)
    o = o * lax.rsqrt(jnp.mean(o * o, axis=0, keepdims=True) + EPS) * go_ref[...]
    o_ref[...] = o * _silu(hgate)


def hgrn_step(hg, state_t, l, lw):
    n = hg.shape[0]
    col = lambda base: pl.BlockSpec((n, LANES), lambda h: (0, base + h // 2))
    return pl.pallas_call(
        _hgrn_step_kernel,
        grid=(HG_HEADS,),
        in_specs=[col(0), col(2), col(4), col(6),
                  pl.BlockSpec((None, HG_DK, 1), lambda h: (h, 0, 0)),
                  pl.BlockSpec((HG_DV, 1), lambda h: (0, 0)),
                  pl.BlockSpec((None, None, HG_DK, HG_DV, n), lambda h: (l, h, 0, 0, 0))],
        out_specs=[pl.BlockSpec((HG_DV, n), lambda h: (h, 0)),
                   pl.BlockSpec((None, HG_DK, HG_DV, n), lambda h: (h, 0, 0, 0))],
        out_shape=[jax.ShapeDtypeStruct((HG_HEADS * HG_DV, n), F32),
                   jax.ShapeDtypeStruct((HG_HEADS, HG_DK, HG_DV, n), F32)],
        scratch_shapes=[pltpu.VMEM((LANES, n), F32)],
        compiler_params=_cparams(("arbitrary",)),
        name="hgrn_step",
    )(hg, hg, hg, hg, lw['lb_t'], lw['go_t'], state_t)


def _t2(x):
    return jnp.concatenate([x[:, 0:LANES].T, x[:, LANES:2 * LANES].T], axis=0)


def _s5_step_kernel(u_ref, h_ref, bre_ref, bim_ref, ar_ref, ai_ref, cre_ref, cim_ref, d_ref, gw_ref, gb_ref,
                    o_ref, ho_ref):
    u = u_ref[...]
    ut = _t2(u).astype(BF16)
    h0r, h0i = h_ref[:, 0, :], h_ref[:, 1, :]
    ar, ai = ar_ref[...], ai_ref[...]
    hr = ar * h0r - ai * h0i + _dot(bre_ref[...], ut)
    hi = ar * h0i + ai * h0r + _dot(bim_ref[...], ut)
    ho_ref[:, 0, :] = hr
    ho_ref[:, 1, :] = hi
    yt = _dot(cre_ref[...], hr.astype(BF16)) - _dot(cim_ref[...], hi.astype(BF16))
    y = jnp.concatenate([yt[0:LANES, :].T, yt[LANES:2 * LANES, :].T], axis=1) + d_ref[...] * u
    z = _gelu_tanh(y)
    o_ref[...] = z * _sigmoid(_dot(z.astype(BF16), gw_ref[...]) + gb_ref[...])


def s5_step(u, state_t, l, lw):
    n = u.shape[0]
    c2 = lambda shape: pl.BlockSpec(shape, lambda i: (0,) * len(shape))
    return pl.pallas_call(
        _s5_step_kernel,
        grid=(1,),
        in_specs=[c2((n, MIX_W)), pl.BlockSpec((None, S5_LANES, 2, n), lambda i: (l, 0, 0, 0)),
                  c2((S5_LANES, MIX_W)), c2((S5_LANES, MIX_W)), c2((S5_LANES, 1)), c2((S5_LANES, 1)),
                  c2((MIX_W, S5_LANES)), c2((MIX_W, S5_LANES)), c2((1, MIX_W)), c2((MIX_W, MIX_W)), c2((1, MIX_W))],
        out_specs=[c2((n, MIX_W)), c2((S5_LANES, 2, n))],
        out_shape=[jax.ShapeDtypeStruct((n, MIX_W), F32), jax.ShapeDtypeStruct((S5_LANES, 2, n), F32)],
        compiler_params=_cparams(("arbitrary",)),
        name="s5_step",
    )(u, state_t, lw['s5_bre_t'], lw['s5_bim_t'], lw['s5_abr_c'], lw['s5_abi_c'], lw['s5_cre_t'], lw['s5_cim_t'],
      lw['s5_d'], lw['s5_gw'], lw['s5_gb'])


def _rope_tables(pos, reps):
    half = HEAD_DIM // 2
    inv = ROPE_THETA ** (-jnp.arange(half, dtype=F32) / half)
    ang = pos.astype(F32)[:, None] * inv[None, :]
    cos, sin = jnp.cos(ang), jnp.sin(ang)
    return (jnp.tile(jnp.concatenate([cos, cos], axis=1), (1, reps)),
            jnp.tile(jnp.concatenate([-sin, sin], axis=1), (1, reps)))


def make_consts(l_prompt, n_sample, past_len):
    cn = {}
    seg = np.arange(256) // HEAD_DIM
    cn['bd256'] = jnp.asarray(seg[:, None] == seg[None, :], BF16)
    cn['eye64'] = jnp.asarray(np.eye(HG_DK), BF16)
    cst, msk = _hgrn_constants()
    cn['hg_cst'] = jnp.asarray(cst, BF16)
    cn['hg_msk'] = jnp.asarray(msk, F32)
    tc = min(512, l_prompt)
    cn['tri'] = jnp.asarray(np.tril(np.ones((tc, tc))), BF16)
    n16 = l_prompt // CMP_STRIDE
    ci = np.arange(n16)[:, None]
    sj = np.arange(LANES)[None, :]
    cover = ((ci * CMP_STRIDE < sj * SEL_BLOCK + SEL_BLOCK) & (ci * CMP_STRIDE + CMP_LEN - 1 >= sj * SEL_BLOCK)
             & (ci < n16 - 1) & (sj < -(-l_prompt // SEL_BLOCK)))
    cn['cover'] = jnp.asarray(cover, BF16)
    cn['expand'] = jnp.asarray(np.arange(LANES)[:, None] == (np.arange(l_prompt)[None, :] // SEL_BLOCK), BF16)
    cn['cos_p'], cn['sin_p'] = _rope_tables(jnp.arange(l_prompt), 4)
    cn['cos_s'], cn['sin_s'] = _rope_tables(jnp.full((n_sample,), past_len), 4)
    n16s = past_len // CMP_STRIDE
    ci = np.arange(n16s)[:, None]
    cover_s = ((ci * CMP_STRIDE < sj * SEL_BLOCK + SEL_BLOCK) & (ci * CMP_STRIDE + CMP_LEN - 1 >= sj * SEL_BLOCK)
               & (ci < n16s - 1) & (sj < -(-(past_len + 1) // SEL_BLOCK)))
    cn['cover_s'] = jnp.asarray(cover_s, BF16)
    n_pages = past_len // PAGE_SIZE
    bpp = PAGE_SIZE // SEL_BLOCK
    cn['amask'] = jnp.asarray(np.arange(LANES)[None, :] // bpp == np.arange(n_pages)[:, None], F32)
    cn['ex'] = jnp.asarray(np.arange(LANES)[:, None] % bpp == np.arange(PAGE_SIZE)[None, :] // SEL_BLOCK, BF16)
    cn['eye128'] = jnp.asarray(np.eye(LANES), BF16)
    tp = np.arange(PAGE_SIZE)
    cn['usuf'] = jnp.asarray(tp[:, None] > tp[None, :], BF16)
    cn['ones128'] = jnp.ones((PAGE_SIZE, PAGE_SIZE), BF16)
    jp = np.arange(n_pages)
    cn['lsu'] = jnp.asarray(jp[None, :] > jp[:, None], BF16)
    return cn


def prep_layer(l, P, lb_all):
    lw = {}
    w = P['w_in'][l]
    pts = np.concatenate([[0], np.cumsum(IN_SIZES)])
    seg = lambda i: w[:, pts[i]:pts[i + 1]]
    z = lambda n: jnp.zeros((D_MODEL, n), w.dtype)
    lw['w_in'] = jnp.concatenate([seg(0), seg(1), seg(2), z(LANES - IN_SIZES[2])] + [seg(i) for i in range(3, 12)]
                                 + [z(LANES - IN_SIZES[11])], axis=1).astype(BF16)
    lw['wg'] = seg(12).astype(BF16)
    lw['attn_norm'] = P['attn_norm'][l].reshape(1, D_MODEL)
    lw['gq'] = jnp.tile(P['nsa_q_norm'][l], 4).reshape(1, 256)
    kn = P['nsa_k_norm'][l]
    lw['gk'] = jnp.concatenate([jnp.tile(kn[1], 4), jnp.tile(kn[2], 2)]).reshape(1, 384)
    lw['gc'] = kn[0].reshape(1, HEAD_DIM)
    lw['gfq'] = jnp.tile(P['fox_q_norm'][l], 4).reshape(1, 256)
    lw['gfk'] = jnp.tile(P['fox_k_norm'][l], 4).reshape(1, 256)
    lw['fbias'] = jnp.pad(P['fox_f_bias'][l], (0, LANES - FOX_HEADS)).reshape(1, LANES)
    wk = P['nsa_cmp_wk'][l].reshape(2, CMP_STRIDE, HEAD_DIM, HEAD_DIM)
    wv = P['nsa_cmp_wv'][l].reshape(2, CMP_STRIDE, HEAD_DIM, HEAD_DIM)
    we = jnp.zeros((CMP_STRIDE, 4, HEAD_DIM, 4, HEAD_DIM), F32)
    we = we.at[:, 0, :, 0, :].set(wk[0]).at[:, 1, :, 1, :].set(wv[0])
    we = we.at[:, 0, :, 2, :].set(wk[1]).at[:, 1, :, 3, :].set(wv[1])
    lw['we'] = we.reshape(CMP_STRIDE * 256, 256).astype(BF16)
    lw['wcmp'] = we.reshape(CMP_STRIDE, 256, 256)[:, 0:2 * HEAD_DIM, :].astype(BF16)
    pe = P['nsa_cmp_pe'][l].reshape(1, CMP_LEN * HEAD_DIM)
    pe_hi = pe.astype(BF16)
    pe_lo = (pe - pe_hi.astype(F32)).astype(BF16)
    lw['pe8'] = jnp.concatenate([pe_hi, pe_lo, jnp.zeros((SUBLANES - 2, CMP_LEN * HEAD_DIM), BF16)], axis=0)
    lw['wkv'] = jnp.concatenate([P['nsa_cmp_wk'][l], P['nsa_cmp_wv'][l]], axis=1).astype(BF16)
    lw['lb'] = lb_all[l].reshape(1, 256)
    lw['go'] = jnp.tile(P['hgrn_o_norm'][l], 4).reshape(1, 256)
    lw['lb_t'] = lb_all[l].reshape(HG_HEADS, HG_DK, 1)
    lw['go_t'] = P['hgrn_o_norm'][l].reshape(HG_DV, 1)
    dt = jnp.exp(P['s5_log_dt'][l].astype(F32))[:, None]
    ar, ai = P['s5_a_re'][l].astype(F32), P['s5_a_im'][l].astype(F32)
    mag = jnp.exp(dt * ar)
    abr, abi = mag * jnp.cos(dt * ai), mag * jnp.sin(dt * ai)
    den = ar * ar + ai * ai
    zr = ((abr - 1.0) * ar + abi * ai) / den
    zi = (abi * ar - (abr - 1.0) * ai) / den
    br_, bi_ = P['s5_b_re'][l].astype(F32), P['s5_b_im'][l].astype(F32)
    bbr = zr[..., None] * br_ - zi[..., None] * bi_
    bbi = zr[..., None] * bi_ + zi[..., None] * br_
    eye_g = jnp.eye(S5_GROUPS, dtype=F32)
    lw['s5_bre'] = jnp.einsum('gpc,gh->gchp', bbr, eye_g).reshape(MIX_W, S5_LANES).astype(BF16)
    lw['s5_bim'] = jnp.einsum('gpc,gh->gchp', bbi, eye_g).reshape(MIX_W, S5_LANES).astype(BF16)
    lw['s5_cre'] = jnp.einsum('gcp,gh->gphc', P['s5_c_re'][l].astype(F32), eye_g).reshape(S5_LANES, MIX_W).astype(BF16)
    lw['s5_cim'] = jnp.einsum('gcp,gh->gphc', P['s5_c_im'][l].astype(F32), eye_g).reshape(S5_LANES, MIX_W).astype(BF16)
    abr_f, abi_f = abr.reshape(1, S5_LANES), abi.reshape(1, S5_LANES)
    lw['s5_abr_c'], lw['s5_abi_c'] = abr.reshape(S5_LANES, 1), abi.reshape(S5_LANES, 1)
    for k in ('s5_bre', 's5_bim', 's5_cre', 's5_cim'):
        lw[k + '_t'] = lw[k].T
    lw['s5_ar8'] = jnp.tile(abr_f, (SUBLANES, 1))
    lw['s5_ai8'] = jnp.concatenate([jnp.tile(-abi_f, (S5_BATCH, 1)), jnp.tile(abi_f, (S5_BATCH, 1))], axis=0)
    lw['s5_d'] = P['s5_d'][l].reshape(1, MIX_W)
    lw['s5_gw'] = P['s5_glu_w'][l].astype(BF16)
    lw['s5_gb'] = P['s5_glu_b'][l].reshape(1, MIX_W)
    lw['wb'] = P['w_branch'][l].astype(BF16)
    lw['wo'] = P['w_out'][l].astype(BF16)
    lw['ffn_g'] = P['ffn_norm'][l].reshape(1, D_MODEL)
    lw['ffn_wu'] = P['ffn_w_up'][l].astype(BF16)
    lw['ffn_cw'] = P['ffn_conv_w'][l]
    lw['ffn_cb'] = P['ffn_conv_b'][l].reshape(1, D_FF)
    lw['ffn_wd'] = P['ffn_w_down'][l].astype(BF16)
    return lw


def prompt_layer(x, lw, cn, b, l):
    r3 = lambda a: a.reshape(b, l, a.shape[-1])
    (hb, qn, qr, rows, win, ng, hg, su, fq, foxr, flog, rows_b, win_b, fox_b, rows_t, win_t, fox_t, flog_t
     ) = in_proj(x, lw, cn['bd256'], cn['cos_p'], cn['sin_p'], 256, b)
    pparts = matmul(rows.reshape(b * l // CMP_STRIDE, CMP_STRIDE * 256), lw['we']).reshape(b, l // CMP_STRIDE, 256)
    oc, mask = nsa_cmp_prompt(r3(qn), pparts, lw, cn['cover'])
    o_nsa = nsa_flash_prompt(r3(qr), r3(rows_b), r3(win_b), mask, cn['expand'], r3(ng), oc)
    c = cumsum_time(r3(flog), cn['tri'])
    c_t = jnp.swapaxes(c[:, :, 0:SUBLANES], 1, 2)
    o_fox = fox_flash_prompt(r3(fq), r3(fox_b), c, c_t)
    o_h, hg_new = hgrn_prompt(r3(hg), lw, cn)
    o_s5, s5h = s5_prompt(r3(su), lw)
    flat = lambda a: a.reshape(b * l, a.shape[-1])
    x1 = merge(x, hb, (flat(o_nsa), flat(o_h), flat(o_s5), flat(o_fox)), lw, 256)
    x2, conv_new = ffn_prompt(x1, lw, b, l)
    nw = min(WINDOW, l)
    s5_new = jnp.moveaxis(s5h.reshape(2, S5_BATCH, S5_GROUPS, S5_STATE), 0, -1)
    return (x2,) + _kv_outputs(rows_t, win_t[:, :, l - nw:], fox_t, flog_t) + (hg_new, s5_new, conv_new)


def _kv_outputs(rows_t, win_t, fox_t, flog_t):
    b, _, l = rows_t.shape
    return (jnp.transpose(rows_t.reshape(b, 4, HEAD_DIM, l), (0, 3, 1, 2)),
            jnp.transpose(win_t.reshape(b, 2, HEAD_DIM, win_t.shape[-1]), (0, 3, 1, 2)),
            jnp.transpose(fox_t.reshape(b, 2, FOX_HEADS, HEAD_DIM, l), (0, 4, 1, 2, 3)),
            jnp.transpose(flog_t[:, 0:FOX_HEADS, :], (0, 2, 1)))


def rmsnorm(x, g):
    xf = x.astype(F32)
    y = xf * lax.rsqrt(jnp.mean(xf * xf, axis=-1, keepdims=True) + EPS)
    return (y * g.astype(F32)).astype(x.dtype)


def rope(x, pos):
    half = HEAD_DIM // 2
    inv = ROPE_THETA ** (-jnp.arange(half, dtype=F32) / half)
    ang = pos.astype(F32)[:, None] * inv[None, :]
    shp = (pos.shape[0],) + (1,) * (x.ndim - 3) + (half,)
    cos, sin = jnp.cos(ang).reshape(shp), jnp.sin(ang).reshape(shp)
    xf = x.astype(F32)
    x1, x2 = xf[..., :half], xf[..., half:]
    return jnp.concatenate([x1 * cos - x2 * sin, x2 * cos + x1 * sin], axis=-1).astype(x.dtype)


def masked_softmax(logits, mask):
    lg = jnp.where(mask, logits.astype(F32), -jnp.inf)
    m = jnp.max(lg, axis=-1, keepdims=True)
    m = jnp.where(jnp.isfinite(m), m, 0.0)
    e = jnp.where(mask, jnp.exp(lg - m), 0.0)
    return e / jnp.maximum(jnp.sum(e, axis=-1, keepdims=True), 1e-30)


def gather_pages(pool, page_table):
    g = pool[page_table]
    return g.reshape((g.shape[0], g.shape[1] * g.shape[2]) + g.shape[3:])


def nsa_compress(rows, pe, w):
    B, T, _ = rows.shape
    n_c = (T - CMP_LEN) // CMP_STRIDE + 1
    idx = jnp.arange(n_c)[:, None] * CMP_STRIDE + jnp.arange(CMP_LEN)[None, :]
    blocks = rows[:, idx] + pe.astype(rows.dtype)
    return blocks.reshape(B, n_c, CMP_LEN * HEAD_DIM) @ w


def nsa_step_jax(q, rows, win, gates, past_len, win_pos0, k_norm_c, pe, wk, wv):
    B, Lq, H, _ = q.shape
    T = rows.shape[1]
    Tw = win.shape[1]
    q_rot = rope(q, past_len + jnp.arange(Lq))
    k_c = rmsnorm(nsa_compress(rows[:, :, 0], pe, wk), k_norm_c)
    v_c = nsa_compress(rows[:, :, 1], pe, wv).astype(F32)
    n_c = k_c.shape[1]
    c_start = jnp.arange(n_c) * CMP_STRIDE
    c_end = c_start + CMP_LEN - 1
    n_s = -(-T // SEL_BLOCK)
    s_start = jnp.arange(n_s) * SEL_BLOCK
    cover = ((c_start[:, None] < s_start[None, :] + SEL_BLOCK) & (c_end[:, None] >= s_start[None, :])).astype(F32)
    sel_rows = jnp.pad(rows[:, :, 2:4], ((0, 0), (0, n_s * SEL_BLOCK - T), (0, 0), (0, 0)))
    sel_rows = sel_rows.reshape(B, n_s, SEL_BLOCK, 2, HEAD_DIM)
    k_top = min(N_SEL, n_s)
    qb = Lq
    kw_all = jnp.pad(win, ((0, 0), (WINDOW, 0), (0, 0), (0, 0)))
    buf_len = past_len - win_pos0
    bidx = jnp.arange(B)[:, None, None]
    pos = past_len + jnp.arange(qb)
    pc = masked_softmax(jnp.einsum('bqhd,bcd->bhqc', q, k_c) * SCALE, (c_end[None, :] <= pos[:, None])[None, None])
    o_c = jnp.einsum('bhqc,bcd->bqhd', pc, v_c)
    imp = jnp.einsum('bhqc,cs->bqs', pc, cover)
    cur = pos // SEL_BLOCK
    j = jnp.arange(n_s)[None, :]
    forced = (j == 0) | (j == cur[:, None]) | (j == cur[:, None] - 1)
    score = jnp.where(j > cur[:, None], -jnp.inf, imp + SEL_FORCE * forced)
    _, sel = lax.top_k(score, k_top)
    chosen = jnp.any(sel[..., None] == jnp.arange(n_s), axis=2)
    tpos = jnp.arange(T)
    ms = chosen[:, :, tpos // SEL_BLOCK] & (tpos[None, None, :] <= pos[None, :, None])
    ls = jnp.einsum('bqhd,btd->bhqt', q_rot, rows[:, :, 2]) * SCALE
    ps = masked_softmax(ls, ms[:, None])
    o_s = jnp.einsum('bhqt,btd->bqhd', ps, rows[:, :, 3].astype(F32))
    start = buf_len
    kwin = lax.dynamic_slice_in_dim(kw_all, start, WINDOW + qb, axis=1)
    wpos = win_pos0 - WINDOW + start + jnp.arange(WINDOW + qb)
    mw = ((wpos[None, :] >= win_pos0) & (wpos[None, :] < win_pos0 + Tw)
          & (wpos[None, :] <= pos[:, None]) & (pos[:, None] - wpos[None, :] < WINDOW))
    pw = masked_softmax(jnp.einsum('bqhd,bkd->bhqk', q_rot, kwin[:, :, 0]) * SCALE, mw[None, None])
    o_w = jnp.einsum('bhqk,bkd->bqhd', pw, kwin[:, :, 1].astype(F32))
    gs = jax.nn.sigmoid(gates.astype(F32))[..., None]
    o = gs[:, :, :, 0] * o_c + gs[:, :, :, 1] * o_s + gs[:, :, :, 2] * o_w
    return o.reshape(B, Lq, H * HEAD_DIM)


def fox_step_jax(q, k, v, logf, past_len):
    B, Lq, H, _ = q.shape
    T = k.shape[1]
    c = jnp.cumsum(logf.astype(F32), axis=1)
    cT = jnp.moveaxis(c, 2, 1)
    pos = past_len + jnp.arange(Lq)
    bias = jnp.moveaxis(c[:, past_len:], 2, 1)[..., None] - cT[:, :, None, :]
    lg = jnp.einsum('bqhd,bkhd->bhqk', q, k).astype(F32) * SCALE + bias
    p = masked_softmax(lg, (jnp.arange(T)[None, :] <= pos[:, None])[None, None])
    return jnp.einsum('bhqk,bkhd->bqhd', p, v.astype(F32))


def sample_layer(x, lw, cn, l, views, pt):
    n = x.shape[0]
    kv_t, win_t, fkv_t, flf_t, hg_t, s5_t, conv = views
    (hb, qn, qr, rows, win, ng, hg, su, fq, foxr, flog, _, _, _, rows_t, _, fox_t, flog_t
     ) = in_proj(x, lw, cn['bd256'], cn['cos_s'], cn['sin_s'], n, 1)
    o_nsa, nwin_t = nsa_step(pt, kv_t, win_t, l, qn, qr, rows, win, ng, lw, cn)
    o_fox = fox_step(pt, fkv_t, flf_t, l, fq, foxr, flog, cn)
    oh_t, hgn_t = hgrn_step(hg, hg_t, l, lw)
    o_s5, s5n_t = s5_step(su, s5_t, l, lw)
    x1 = merge(x, hb, (o_nsa, oh_t.T, o_s5, o_fox), lw, n)
    x2, conv_new = ffn_step(x1, conv, l, lw)
    wb = nwin_t.shape[-1]
    new_win = jnp.transpose(nwin_t.reshape(n, 2, HEAD_DIM, wb), (0, 3, 1, 2))
    hg_new = jnp.transpose(hgn_t, (3, 0, 1, 2))
    s5_new = jnp.transpose(s5n_t.reshape(S5_GROUPS, S5_STATE, 2, n), (3, 0, 1, 2))
    rows_o, _, fox_o, flog_o = _kv_outputs(rows_t, rows_t[:, 0:2 * HEAD_DIM], fox_t, flog_t)
    seq_first = lambda a: jnp.swapaxes(a, 0, 1)
    return (x2, seq_first(rows_o), new_win, seq_first(fox_o), seq_first(flog_o), hg_new, s5_new, conv_new)


def sample_views(caches, page_table):
    cache_nsa_kv, cache_nsa_win, cache_fox_kv, cache_fox_logf, state_hgrn, state_s5, state_conv = caches
    d, n_pool = cache_nsa_kv.shape[0:2]
    n = cache_nsa_win.shape[1]
    kv_t = jnp.transpose(cache_nsa_kv, (0, 1, 3, 4, 2)).reshape(d, n_pool, 4 * HEAD_DIM, PAGE_SIZE)
    win_t = jnp.transpose(cache_nsa_win, (0, 1, 3, 4, 2)).reshape(d, n, 2 * HEAD_DIM, cache_nsa_win.shape[2])
    fkv_t = jnp.transpose(cache_fox_kv, (0, 1, 3, 4, 5, 2)).reshape(d, n_pool, 2 * FOX_HEADS * HEAD_DIM, PAGE_SIZE)
    flf_t = jnp.transpose(cache_fox_logf, (0, 1, 3, 2))
    hg_t = jnp.transpose(state_hgrn, (0, 2, 3, 4, 1))
    s5_t = jnp.transpose(state_s5, (0, 2, 3, 4, 1)).reshape(d, S5_LANES, 2, n)
    pt = jnp.transpose(page_table).reshape(-1)
    return (kv_t, win_t, fkv_t, flf_t, hg_t, s5_t, state_conv), pt


def _old_sample_layer_jax(x, lw, cn, P, l, caches, page_table, lb):
    n = x.shape[0]
    cache_nsa_kv, cache_nsa_win, cache_fox_kv, cache_fox_logf, state_hgrn, state_s5, state_conv = caches
    hb, qn, qr, rows, win, ng, hg, su, fq, foxr, flog = in_proj(x, lw, cn['bd256'], cn['cos_s'], cn['sin_s'], n)
    nsa_past = gather_pages(cache_nsa_kv, page_table)
    past_len = nsa_past.shape[1]
    rows_full = jnp.concatenate([nsa_past, rows.reshape(n, 1, 4, HEAD_DIM)], axis=1)
    win_full = jnp.concatenate([cache_nsa_win, win.reshape(n, 1, 2, HEAD_DIM)], axis=1)
    o_nsa = nsa_step_jax(qn.reshape(n, 1, NSA_HEADS, HEAD_DIM), rows_full, win_full,
                         ng[:, 0:3 * NSA_HEADS].reshape(n, 1, NSA_HEADS, 3), past_len,
                         past_len - cache_nsa_win.shape[1], P['nsa_k_norm'][l][0], P['nsa_cmp_pe'][l],
                         P['nsa_cmp_wk'][l], P['nsa_cmp_wv'][l]).reshape(n, MIX_W)
    new_win = win_full[:, win_full.shape[1] - min(WINDOW, win_full.shape[1]):]
    hq, hf, hv, hgate = (hg[:, i * 256:(i + 1) * 256].reshape(n, HG_HEADS, HG_DK) for i in range(4))
    lbh = lb.reshape(HG_HEADS, HG_DK)
    f = lbh + (1.0 - lbh) * jax.nn.sigmoid(hf)
    hg_new = f[..., None] * state_hgrn + (1.0 - f)[..., None] * hv[:, :, None, :]
    o_h = jnp.einsum('bhk,bhkv->bhv', hq, hg_new)
    o_h = (rmsnorm(o_h, P['hgrn_o_norm'][l]) * jax.nn.silu(hgate)).reshape(n, MIX_W)
    abr, abi = lw['s5_abr'], lw['s5_abi']
    h0r, h0i = state_s5[..., 0].reshape(n, S5_LANES), state_s5[..., 1].reshape(n, S5_LANES)
    hr = abr * h0r - abi * h0i + su @ lw['s5_bre'].astype(F32)
    hi = abr * h0i + abi * h0r + su @ lw['s5_bim'].astype(F32)
    y5 = hr @ lw['s5_cre'].astype(F32) - hi @ lw['s5_cim'].astype(F32) + lw['s5_d'] * su
    z5 = jax.nn.gelu(y5)
    o_s5 = z5 * jax.nn.sigmoid(z5 @ lw['s5_gw'].astype(F32) + lw['s5_gb'])
    s5_new = jnp.stack([hr.reshape(n, S5_GROUPS, S5_STATE), hi.reshape(n, S5_GROUPS, S5_STATE)], axis=-1)
    fox_rows = foxr.reshape(n, 1, 2, FOX_HEADS, HEAD_DIM)
    kv_full = jnp.concatenate([gather_pages(cache_fox_kv, page_table), fox_rows], axis=1)
    flog4 = flog[:, 0:FOX_HEADS].reshape(n, 1, FOX_HEADS)
    lf_full = jnp.concatenate([gather_pages(cache_fox_logf, page_table), flog4], axis=1)
    o_fox = fox_step_jax(fq.reshape(n, 1, FOX_HEADS, HEAD_DIM), kv_full[:, :, 0], kv_full[:, :, 1], lf_full,
                         past_len).reshape(n, MIX_W)
    x1 = merge(x, hb, (o_nsa, o_h, o_s5, o_fox), lw, n)
    x2, a_new = ffn_step(x1, state_conv[:, 0], state_conv[:, 1], lw)
    conv_new = jnp.stack([state_conv[:, 1], a_new], axis=1)
    return (x2, rows.reshape(n, 1, 4, HEAD_DIM), new_win, fox_rows, flog4, hg_new, s5_new, conv_new)


PARAM_NAMES = ('attn_norm', 'w_in', 'nsa_q_norm', 'nsa_k_norm', 'nsa_cmp_pe', 'nsa_cmp_wk', 'nsa_cmp_wv',
               'hgrn_o_norm', 's5_a_re', 's5_a_im', 's5_log_dt', 's5_b_re', 's5_b_im', 's5_c_re', 's5_c_im',
               's5_d', 's5_glu_w', 's5_glu_b', 'fox_q_norm', 'fox_k_norm', 'fox_f_bias', 'w_branch', 'w_out',
               'ffn_norm', 'ffn_w_up', 'ffn_conv_w', 'ffn_conv_b', 'ffn_w_down')


def trunk(x_prompt, x_sample, caches, page_table, hgrn_lb_logits, P):
    depth = P['w_in'].shape[0]
    bp, lp, _ = x_prompt.shape
    ns = x_sample.shape[0]
    past_len = page_table.shape[1] * PAGE_SIZE
    sm = jax.nn.softmax(hgrn_lb_logits.astype(F32), axis=0)
    lb_all = jnp.cumsum(sm, axis=0) - sm[0]
    cn = make_consts(lp, ns, past_len)
    yp = x_prompt.reshape(bp * lp, D_MODEL)
    ys = x_sample.reshape(ns, D_MODEL)
    outs_p, outs_s = [], []
    views, pt = sample_views(caches, page_table)
    for l in range(depth):
        lw = prep_layer(l, P, lb_all)
        res_p = prompt_layer(yp, lw, cn, bp, lp)
        res_s = sample_layer(ys, lw, cn, l, views, pt)
        yp, ys = res_p[0], res_s[0]
        outs_p.append(res_p[1:])
        outs_s.append(res_s[1:])
    sp = [jnp.stack([o[i] for o in outs_p], axis=0) for i in range(7)]
    ss = [jnp.stack([o[i] for o in outs_s], axis=0) for i in range(7)]
    nsa_kv_p, nsa_win_p, fox_kv_p, fox_logf_p, hgrn_p, s5_p, conv_p = sp
    nsa_kv_s, nsa_win_s, fox_kv_s, fox_logf_s, hgrn_s, s5_s, conv_s = ss
    return (yp.reshape(bp, lp, D_MODEL), ys.reshape(ns, 1, D_MODEL), nsa_kv_p, nsa_kv_s, nsa_win_p, nsa_win_s,
            fox_kv_p, fox_kv_s, fox_logf_p, fox_logf_s, hgrn_p, hgrn_s, s5_p, s5_s, conv_p, conv_s)


def kernel(x_prompt, x_sample, cache_nsa_kv, cache_nsa_win, cache_fox_kv, cache_fox_logf, state_hgrn, state_s5, state_ffn_conv, page_table, attn_norm, w_in, nsa_q_norm, nsa_k_norm, nsa_cmp_pe, nsa_cmp_wk, nsa_cmp_wv, hgrn_lb_logits, hgrn_o_norm, s5_a_re, s5_a_im, s5_log_dt, s5_b_re, s5_b_im, s5_c_re, s5_c_im, s5_d, s5_glu_w, s5_glu_b, fox_q_norm, fox_k_norm, fox_f_bias, w_branch, w_out, ffn_norm, ffn_w_up, ffn_conv_w, ffn_conv_b, ffn_w_down):
    P = dict(attn_norm=attn_norm, w_in=w_in, nsa_q_norm=nsa_q_norm, nsa_k_norm=nsa_k_norm,
             nsa_cmp_pe=nsa_cmp_pe, nsa_cmp_wk=nsa_cmp_wk, nsa_cmp_wv=nsa_cmp_wv, hgrn_o_norm=hgrn_o_norm,
             s5_a_re=s5_a_re, s5_a_im=s5_a_im, s5_log_dt=s5_log_dt, s5_b_re=s5_b_re, s5_b_im=s5_b_im,
             s5_c_re=s5_c_re, s5_c_im=s5_c_im, s5_d=s5_d, s5_glu_w=s5_glu_w, s5_glu_b=s5_glu_b,
             fox_q_norm=fox_q_norm, fox_k_norm=fox_k_norm, fox_f_bias=fox_f_bias, w_branch=w_branch,
             w_out=w_out, ffn_norm=ffn_norm, ffn_w_up=ffn_w_up, ffn_conv_w=ffn_conv_w,
             ffn_conv_b=ffn_conv_b, ffn_w_down=ffn_w_down)
    caches = (cache_nsa_kv, cache_nsa_win, cache_fox_kv, cache_fox_logf, state_hgrn, state_s5, state_ffn_conv)
    return trunk(x_prompt, x_sample, caches, page_table, hgrn_lb_logits, P)
```

```python
import math
from functools import partial

import numpy as np
import jax
import jax.numpy as jnp
from jax import lax
from jax.experimental import pallas as pl
from jax.experimental.pallas import tpu as pltpu

D_MODEL = 1024
DEPTH = 4
PAGE_SIZE = 128
HEAD_DIM = 64
N_BRANCH = 4
MIX_W = D_MODEL // N_BRANCH
NSA_HEADS = MIX_W // HEAD_DIM
CMP_LEN = 32
CMP_STRIDE = 16
SEL_BLOCK = 64
N_SEL = 16
WINDOW = 512
SEL_FORCE = 1000.0
HG_HEADS = 4
HG_DK = MIX_W // HG_HEADS
HG_DV = MIX_W // HG_HEADS
HG_CHUNK = 64
S5_GROUP = 16
S5_GROUPS = MIX_W // S5_GROUP
S5_STATE = 64
S5_LANES = S5_GROUPS * S5_STATE
FOX_HEADS = MIX_W // HEAD_DIM
Q_BLOCK = 128
D_FF = ((8 * D_MODEL // 3 + 127) // 128) * 128
CONV_W = 3
ROPE_THETA = 10000.0
EPS = 1e-6
SCALE = HEAD_DIM ** -0.5
NEG = -1e30

IN_SIZES = (NSA_HEADS * HEAD_DIM, 6 * HEAD_DIM, 3 * NSA_HEADS,
            HG_HEADS * HG_DK, HG_HEADS * HG_DK, HG_HEADS * HG_DV, HG_HEADS * HG_DV,
            MIX_W,
            FOX_HEADS * HEAD_DIM, FOX_HEADS * HEAD_DIM, FOX_HEADS * HEAD_DIM, FOX_HEADS,
            N_BRANCH * D_MODEL)

LANES = 128
SUBLANES = 8
VMEM_LIMIT_BYTES = 56 * 1024 * 1024

F32 = jnp.float32
BF16 = jnp.bfloat16

C_NQ, C_NKV, C_NG, C_HG, C_SU, C_FQ, C_FK, C_FV, C_FF, C_END = 0, 256, 640, 768, 1792, 2048, 2304, 2560, 2816, 2944


def _cparams(sem):
    return pltpu.CompilerParams(dimension_semantics=sem, vmem_limit_bytes=VMEM_LIMIT_BYTES)


def _split2(x):
    hi = x.astype(BF16)
    lo = (x - hi.astype(F32)).astype(BF16)
    return hi, lo


def _split3(x):
    hi = x.astype(BF16)
    r = x - hi.astype(F32)
    mid = r.astype(BF16)
    lo = (r - mid.astype(F32)).astype(BF16)
    return hi, mid, lo


def _dot(a, b):
    return jnp.dot(a, b, preferred_element_type=F32)


def _dot_nt(a, b):
    return lax.dot_general(a, b, (((1,), (1,)), ((), ())), preferred_element_type=F32)


def _dot_tn(a, b):
    return lax.dot_general(a, b, (((0,), (0,)), ((), ())), preferred_element_type=F32)


def _dot_exact_lhs(a01, x):
    hi, mid, lo = _split3(x)
    return _dot(a01, hi) + _dot(a01, mid) + _dot(a01, lo)


def _dot_exact_rhs(x, b01):
    hi, mid, lo = _split3(x)
    return _dot(hi, b01) + _dot(mid, b01) + _dot(lo, b01)


def _seg_rms(x, bd, gain):
    ms = _dot_exact_rhs(x * x, bd) * (1.0 / HEAD_DIM)
    return x * lax.rsqrt(ms + EPS) * gain


def _rope(x, cos, sin_signed):
    w = x.shape[-1]
    half = HEAD_DIM // 2
    lane = lax.broadcasted_iota(jnp.int32, x.shape, 1)
    first = (lane % HEAD_DIM) < half
    swapped = jnp.where(first, pltpu.roll(x, w - half, 1), pltpu.roll(x, half, 1))
    return x * cos + swapped * sin_signed


def _log_sigmoid(x):
    return jnp.minimum(x, 0.0) - jnp.log(1.0 + jnp.exp(-jnp.abs(x)))


def _sigmoid(x):
    return 1.0 / (1.0 + jnp.exp(-x))


def _silu(x):
    return x * _sigmoid(x)


def _mm_kernel(x_ref, w_ref, o_ref):
    o_ref[...] = _dot(x_ref[...].astype(BF16), w_ref[...])


def matmul(x, w, tm=256):
    m, k = x.shape
    n = w.shape[1]
    tm = min(tm, m)
    return pl.pallas_call(
        _mm_kernel,
        grid=(m // tm,),
        in_specs=[pl.BlockSpec((tm, k), lambda i: (i, 0)),
                  pl.BlockSpec((k, n), lambda i: (0, 0))],
        out_specs=pl.BlockSpec((tm, n), lambda i: (i, 0)),
        out_shape=jax.ShapeDtypeStruct((m, n), F32),
        compiler_params=_cparams(("arbitrary",)),
        name="matmul",
    )(x, w.astype(BF16))


def _in_proj_kernel(x_ref, g_ref, w_ref, cos_ref, sin_ref, bd_ref, gq_ref, gk_ref, gfq_ref, gfk_ref, fb_ref,
                    h_ref, qn_ref, qr_ref, rows_ref, win_ref, ng_ref, hg_ref, su_ref, fq_ref, fox_ref, flog_ref,
                    rows_b_ref, win_b_ref, fox_b_ref, rows_t_ref, win_t_ref, fox_t_ref, flog_t_ref):
    x = x_ref[...]
    h = x * lax.rsqrt(jnp.mean(x * x, axis=-1, keepdims=True) + EPS) * g_ref[...]
    hb = h.astype(BF16)
    h_ref[...] = hb
    p = _dot(hb, w_ref[...])
    bd = bd_ref[...]
    cos, sin = cos_ref[...], sin_ref[...]
    qn = _seg_rms(p[:, C_NQ:C_NKV], bd, gq_ref[...])
    qn_ref[...] = qn
    qr_ref[...] = _rope(qn, cos, sin)
    raw = p[:, C_NKV:C_NKV + 256]
    lane = lax.broadcasted_iota(jnp.int32, raw.shape, 1)
    nr = _rope(_seg_rms(raw, bd, gk_ref[:, 0:256]), cos, sin)
    rows = jnp.where((lane >= 2 * HEAD_DIM) & (lane < 3 * HEAD_DIM), nr, raw)
    raww = p[:, C_NKV + 256:C_NG]
    nw = _rope(_seg_rms(raww, bd[0:128, 0:128], gk_ref[:, 256:384]), cos[:, 0:128], sin[:, 0:128])
    lane_w = lax.broadcasted_iota(jnp.int32, raww.shape, 1)
    win = jnp.where(lane_w < HEAD_DIM, nw, raww)
    ng_ref[...] = p[:, C_NG:C_HG]
    hg_ref[...] = p[:, C_HG:C_SU]
    su_ref[...] = p[:, C_SU:C_FQ]
    fq_ref[...] = _seg_rms(p[:, C_FQ:C_FK], bd, gfq_ref[...])
    fk = _seg_rms(p[:, C_FK:C_FV], bd, gfk_ref[...])
    fv = p[:, C_FV:C_FF]
    flog = _log_sigmoid(p[:, C_FF:C_END] + fb_ref[...])
    rows_ref[...] = rows
    win_ref[...] = win
    fox_ref[:, 0:256] = fk
    fox_ref[:, 256:512] = fv
    flog_ref[...] = flog
    rows_b_ref[...] = rows.astype(BF16)
    win_b_ref[...] = win.astype(BF16)
    fox_b_ref[:, 0:256] = fk.astype(BF16)
    fox_b_ref[:, 256:512] = fv.astype(BF16)
    rows_t_ref[...] = rows.T
    win_t_ref[...] = win.T
    fox_t_ref[0:256, :] = fk.T
    fox_t_ref[256:512, :] = fv.T
    flog_t_ref[...] = flog.T[0:SUBLANES, :]


def in_proj(x, lw, bd256, cos_t, sin_t, tm, nb):
    t = x.shape[0]
    seg = t // nb
    nt = seg // tm
    row = lambda i: (i, 0)
    const = lambda i: (0, 0)
    tab = lambda i: (i % nt, 0)
    tmap = lambda i: (i // nt, 0, i % nt)
    widths = (256, 256, 256, 128, 128, 1024, 256, 256, 512, 128)
    bwidths = (256, 128, 512)
    twidths = (256, 128, 512, SUBLANES)
    out_shape = ([jax.ShapeDtypeStruct((t, D_MODEL), BF16)] + [jax.ShapeDtypeStruct((t, w), F32) for w in widths]
                 + [jax.ShapeDtypeStruct((t, w), BF16) for w in bwidths]
                 + [jax.ShapeDtypeStruct((nb, w, seg), F32) for w in twidths])
    out_specs = ([pl.BlockSpec((tm, D_MODEL), row)] + [pl.BlockSpec((tm, w), row) for w in widths]
                 + [pl.BlockSpec((tm, w), row) for w in bwidths]
                 + [pl.BlockSpec((None, w, tm), tmap) for w in twidths])
    return pl.pallas_call(
        _in_proj_kernel,
        grid=(t // tm,),
        in_specs=[pl.BlockSpec((tm, D_MODEL), row), pl.BlockSpec((1, D_MODEL), const),
                  pl.BlockSpec((D_MODEL, C_END), const), pl.BlockSpec((tm, 256), tab), pl.BlockSpec((tm, 256), tab),
                  pl.BlockSpec((256, 256), const), pl.BlockSpec((1, 256), const), pl.BlockSpec((1, 384), const),
                  pl.BlockSpec((1, 256), const), pl.BlockSpec((1, 256), const), pl.BlockSpec((1, 128), const)],
        out_specs=out_specs,
        out_shape=out_shape,
        compiler_params=_cparams(("arbitrary",)),
        name="in_proj",
    )(x, lw['attn_norm'], lw['w_in'], cos_t, sin_t, bd256, lw['gq'], lw['gk'], lw['gfq'], lw['gfk'], lw['fbias'])


def _topk_mask(score, k, n_cand):
    r = -(-n_cand // SUBLANES) * SUBLANES
    s = score.T[0:r, :]
    idx_f = lax.broadcasted_iota(jnp.int32, s.shape, 0).astype(F32)
    sel = jnp.zeros(s.shape, F32)
    for _ in range(k):
        m = jnp.max(s, axis=0, keepdims=True)
        first = jnp.min(jnp.where(s == m, idx_f, float(LANES)), axis=0, keepdims=True)
        pick = idx_f == first
        sel = jnp.where(pick, 1.0, sel)
        s = jnp.where(pick, NEG, s)
    if r < LANES:
        sel = jnp.concatenate([sel, jnp.zeros((LANES - r, s.shape[1]), F32)], axis=0)
    return sel.T


def _topk_rank(score_row, k, eye, ones):
    n = score_row.shape[1]
    row_b = jnp.broadcast_to(score_row, (n, n))
    col_b = _dot_exact_rhs(eye.astype(F32) * row_b, ones)
    ii = lax.broadcasted_iota(jnp.int32, (n, n), 0)
    jj = lax.broadcasted_iota(jnp.int32, (n, n), 1)
    beats = (row_b > col_b) | ((row_b == col_b) & (jj < ii))
    return _dot(jnp.where(beats, 1.0, 0.0).astype(BF16), ones) < k


def _pe_term(pe_ref, wkv_ref):
    pew = _dot(pe_ref[...], wkv_ref[...])
    return pew[0:1, :] + pew[1:2, :]


def _cmp_tokens(p, pew, gain_c, n_c):
    n16 = p.shape[0]
    nxt = pltpu.roll(p, n16 - 1, 0)
    kc = p[:, 0:64] + nxt[:, 128:192] + pew[0:1, 0:64]
    vc = p[:, 64:128] + nxt[:, 192:256] + pew[0:1, 64:128]
    kc = kc * lax.rsqrt(jnp.mean(kc * kc, axis=-1, keepdims=True) + EPS) * gain_c
    return kc, vc


def _cmp_attend(q, kc, vc, valid):
    qh, ql = _split2(q)
    kh, kl = _split2(kc)
    s = _dot_nt(qh, kh) + _dot_nt(ql, kh) + _dot_nt(qh, kl)
    s = jnp.where(valid, s, NEG)
    m = jnp.max(s, axis=1, keepdims=True)
    e = jnp.where(valid, jnp.exp(s - m), 0.0)
    pc = e / jnp.maximum(jnp.sum(e, axis=1, keepdims=True), 1e-30)
    return pc, _dot(pc.astype(BF16), vc.astype(BF16))


def _cmp_kernel(q_ref, p_ref, pe_ref, wkv_ref, gc_ref, cover_ref, oc_ref, mask_ref, kc_ref, vc_ref, *, tq, n_c):
    qi = pl.program_id(1)

    @pl.when(qi == 0)
    def _():
        kc, vc = _cmp_tokens(p_ref[...], _pe_term(pe_ref, wkv_ref), gc_ref[...], n_c)
        kc_ref[...] = kc
        vc_ref[...] = vc

    kc, vc = kc_ref[...], vc_ref[...]
    n16 = kc.shape[0]
    pos = qi * tq + lax.broadcasted_iota(jnp.int32, (tq, n16), 0)
    ci = lax.broadcasted_iota(jnp.int32, (tq, n16), 1)
    valid = (ci * CMP_STRIDE + CMP_LEN - 1 <= pos) & (ci < n_c)
    q = q_ref[...]
    pcs = jnp.zeros((tq, n16), F32)
    outs = []
    for h in range(NSA_HEADS):
        pc, o = _cmp_attend(q[:, h * HEAD_DIM:(h + 1) * HEAD_DIM] * SCALE, kc, vc, valid)
        pcs = pcs + pc
        outs.append(o)
    oc_ref[...] = jnp.concatenate(outs, axis=1)
    imp = _dot_exact_rhs(pcs, cover_ref[...])
    j = lax.broadcasted_iota(jnp.int32, (tq, LANES), 1)
    cur = (qi * tq + lax.broadcasted_iota(jnp.int32, (tq, LANES), 0)) // SEL_BLOCK
    forced = (j == 0) | (j == cur) | (j == cur - 1)
    score = jnp.where(j > cur, NEG, imp + jnp.where(forced, SEL_FORCE, 0.0))
    mask_ref[...] = _topk_mask(score, N_SEL, (n_c + 1) * CMP_STRIDE // SEL_BLOCK)


def nsa_cmp_prompt(qn, pparts, lw, cover, tq=256):
    b, l, _ = qn.shape
    n16 = l // CMP_STRIDE
    n_c = n16 - 1
    return pl.pallas_call(
        partial(_cmp_kernel, tq=tq, n_c=n_c),
        grid=(b, l // tq),
        in_specs=[pl.BlockSpec((None, tq, 256), lambda bi, qi: (bi, qi, 0)),
                  pl.BlockSpec((None, n16, 256), lambda bi, qi: (bi, 0, 0)),
                  pl.BlockSpec((SUBLANES, CMP_LEN * HEAD_DIM), lambda bi, qi: (0, 0)),
                  pl.BlockSpec((CMP_LEN * HEAD_DIM, 128), lambda bi, qi: (0, 0)),
                  pl.BlockSpec((1, HEAD_DIM), lambda bi, qi: (0, 0)),
                  pl.BlockSpec((n16, LANES), lambda bi, qi: (0, 0))],
        out_specs=[pl.BlockSpec((None, tq, 256), lambda bi, qi: (bi, qi, 0)),
                   pl.BlockSpec((None, tq, LANES), lambda bi, qi: (bi, qi, 0))],
        out_shape=[jax.ShapeDtypeStruct((b, l, 256), F32), jax.ShapeDtypeStruct((b, l, LANES), F32)],
        scratch_shapes=[pltpu.VMEM((n16, HEAD_DIM), F32), pltpu.VMEM((n16, HEAD_DIM), F32)],
        compiler_params=_cparams(("arbitrary", "arbitrary")),
        name="nsa_cmp",
    )(qn, pparts, lw['pe8'], lw['wkv'], lw['gc'], cover)


def _online_update(s, v, m_ref, l_ref, acc_ref, h):
    m_old = m_ref[h]
    m_new = jnp.maximum(m_old, jnp.max(s, axis=1, keepdims=True))
    alpha = jnp.exp(m_old - m_new)
    p = jnp.exp(s - jnp.concatenate([m_new] * (s.shape[1] // LANES), axis=1))
    l_ref[h] = alpha * l_ref[h] + jnp.sum(p, axis=1, keepdims=True)
    acc_ref[h] = alpha[:, 0:HEAD_DIM] * acc_ref[h] + _dot(p.astype(BF16), v)
    m_ref[h] = m_new


def _flash_init(m_refs, l_refs, acc_refs):
    for r in m_refs:
        r[...] = jnp.full(r.shape, NEG, F32)
    for r in l_refs + acc_refs:
        r[...] = jnp.zeros(r.shape, F32)


def _nsa_flash_kernel(q_ref, rows_ref, win_ref, mask_ref, e_ref, ng_ref, oc_ref, o_ref,
                      qs_ref, ms_ref, ls_ref, as_ref, mw_ref, lw_ref, aw_ref, *, tq, tk):
    qi, kj = pl.program_id(1), pl.program_id(2)
    nk = pl.num_programs(2)

    @pl.when(kj == 0)
    def _():
        _flash_init([ms_ref, mw_ref], [ls_ref, lw_ref], [as_ref, aw_ref])
        q = q_ref[...]
        for h in range(NSA_HEADS):
            qs_ref[h] = (q[:, h * HEAD_DIM:(h + 1) * HEAD_DIM] * SCALE).astype(BF16)

    q_lo = qi * tq
    k_lo = kj * tk
    qpos = q_lo + lax.broadcasted_iota(jnp.int32, (tq, tk), 0)
    kpos = k_lo + lax.broadcasted_iota(jnp.int32, (tq, tk), 1)
    causal = kpos <= qpos

    @pl.when(k_lo <= q_lo + tq - 1)
    def _():
        rows = rows_ref[...]
        ks = rows[:, 2 * HEAD_DIM:3 * HEAD_DIM]
        vs = rows[:, 3 * HEAD_DIM:4 * HEAD_DIM]
        sel = _dot(mask_ref[...].astype(BF16), e_ref[...]) > 0.5
        bias = jnp.where(sel & causal, 0.0, NEG)
        for h in range(NSA_HEADS):
            _online_update(_dot_nt(qs_ref[h], ks) + bias, vs, ms_ref, ls_ref, as_ref, h)

    @pl.when((k_lo <= q_lo + tq - 1) & (k_lo + tk - 1 > q_lo - WINDOW))
    def _():
        win = win_ref[...]
        kw = win[:, 0:HEAD_DIM]
        vw = win[:, HEAD_DIM:2 * HEAD_DIM]
        bias = jnp.where(causal & (qpos - kpos < WINDOW), 0.0, NEG)
        for h in range(NSA_HEADS):
            _online_update(_dot_nt(qs_ref[h], kw) + bias, vw, mw_ref, lw_ref, aw_ref, h)

    @pl.when(kj == nk - 1)
    def _():
        gs = _sigmoid(ng_ref[...])
        oc = oc_ref[...]
        outs = []
        for h in range(NSA_HEADS):
            o_s = as_ref[h] / ls_ref[h][:, 0:HEAD_DIM]
            o_w = aw_ref[h] / lw_ref[h][:, 0:HEAD_DIM]
            outs.append(gs[:, 3 * h:3 * h + 1] * oc[:, h * HEAD_DIM:(h + 1) * HEAD_DIM]
                        + gs[:, 3 * h + 1:3 * h + 2] * o_s + gs[:, 3 * h + 2:3 * h + 3] * o_w)
        o_ref[...] = jnp.concatenate(outs, axis=1)


def nsa_flash_prompt(qr, rows, win, mask, expand, ng, oc, tq=512, tk=512):
    b, l, _ = qr.shape
    tq, tk = min(tq, l), min(tk, l)
    nq, nk = l // tq, l // tk

    def kv_idx(bi, qi, kj):
        return (bi, jnp.minimum(kj, (qi * tq + tq - 1) // tk), 0)

    def win_idx(bi, qi, kj):
        hi = (qi * tq + tq - 1) // tk
        lo = jnp.maximum((qi * tq - WINDOW + 1) // tk, 0)
        return (bi, jnp.clip(kj, lo, hi), 0)

    qmap = lambda bi, qi, kj: (bi, qi, 0)
    hshape = (NSA_HEADS, tq, LANES)
    return pl.pallas_call(
        partial(_nsa_flash_kernel, tq=tq, tk=tk),
        grid=(b, nq, nk),
        in_specs=[pl.BlockSpec((None, tq, 256), qmap),
                  pl.BlockSpec((None, tk, 256), kv_idx),
                  pl.BlockSpec((None, tk, 128), win_idx),
                  pl.BlockSpec((None, tq, LANES), qmap),
                  pl.BlockSpec((LANES, tk), lambda bi, qi, kj: (0, jnp.minimum(kj, (qi * tq + tq - 1) // tk))),
                  pl.BlockSpec((None, tq, LANES), qmap),
                  pl.BlockSpec((None, tq, 256), qmap)],
        out_specs=pl.BlockSpec((None, tq, 256), qmap),
        out_shape=jax.ShapeDtypeStruct((b, l, 256), F32),
        scratch_shapes=[pltpu.VMEM((NSA_HEADS, tq, HEAD_DIM), BF16),
                        pltpu.VMEM(hshape, F32), pltpu.VMEM(hshape, F32), pltpu.VMEM((NSA_HEADS, tq, HEAD_DIM), F32),
                        pltpu.VMEM(hshape, F32), pltpu.VMEM(hshape, F32), pltpu.VMEM((NSA_HEADS, tq, HEAD_DIM), F32)],
        compiler_params=_cparams(("arbitrary", "arbitrary", "arbitrary")),
        name="nsa_flash",
    )(qr, rows, win, mask, expand, ng, oc)


def _cumsum_kernel(x_ref, tri_ref, o_ref, carry_ref):
    @pl.when(pl.program_id(1) == 0)
    def _():
        carry_ref[...] = jnp.zeros(carry_ref.shape, F32)

    c = _dot_exact_lhs(tri_ref[...], x_ref[...]) + carry_ref[0:1, :]
    o_ref[...] = c
    carry_ref[0:1, :] = c[c.shape[0] - 1:c.shape[0], :]


def cumsum_time(x, tri):
    b, l, w = x.shape
    tc = tri.shape[0]
    return pl.pallas_call(
        _cumsum_kernel,
        grid=(b, l // tc),
        in_specs=[pl.BlockSpec((None, tc, w), lambda bi, i: (bi, i, 0)),
                  pl.BlockSpec((tc, tc), lambda bi, i: (0, 0))],
        out_specs=pl.BlockSpec((None, tc, w), lambda bi, i: (bi, i, 0)),
        out_shape=jax.ShapeDtypeStruct((b, l, w), F32),
        scratch_shapes=[pltpu.VMEM((SUBLANES, w), F32)],
        compiler_params=_cparams(("arbitrary", "arbitrary")),
        name="cumsum_time",
    )(x, tri)


def _fox_flash_kernel(q_ref, kv_ref, cq_ref, ck_ref, o_ref, qs_ref, cqb_ref, m_ref, l_ref, acc_ref, *, tq, tk):
    qi, kj = pl.program_id(1), pl.program_id(2)
    nk = pl.num_programs(2)

    @pl.when(kj == 0)
    def _():
        _flash_init([m_ref], [l_ref], [acc_ref])
        q = q_ref[...]
        cq = cq_ref[...]
        for h in range(FOX_HEADS):
            qs_ref[h] = (q[:, h * HEAD_DIM:(h + 1) * HEAD_DIM] * SCALE).astype(BF16)
            cqb_ref[h] = jnp.broadcast_to(cq[:, h:h + 1], (tq, LANES))

    q_lo = qi * tq
    k_lo = kj * tk

    def step(on_diagonal):
        kv = kv_ref[...]
        ck = ck_ref[...]
        if on_diagonal:
            qpos = q_lo + lax.broadcasted_iota(jnp.int32, (tq, tk), 0)
            kpos = k_lo + lax.broadcasted_iota(jnp.int32, (tq, tk), 1)
            causal = jnp.where(kpos <= qpos, 0.0, NEG)
        for h in range(FOX_HEADS):
            k = kv[:, h * HEAD_DIM:(h + 1) * HEAD_DIM]
            v = kv[:, 256 + h * HEAD_DIM:256 + (h + 1) * HEAD_DIM]
            bias = jnp.concatenate([cqb_ref[h]] * (tk // LANES), axis=1) - ck[h:h + 1, :]
            if on_diagonal:
                bias = bias + causal
            _online_update(_dot_nt(qs_ref[h], k) + bias, v, m_ref, l_ref, acc_ref, h)

    @pl.when(k_lo + tk - 1 <= q_lo)
    def _():
        step(False)

    @pl.when((k_lo <= q_lo + tq - 1) & (k_lo + tk - 1 > q_lo))
    def _():
        step(True)

    @pl.when(kj == nk - 1)
    def _():
        o_ref[...] = jnp.concatenate(
            [acc_ref[h] / l_ref[h][:, 0:HEAD_DIM] for h in range(FOX_HEADS)], axis=1)


def fox_flash_prompt(fq, fox_rows, c, c_t, tq=512, tk=512):
    b, l, _ = fq.shape
    tq, tk = min(tq, l), min(tk, l)
    nq, nk = l // tq, l // tk
    qmap = lambda bi, qi, kj: (bi, qi, 0)
    last = lambda qi: (qi * tq + tq - 1) // tk
    hshape = (FOX_HEADS, tq, LANES)
    return pl.pallas_call(
        partial(_fox_flash_kernel, tq=tq, tk=tk),
        grid=(b, nq, nk),
        in_specs=[pl.BlockSpec((None, tq, 256), qmap),
                  pl.BlockSpec((None, tk, 512), lambda bi, qi, kj: (bi, jnp.minimum(kj, last(qi)), 0)),
                  pl.BlockSpec((None, tq, LANES), qmap),
                  pl.BlockSpec((None, SUBLANES, tk), lambda bi, qi, kj: (bi, 0, jnp.minimum(kj, last(qi))))],
        out_specs=pl.BlockSpec((None, tq, 256), qmap),
        out_shape=jax.ShapeDtypeStruct((b, l, 256), F32),
        scratch_shapes=[pltpu.VMEM((FOX_HEADS, tq, HEAD_DIM), BF16), pltpu.VMEM(hshape, F32),
                        pltpu.VMEM(hshape, F32), pltpu.VMEM(hshape, F32), pltpu.VMEM((FOX_HEADS, tq, HEAD_DIM), F32)],
        compiler_params=_cparams(("arbitrary", "arbitrary", "arbitrary")),
        name="fox_flash",
    )(fq, fox_rows, c, c_t)


HG_LEVELS = int(math.log2(HG_CHUNK))


def _hgrn_constants():
    c = HG_CHUNK
    t = np.arange(c)[:, None]
    u = np.arange(c)[None, :]
    mats = [(u <= t), (u > t)]
    masks = []
    for lv in range(HG_LEVELS):
        bit = (t >> lv) & 1
        p_t = (t >> lv) << lv
        mats.append((bit == 1) & (u >= p_t) & (u <= t))
    for lv in range(HG_LEVELS):
        bit = (t >> lv) & 1
        p_s = ((t >> lv) + 1) << lv
        mats.append((bit == 0) & (u > t) & (u <= p_s - 1))
    for lv in range(HG_LEVELS):
        masks.append(((t >> (lv + 1)) == (u >> (lv + 1))) & (((t >> lv) & 1) == 1) & (((u >> lv) & 1) == 0))
    masks.append(t == u)
    cst = np.concatenate([m.astype(np.float32) for m in mats], axis=0)
    msk = np.stack([m.astype(np.float32) for m in masks], axis=0)
    return cst, msk


def _hgrn_kernel(hg_ref, lb_ref, cst_ref, msk_ref, eye_ref, bd_ref, go_ref, o_ref, st_ref, s_ref, *, tt):
    i = pl.program_id(1)
    c = HG_CHUNK

    @pl.when(i == 0)
    def _():
        s_ref[...] = jnp.zeros(s_ref.shape, F32)

    lb = lb_ref[...]
    nbb = hg_ref.shape[0]
    for ch, bb in [(ch, bb) for ch in range(tt // c) for bb in range(nbb)]:
        r = ch * c
        hq = hg_ref[bb, r:r + c, 0:256]
        hf = hg_ref[bb, r:r + c, 256:512]
        hv = hg_ref[bb, r:r + c, 512:768]
        hgate = hg_ref[bb, r:r + c, 768:1024]
        f = lb + (1.0 - lb) * _sigmoid(hf)
        lf = jnp.log(f)
        kk = 1.0 - f
        e = _dot_exact_lhs(cst_ref[...], lf)
        eb = jnp.exp(e[0:c])
        esu = jnp.exp(e[c:2 * c])
        ebl = eb[c - 1:c, :]
        qf = [(hq * jnp.exp(e[(2 + lv) * c:(3 + lv) * c])).astype(BF16) for lv in range(HG_LEVELS)]
        kg = [(kk * jnp.exp(e[(2 + HG_LEVELS + lv) * c:(3 + HG_LEVELS + lv) * c])).astype(BF16)
              for lv in range(HG_LEVELS)]
        qb, kb, vb = hq.astype(BF16), kk.astype(BF16), hv.astype(BF16)
        qe, ke = (hq * eb).astype(BF16), (kk * esu).astype(BF16)
        outs = []
        for h in range(HG_HEADS):
            sl = slice(h * HG_DK, (h + 1) * HG_DK)
            att = msk_ref[HG_LEVELS] * _dot_nt(qb[:, sl], kb[:, sl])
            for lv in range(HG_LEVELS):
                att = att + msk_ref[lv] * _dot_nt(qf[lv][:, sl], kg[lv][:, sl])
            st = s_ref[bb, h]
            outs.append(_dot(att.astype(BF16), vb[:, sl]) + _dot_nt(qe[:, sl], st.astype(BF16)))
            s_ref[bb, h] = st * ebl[:, sl] + _dot_tn(vb[:, sl], ke[:, sl])
        o = jnp.concatenate(outs, axis=1)
        o_ref[bb, r:r + c, :] = _seg_rms(o, bd_ref[...], go_ref[...]) * _silu(hgate)

    @pl.when(i == pl.num_programs(1) - 1)
    def _():
        for bb in range(nbb):
            for h in range(HG_HEADS):
                hi, mid, lo = _split3(s_ref[bb, h])
                eye = eye_ref[...]
                st_ref[bb, h] = _dot_nt(eye, hi) + _dot_nt(eye, mid) + _dot_nt(eye, lo)


def hgrn_prompt(hg, lw, cn, tt=256, nbb=2):
    b, l, _ = hg.shape
    const2 = lambda bi, i: (0, 0)
    return pl.pallas_call(
        partial(_hgrn_kernel, tt=tt),
        grid=(b // nbb, l // tt),
        in_specs=[pl.BlockSpec((nbb, tt, 1024), lambda bi, i: (bi, i, 0)),
                  pl.BlockSpec((1, 256), const2),
                  pl.BlockSpec(cn['hg_cst'].shape, const2),
                  pl.BlockSpec(cn['hg_msk'].shape, lambda bi, i: (0, 0, 0)),
                  pl.BlockSpec((HG_DK, HG_DK), const2),
                  pl.BlockSpec((256, 256), const2),
                  pl.BlockSpec((1, 256), const2)],
        out_specs=[pl.BlockSpec((nbb, tt, 256), lambda bi, i: (bi, i, 0)),
                   pl.BlockSpec((nbb, HG_HEADS, HG_DK, HG_DV), lambda bi, i: (bi, 0, 0, 0))],
        out_shape=[jax.ShapeDtypeStruct((b, l, 256), F32),
                   jax.ShapeDtypeStruct((b, HG_HEADS, HG_DK, HG_DV), F32)],
        scratch_shapes=[pltpu.VMEM((nbb, HG_HEADS, HG_DV, HG_DK), F32)],
        compiler_params=_cparams(("arbitrary", "arbitrary")),
        name="hgrn",
    )(hg, lw['lb'], cn['hg_cst'], cn['hg_msk'], cn['eye64'], cn['bd256'], lw['go'])


S5_BATCH = 4


def _gelu_tanh(x):
    return 0.5 * x * (1.0 + jnp.tanh(math.sqrt(2.0 / math.pi) * (x + 0.044715 * (x * x * x))))


def _s5_kernel(u_ref, bre_ref, bim_ref, ar_ref, ai_ref, cre_ref, cim_ref, d_ref, gw_ref, gb_ref,
               o_ref, hout_ref, xs_ref, h_ref, *, tc):
    i = pl.program_id(0)

    @pl.when(i == 0)
    def _():
        h_ref[...] = jnp.zeros(h_ref.shape, F32)

    nlt = S5_LANES // LANES
    for b in range(S5_BATCH):
        ub = u_ref[b].astype(BF16)
        xre, xim = _dot(ub, bre_ref[...]), _dot(ub, bim_ref[...])
        for j in range(nlt):
            xs_ref[j, pl.ds(b, tc, stride=SUBLANES), :] = xre[:, j * LANES:(j + 1) * LANES]
            xs_ref[j, pl.ds(S5_BATCH + b, tc, stride=SUBLANES), :] = xim[:, j * LANES:(j + 1) * LANES]
    ar = [ar_ref[:, j * LANES:(j + 1) * LANES] for j in range(nlt)]
    ai = [ai_ref[:, j * LANES:(j + 1) * LANES] for j in range(nlt)]

    def body(t, hcur):
        off = pl.multiple_of(t * SUBLANES, SUBLANES)
        hnew = []
        for j in range(nlt):
            hj = ar[j] * hcur[j] + ai[j] * pltpu.roll(hcur[j], S5_BATCH, 0) + xs_ref[j, pl.ds(off, SUBLANES), :]
            xs_ref[j, pl.ds(off, SUBLANES), :] = hj
            hnew.append(hj)
        return tuple(hnew)

    h0 = tuple(h_ref[:, j * LANES:(j + 1) * LANES] for j in range(nlt))
    hfin = lax.fori_loop(0, tc, body, h0, unroll=8)
    for j in range(nlt):
        h_ref[:, j * LANES:(j + 1) * LANES] = hfin[j]
    for b in range(S5_BATCH):
        hre = jnp.concatenate([xs_ref[j, pl.ds(b, tc, stride=SUBLANES), :] for j in range(nlt)], axis=1)
        him = jnp.concatenate([xs_ref[j, pl.ds(S5_BATCH + b, tc, stride=SUBLANES), :] for j in range(nlt)], axis=1)
        y = _dot(hre.astype(BF16), cre_ref[...]) - _dot(him.astype(BF16), cim_ref[...]) + d_ref[...] * u_ref[b]
        z = _gelu_tanh(y)
        o_ref[b] = z * _sigmoid(_dot(z.astype(BF16), gw_ref[...]) + gb_ref[...])

    @pl.when(i == pl.num_programs(0) - 1)
    def _():
        hout_ref[...] = h_ref[...]


def s5_prompt(u, lw, tc=256):
    b, l, _ = u.shape
    assert b == S5_BATCH
    const = lambda i: (0, 0)
    return pl.pallas_call(
        partial(_s5_kernel, tc=tc),
        grid=(l // tc,),
        in_specs=[pl.BlockSpec((b, tc, 256), lambda i: (0, i, 0)),
                  pl.BlockSpec((256, S5_LANES), const), pl.BlockSpec((256, S5_LANES), const),
                  pl.BlockSpec((SUBLANES, S5_LANES), const), pl.BlockSpec((SUBLANES, S5_LANES), const),
                  pl.BlockSpec((S5_LANES, 256), const), pl.BlockSpec((S5_LANES, 256), const),
                  pl.BlockSpec((1, 256), const), pl.BlockSpec((256, 256), const), pl.BlockSpec((1, 256), const)],
        out_specs=[pl.BlockSpec((b, tc, 256), lambda i: (0, i, 0)),
                   pl.BlockSpec((SUBLANES, S5_LANES), const)],
        out_shape=[jax.ShapeDtypeStruct((b, l, 256), F32), jax.ShapeDtypeStruct((SUBLANES, S5_LANES), F32)],
        scratch_shapes=[pltpu.VMEM((S5_LANES // LANES, tc * SUBLANES, LANES), F32),
                        pltpu.VMEM((SUBLANES, S5_LANES), F32)],
        compiler_params=_cparams(("arbitrary",)),
        name="s5_scan",
    )(u, lw['s5_bre'], lw['s5_bim'], lw['s5_ar8'], lw['s5_ai8'], lw['s5_cre'], lw['s5_cim'],
      lw['s5_d'], lw['s5_gw'], lw['s5_gb'])


def _merge_kernel(x_ref, h_ref, b0_ref, b1_ref, b2_ref, b3_ref, wg_ref, wb_ref, wo_ref, o_ref):
    hb = h_ref[...]
    merged = None
    for n, br in enumerate((b0_ref, b1_ref, b2_ref, b3_ref)):
        g = _sigmoid(_dot(hb, wg_ref[:, n * D_MODEL:(n + 1) * D_MODEL]))
        term = g * _dot(br[...].astype(BF16), wb_ref[n])
        merged = term if merged is None else merged + term
    o_ref[...] = x_ref[...] + _dot(merged.astype(BF16), wo_ref[...])


def merge(x, hb, branches, lw, tm):
    t = x.shape[0]
    row = lambda i: (i, 0)
    const = lambda i: (0, 0)
    return pl.pallas_call(
        _merge_kernel,
        grid=(t // tm,),
        in_specs=[pl.BlockSpec((tm, D_MODEL), row), pl.BlockSpec((tm, D_MODEL), row)]
                 + [pl.BlockSpec((tm, MIX_W), row)] * N_BRANCH
                 + [pl.BlockSpec((D_MODEL, N_BRANCH * D_MODEL), const),
                    pl.BlockSpec((N_BRANCH, MIX_W, D_MODEL), lambda i: (0, 0, 0)),
                    pl.BlockSpec((D_MODEL, D_MODEL), const)],
        out_specs=pl.BlockSpec((tm, D_MODEL), row),
        out_shape=jax.ShapeDtypeStruct((t, D_MODEL), F32),
        compiler_params=_cparams(("arbitrary",)),
        name="merge",
    )(x, hb, *branches, lw['wg'], lw['wb'], lw['wo'])


def _ffn_core(x, g, wu, cw, cb, wd, a_m2, a_m1):
    h = x * lax.rsqrt(jnp.mean(x * x, axis=-1, keepdims=True) + EPS) * g
    up = _dot(h.astype(BF16), wu)
    a, bb = up[:, 0:D_FF], up[:, D_FF:2 * D_FF]
    ac = cb + a_m2(a) * cw[0:1, :] + a_m1(a) * cw[1:2, :] + a * cw[2:3, :]
    return a, x + _dot((_silu(ac) * bb).astype(BF16), wd)


def _ffn_seq_kernel(x_ref, g_ref, wu_ref, cw_ref, cb_ref, wd_ref, o_ref, cs_ref, prev_ref, *, tm):
    @pl.when(pl.program_id(1) == 0)
    def _():
        prev_ref[...] = jnp.zeros(prev_ref.shape, F32)

    prev = prev_ref[...]
    row = lax.broadcasted_iota(jnp.int32, (tm, D_FF), 0)
    p1, p2 = prev[SUBLANES - 1:SUBLANES, :], prev[SUBLANES - 2:SUBLANES - 1, :]
    a_m1 = lambda a: jnp.where(row == 0, p1, pltpu.roll(a, 1, 0))
    a_m2 = lambda a: jnp.where(row == 0, p2, jnp.where(row == 1, p1, pltpu.roll(a, 2, 0)))
    a, o = _ffn_core(x_ref[...], g_ref[...], wu_ref[...], cw_ref[...], cb_ref[...], wd_ref[...], a_m2, a_m1)
    o_ref[...] = o
    tail = a[tm - SUBLANES:tm, :]
    prev_ref[...] = tail
    cs_ref[...] = tail[SUBLANES - (CONV_W - 1):SUBLANES, :]


def _ffn_step_kernel(x_ref, cs_ref, g_ref, wu_ref, cw_ref, cb_ref, wd_ref, o_ref, cso_ref):
    b0, b1 = cs_ref[:, 0, :], cs_ref[:, 1, :]
    a, o = _ffn_core(x_ref[...], g_ref[...], wu_ref[...], cw_ref[...], cb_ref[...], wd_ref[...],
                     lambda a: b0, lambda a: b1)
    o_ref[...] = o
    cso_ref[:, 0, :] = b1
    cso_ref[:, 1, :] = a


def _ffn_weight_specs(const):
    return [pl.BlockSpec((1, D_MODEL), const), pl.BlockSpec((D_MODEL, 2 * D_FF), const),
            pl.BlockSpec((CONV_W, D_FF), const), pl.BlockSpec((1, D_FF), const), pl.BlockSpec((D_FF, D_MODEL), const)]


def ffn_prompt(x, lw, b, l, tm=256):
    nt = l // tm
    const = lambda bi, i: (0, 0)
    row = lambda bi, i: (bi * nt + i, 0)
    return pl.pallas_call(
        partial(_ffn_seq_kernel, tm=tm),
        grid=(b, nt),
        in_specs=[pl.BlockSpec((tm, D_MODEL), row)] + _ffn_weight_specs(const),
        out_specs=[pl.BlockSpec((tm, D_MODEL), row),
                   pl.BlockSpec((None, CONV_W - 1, D_FF), lambda bi, i: (bi, 0, 0))],
        out_shape=[jax.ShapeDtypeStruct((b * l, D_MODEL), F32), jax.ShapeDtypeStruct((b, CONV_W - 1, D_FF), F32)],
        scratch_shapes=[pltpu.VMEM((SUBLANES, D_FF), F32)],
        compiler_params=_cparams(("arbitrary", "arbitrary")),
        name="ffn_seq",
    )(x, lw['ffn_g'], lw['ffn_wu'], lw['ffn_cw'], lw['ffn_cb'], lw['ffn_wd'])


def ffn_step(x, conv_state, l, lw):
    n = x.shape[0]
    const = lambda i: (0, 0)
    return pl.pallas_call(
        _ffn_step_kernel,
        grid=(1,),
        in_specs=[pl.BlockSpec((n, D_MODEL), const),
                  pl.BlockSpec((None, n, CONV_W - 1, D_FF), lambda i: (l, 0, 0, 0))] + _ffn_weight_specs(const),
        out_specs=[pl.BlockSpec((n, D_MODEL), const), pl.BlockSpec((n, CONV_W - 1, D_FF), lambda i: (0, 0, 0))],
        out_shape=[jax.ShapeDtypeStruct((n, D_MODEL), F32), jax.ShapeDtypeStruct((n, CONV_W - 1, D_FF), F32)],
        compiler_params=_cparams(("arbitrary",)),
        name="ffn_step",
    )(x, conv_state, lw['ffn_g'], lw['ffn_wu'], lw['ffn_cw'], lw['ffn_cb'], lw['ffn_wd'])


def _heads_to_rows(row):
    return jnp.concatenate([row[:, h * HEAD_DIM:(h + 1) * HEAD_DIM] for h in range(NSA_HEADS)]
                           + [jnp.zeros((SUBLANES - NSA_HEADS, HEAD_DIM), F32)], axis=0)


def _rows_to_heads(o8):
    return jnp.concatenate([o8[h:h + 1, :] for h in range(NSA_HEADS)], axis=1)


def _col8(cells):
    return jnp.concatenate(list(cells) + [jnp.zeros((SUBLANES - len(cells), 1), F32)], axis=0)


def _all_max(x):
    return jnp.max(jnp.max(x, axis=1, keepdims=True), axis=0, keepdims=True)


def _all_sum(x):
    return jnp.sum(jnp.sum(x, axis=1, keepdims=True), axis=0, keepdims=True)


def _paged_softmax(s, valid, s_new):
    m = jnp.maximum(_all_max(jnp.where(valid, s, NEG)), s_new)
    e = jnp.where(valid, jnp.exp(s - m), 0.0)
    e_new = jnp.exp(s_new - m)
    inv = 1.0 / (_all_sum(e) + e_new)
    return e * inv, e_new * inv


SEQ_PER_STEP = 2


def _nsa_step_kernel(pt_ref, *refs, n_pages, wb, past_len):
    ns = SEQ_PER_STEP
    o = ns * n_pages
    winc_ref = refs[o]
    shared = refs[o + 1:o + 15]
    o_ref, nwin_ref = refs[o + 15], refs[o + 16]
    scr = refs[o + 17:]
    for s in range(ns):
        _nsa_step_one(pl.program_id(0) * ns + s, refs[s * n_pages:(s + 1) * n_pages], winc_ref.at[s], *shared,
                      o_ref, nwin_ref.at[s], *scr[3 * s:3 * s + 3], n_pages=n_pages, wb=wb, past_len=past_len)


def _nsa_step_one(b, kv, winc_ref, qn_ref, qr_ref, rows_ref, win_ref, ng_ref, wcmp_ref, pe_ref, wkv_ref, gc_ref,
                  cover_ref, amask_ref, ex_ref, eye_ref, ones_ref, o_ref, nwin_ref, xc_ref, sl_ref, pl_ref,
                  *, n_pages, wb, past_len):
    row = lambda ref: ref[pl.ds(b, 1), :]
    rnew, wnew = row(rows_ref), row(win_ref)
    gs = _sigmoid(row(ng_ref))
    for j in range(n_pages):
        xc_ref[j * PAGE_SIZE:(j + 1) * PAGE_SIZE, :] = kv[j][0:2 * HEAD_DIM, :].T
    n16 = n_pages * PAGE_SIZE // CMP_STRIDE
    p = jnp.zeros((n16, 256), F32)
    for r in range(CMP_STRIDE):
        p = p + _dot(xc_ref[pl.ds(r, n16, stride=CMP_STRIDE), :].astype(BF16), wcmp_ref[r])
    n_c = n16 - 1
    kc, vc = _cmp_tokens(p, _pe_term(pe_ref, wkv_ref), gc_ref[...], n_c)
    ci = lax.broadcasted_iota(jnp.int32, (SUBLANES, n16), 1)
    ri = lax.broadcasted_iota(jnp.int32, (SUBLANES, n16), 0)
    pc, o_c = _cmp_attend(_heads_to_rows(row(qn_ref)) * SCALE, kc, vc, ci < n_c)
    pcs = jnp.sum(jnp.where(ri < NSA_HEADS, pc, 0.0), axis=0, keepdims=True)
    imp = _dot_exact_rhs(jnp.broadcast_to(pcs, (SUBLANES, n16)), cover_ref[...])
    j_l = lax.broadcasted_iota(jnp.int32, (SUBLANES, LANES), 1)
    cur = past_len // SEL_BLOCK
    forced = (j_l == 0) | (j_l == cur) | (j_l == cur - 1)
    score = jnp.where(j_l > cur, NEG, imp + jnp.where(forced, SEL_FORCE, 0.0))
    sel_b = _topk_rank(score[0:1, :], N_SEL, eye_ref[...], ones_ref[...])
    selx = _dot(amask_ref[...].astype(BF16), jnp.where(sel_b, ex_ref[...], 0.0).astype(BF16)) > 0.5
    qr8 = _heads_to_rows(row(qr_ref)) * SCALE
    qrb = qr8.astype(BF16)
    for j in range(n_pages):
        sl_ref[:, j, :] = _dot(qrb, kv[j][2 * HEAD_DIM:3 * HEAD_DIM, :].astype(BF16))
    s_new = jnp.sum(qr8 * rnew[:, 2 * HEAD_DIM:3 * HEAD_DIM], axis=1, keepdims=True)
    p_new = []
    for h in range(NSA_HEADS):
        ph, pn = _paged_softmax(sl_ref[h], selx, s_new[h:h + 1, :])
        pl_ref[h] = ph
        p_new.append(pn)
    pl_ref[NSA_HEADS:SUBLANES] = jnp.zeros((SUBLANES - NSA_HEADS, n_pages, PAGE_SIZE), F32)
    o_s = _col8(p_new) * rnew[:, 3 * HEAD_DIM:4 * HEAD_DIM]
    for j in range(n_pages):
        o_s = o_s + _dot_nt(pl_ref[:, j, :].astype(BF16), kv[j][3 * HEAD_DIM:4 * HEAD_DIM, :].astype(BF16))
    wc = winc_ref[...]
    s_w = _dot(qrb, wc[0:HEAD_DIM, :].astype(BF16))
    valid_w = lax.broadcasted_iota(jnp.int32, s_w.shape, 1) > wb - WINDOW
    s_wn = jnp.sum(qr8 * wnew[:, 0:HEAD_DIM], axis=1, keepdims=True)
    m_w = jnp.maximum(jnp.max(jnp.where(valid_w, s_w, NEG), axis=1, keepdims=True), s_wn)
    e_w = jnp.where(valid_w, jnp.exp(s_w - m_w), 0.0)
    e_wn = jnp.exp(s_wn - m_w)
    o_w = (_dot_nt(e_w.astype(BF16), wc[HEAD_DIM:2 * HEAD_DIM, :].astype(BF16)) + e_wn * wnew[:, HEAD_DIM:2 * HEAD_DIM]
           ) / (jnp.sum(e_w, axis=1, keepdims=True) + e_wn)
    gate = lambda k: _col8([gs[:, 3 * h + k:3 * h + k + 1] for h in range(NSA_HEADS)])
    o_ref[pl.ds(b, 1), :] = _rows_to_heads(gate(0) * o_c + gate(1) * o_s + gate(2) * o_w)
    wn8 = jnp.broadcast_to(wnew, (SUBLANES, 2 * HEAD_DIM))
    hi, mid, lo = _split3(wn8)
    eye = eye_ref[...]
    col = (_dot_nt(eye, hi) + _dot_nt(eye, mid) + _dot_nt(eye, lo))[:, 0:1]
    lane = lax.broadcasted_iota(jnp.int32, wc.shape, 1)
    nwin_ref[...] = jnp.where(lane == wb - 1, col, pltpu.roll(wc, wb - 1, 1))


def nsa_step(pt, kv_t, win_t, l, qn, qr, rows, win, ng, lw, cn):
    n = qn.shape[0]
    wb = win_t.shape[-1]
    n_pages = pt.shape[0] // n
    past_len = n_pages * PAGE_SIZE
    assert wb == WINDOW
    ns = SEQ_PER_STEP
    res = lambda w: pl.BlockSpec((n, w), lambda b, pt: (0, 0))
    c2 = lambda shape: pl.BlockSpec(shape, lambda b, pt: (0,) * len(shape))
    page = lambda s, j: pl.BlockSpec((None, None, 256, PAGE_SIZE), lambda b, pt: (l, pt[j * n + b * ns + s], 0, 0))
    seqs = range(ns)
    return pl.pallas_call(
        partial(_nsa_step_kernel, n_pages=n_pages, wb=wb, past_len=past_len),
        grid_spec=pltpu.PrefetchScalarGridSpec(
            num_scalar_prefetch=1,
            grid=(n // ns,),
            in_specs=[page(s, j) for s in seqs for j in range(n_pages)]
                     + [pl.BlockSpec((None, ns, 2 * HEAD_DIM, wb), lambda b, pt: (l, b, 0, 0)),
                        res(256), res(256), res(256), res(128), res(128),
                        c2((CMP_STRIDE, 128, 256)), c2((SUBLANES, CMP_LEN * HEAD_DIM)),
                        c2((CMP_LEN * HEAD_DIM, 128)), c2((1, HEAD_DIM)), c2((past_len // CMP_STRIDE, LANES)),
                        c2((n_pages, LANES)), c2((LANES, PAGE_SIZE)), c2((LANES, LANES)), c2((LANES, LANES))],
            out_specs=[pl.BlockSpec((n, 256), lambda b, pt: (0, 0)),
                       pl.BlockSpec((ns, 2 * HEAD_DIM, wb), lambda b, pt: (b, 0, 0))],
            scratch_shapes=[pltpu.VMEM((past_len, 2 * HEAD_DIM), F32),
                            pltpu.VMEM((SUBLANES, n_pages, PAGE_SIZE), F32),
                            pltpu.VMEM((SUBLANES, n_pages, PAGE_SIZE), F32)] * ns),
        out_shape=[jax.ShapeDtypeStruct((n, 256), F32), jax.ShapeDtypeStruct((n, 2 * HEAD_DIM, wb), F32)],
        compiler_params=_cparams(("arbitrary",)),
        name="nsa_step",
    )(pt, *([kv_t] * (ns * n_pages)), win_t, qn, qr, rows, win, ng, lw['wcmp'], lw['pe8'], lw['wkv'],
      lw['gc'], cn['cover_s'], cn['amask'], cn['ex'], cn['eye128'], cn['ones128'])


def _fox_step_kernel(pt_ref, *refs, n_pages):
    ns = SEQ_PER_STEP
    o = 2 * ns * n_pages
    shared = refs[o:o + 7]
    scr = refs[o + 7:]
    for s in range(ns):
        _fox_step_one(pl.program_id(0) * ns + s, refs[s * n_pages:(s + 1) * n_pages],
                      refs[(ns + s) * n_pages:(ns + s + 1) * n_pages], *shared, *scr[3 * s:3 * s + 3],
                      n_pages=n_pages)


def _fox_step_one(b, kv, lf, fq_ref, fox_ref, flog_ref, usuf_ref, ones_ref, lsu_ref, o_ref, lfs_ref, sl_ref, pl_ref,
                  *, n_pages):
    fq = fq_ref[pl.ds(b, 1), :]
    kvnew = fox_ref[pl.ds(b, 1), :]
    fl = flog_ref[pl.ds(b, 1), :]
    ri = lax.broadcasted_iota(jnp.int32, (SUBLANES, 256), 0)
    li = lax.broadcasted_iota(jnp.int32, (SUBLANES, 256), 1)
    bdm = (li // HEAD_DIM) == ri
    qbd = jnp.where(bdm, jnp.broadcast_to(fq * SCALE, (SUBLANES, 256)), 0.0)
    qb = qbd.astype(BF16)
    for j in range(n_pages):
        lfs_ref[:, j, :] = lf[j][...]
        sl_ref[:, j, :] = _dot(qb, kv[j][0:256, :].astype(BF16))
    s_new = jnp.sum(qbd * kvnew[:, 0:256], axis=1, keepdims=True)
    always = jnp.full((n_pages, PAGE_SIZE), True)
    p_new = []
    lf_all = lfs_ref[...].reshape(FOX_HEADS * n_pages, PAGE_SIZE)
    suf = _dot_exact_rhs(lf_all, usuf_ref[...])
    tail = _dot_exact_lhs(lsu_ref[...], _dot_exact_rhs(lf_all, ones_ref[...]))
    bias_all = suf + tail
    for h in range(FOX_HEADS):
        bias = bias_all[h * n_pages:(h + 1) * n_pages, :] + fl[:, h:h + 1]
        ph, pn = _paged_softmax(sl_ref[h] + bias, always, s_new[h:h + 1, :])
        pl_ref[h] = ph
        p_new.append(pn)
    pl_ref[FOX_HEADS:SUBLANES] = jnp.zeros((SUBLANES - FOX_HEADS, n_pages, PAGE_SIZE), F32)
    acc = _col8(p_new) * kvnew[:, 256:512]
    for j in range(n_pages):
        acc = acc + _dot_nt(pl_ref[:, j, :].astype(BF16), kv[j][256:512, :].astype(BF16))
    o_ref[pl.ds(b, 1), :] = jnp.sum(jnp.where(bdm, acc, 0.0), axis=0, keepdims=True)


def fox_step(pt, kv_t, lf_t, l, fq, foxr, flog, cn):
    n = fq.shape[0]
    n_pages = pt.shape[0] // n
    ns = SEQ_PER_STEP
    res = lambda w: pl.BlockSpec((n, w), lambda b, pt: (0, 0))
    c2 = lambda shape: pl.BlockSpec(shape, lambda b, pt: (0,) * len(shape))
    pidx = lambda s, j: (lambda b, pt: (l, pt[j * n + b * ns + s], 0, 0))
    page = lambda s, j: pl.BlockSpec((None, None, 512, PAGE_SIZE), pidx(s, j))
    lpage = lambda s, j: pl.BlockSpec((None, None, FOX_HEADS, PAGE_SIZE), pidx(s, j))
    scr = lambda r: pltpu.VMEM((r, n_pages, PAGE_SIZE), F32)
    every = [(s, j) for s in range(ns) for j in range(n_pages)]
    return pl.pallas_call(
        partial(_fox_step_kernel, n_pages=n_pages),
        grid_spec=pltpu.PrefetchScalarGridSpec(
            num_scalar_prefetch=1,
            grid=(n // ns,),
            in_specs=[page(s, j) for s, j in every] + [lpage(s, j) for s, j in every]
                     + [res(256), res(512), res(128), c2((PAGE_SIZE, PAGE_SIZE)), c2((PAGE_SIZE, PAGE_SIZE)),
                        c2((FOX_HEADS * n_pages, FOX_HEADS * n_pages))],
            out_specs=pl.BlockSpec((n, 256), lambda b, pt: (0, 0)),
            scratch_shapes=[scr(FOX_HEADS), scr(SUBLANES), scr(SUBLANES)] * ns),
        out_shape=jax.ShapeDtypeStruct((n, 256), F32),
        compiler_params=_cparams(("arbitrary",)),
        name="fox_step",
    )(pt, *([kv_t] * (ns * n_pages)), *([lf_t] * (ns * n_pages)), fq, foxr, flog, cn['usuf'], cn['ones128'],
      cn['lsu'])


def _hgrn_step_kernel(q_ref, f_ref, v_ref, g_ref, lb_ref, go_ref, s_ref, o_ref, so_ref, t_ref):
    off = pl.multiple_of((pl.program_id(0) % 2) * HG_DK, HG_DK)

    def head_t(ref):
        t_ref[...] = ref[...].T
        return t_ref[pl.ds(off, HG_DK), :]

    q, hf, v, hgate = head_t(q_ref), head_t(f_ref), head_t(v_ref), head_t(g_ref)
    lb = lb_ref[...]
    f = lb + (1.0 - lb) * _sigmoid(hf)
    sn = f[:, None, :] * s_ref[...] + (1.0 - f)[:, None, :] * v[None, :, :]
    so_ref[...] = sn
    o = jnp.sum(q[:, None, :] * sn, axis=0)
    o = o * lax.rsqrt(jnp.mean(o * o, axis=0, keepdims=True) + EPS) * go_ref[...]
    o_ref[...] = o * _silu(hgate)


def hgrn_step(hg, state_t, l, lw):
    n = hg.shape[0]
    col = lambda base: pl.BlockSpec((n, LANES), lambda h: (0, base + h // 2))
    return pl.pallas_call(
        _hgrn_step_kernel,
        grid=(HG_HEADS,),
        in_specs=[col(0), col(2), col(4), col(6),
                  pl.BlockSpec((None, HG_DK, 1), lambda h: (h, 0, 0)),
                  pl.BlockSpec((HG_DV, 1), lambda h: (0, 0)),
                  pl.BlockSpec((None, None, HG_DK, HG_DV, n), lambda h: (l, h, 0, 0, 0))],
        out_specs=[pl.BlockSpec((HG_DV, n), lambda h: (h, 0)),
                   pl.BlockSpec((None, HG_DK, HG_DV, n), lambda h: (h, 0, 0, 0))],
        out_shape=[jax.ShapeDtypeStruct((HG_HEADS * HG_DV, n), F32),
                   jax.ShapeDtypeStruct((HG_HEADS, HG_DK, HG_DV, n), F32)],
        scratch_shapes=[pltpu.VMEM((LANES, n), F32)],
        compiler_params=_cparams(("arbitrary",)),
        name="hgrn_step",
    )(hg, hg, hg, hg, lw['lb_t'], lw['go_t'], state_t)


def _t2(x):
    return jnp.concatenate([x[:, 0:LANES].T, x[:, LANES:2 * LANES].T], axis=0)


def _s5_step_kernel(u_ref, h_ref, bre_ref, bim_ref, ar_ref, ai_ref, cre_ref, cim_ref, d_ref, gw_ref, gb_ref,
                    o_ref, ho_ref):
    u = u_ref[...]
    ut = _t2(u).astype(BF16)
    h0r, h0i = h_ref[:, 0, :], h_ref[:, 1, :]
    ar, ai = ar_ref[...], ai_ref[...]
    hr = ar * h0r - ai * h0i + _dot(bre_ref[...], ut)
    hi = ar * h0i + ai * h0r + _dot(bim_ref[...], ut)
    ho_ref[:, 0, :] = hr
    ho_ref[:, 1, :] = hi
    yt = _dot(cre_ref[...], hr.astype(BF16)) - _dot(cim_ref[...], hi.astype(BF16))
    y = jnp.concatenate([yt[0:LANES, :].T, yt[LANES:2 * LANES, :].T], axis=1) + d_ref[...] * u
    z = _gelu_tanh(y)
    o_ref[...] = z * _sigmoid(_dot(z.astype(BF16), gw_ref[...]) + gb_ref[...])


def s5_step(u, state_t, l, lw):
    n = u.shape[0]
    c2 = lambda shape: pl.BlockSpec(shape, lambda i: (0,) * len(shape))
    return pl.pallas_call(
        _s5_step_kernel,
        grid=(1,),
        in_specs=[c2((n, MIX_W)), pl.BlockSpec((None, S5_LANES, 2, n), lambda i: (l, 0, 0, 0)),
                  c2((S5_LANES, MIX_W)), c2((S5_LANES, MIX_W)), c2((S5_LANES, 1)), c2((S5_LANES, 1)),
                  c2((MIX_W, S5_LANES)), c2((MIX_W, S5_LANES)), c2((1, MIX_W)), c2((MIX_W, MIX_W)), c2((1, MIX_W))],
        out_specs=[c2((n, MIX_W)), c2((S5_LANES, 2, n))],
        out_shape=[jax.ShapeDtypeStruct((n, MIX_W), F32), jax.ShapeDtypeStruct((S5_LANES, 2, n), F32)],
        compiler_params=_cparams(("arbitrary",)),
        name="s5_step",
    )(u, state_t, lw['s5_bre_t'], lw['s5_bim_t'], lw['s5_abr_c'], lw['s5_abi_c'], lw['s5_cre_t'], lw['s5_cim_t'],
      lw['s5_d'], lw['s5_gw'], lw['s5_gb'])


def _rope_tables(pos, reps):
    half = HEAD_DIM // 2
    inv = ROPE_THETA ** (-jnp.arange(half, dtype=F32) / half)
    ang = pos.astype(F32)[:, None] * inv[None, :]
    cos, sin = jnp.cos(ang), jnp.sin(ang)
    return (jnp.tile(jnp.concatenate([cos, cos], axis=1), (1, reps)),
            jnp.tile(jnp.concatenate([-sin, sin], axis=1), (1, reps)))


def make_consts(l_prompt, n_sample, past_len):
    cn = {}
    seg = np.arange(256) // HEAD_DIM
    cn['bd256'] = jnp.asarray(seg[:, None] == seg[None, :], BF16)
    cn['eye64'] = jnp.asarray(np.eye(HG_DK), BF16)
    cst, msk = _hgrn_constants()
    cn['hg_cst'] = jnp.asarray(cst, BF16)
    cn['hg_msk'] = jnp.asarray(msk, F32)
    tc = min(512, l_prompt)
    cn['tri'] = jnp.asarray(np.tril(np.ones((tc, tc))), BF16)
    n16 = l_prompt // CMP_STRIDE
    ci = np.arange(n16)[:, None]
    sj = np.arange(LANES)[None, :]
    cover = ((ci * CMP_STRIDE < sj * SEL_BLOCK + SEL_BLOCK) & (ci * CMP_STRIDE + CMP_LEN - 1 >= sj * SEL_BLOCK)
             & (ci < n16 - 1) & (sj < -(-l_prompt // SEL_BLOCK)))
    cn['cover'] = jnp.asarray(cover, BF16)
    cn['expand'] = jnp.asarray(np.arange(LANES)[:, None] == (np.arange(l_prompt)[None, :] // SEL_BLOCK), BF16)
    cn['cos_p'], cn['sin_p'] = _rope_tables(jnp.arange(l_prompt), 4)
    cn['cos_s'], cn['sin_s'] = _rope_tables(jnp.full((n_sample,), past_len), 4)
    n16s = past_len // CMP_STRIDE
    ci = np.arange(n16s)[:, None]
    cover_s = ((ci * CMP_STRIDE < sj * SEL_BLOCK + SEL_BLOCK) & (ci * CMP_STRIDE + CMP_LEN - 1 >= sj * SEL_BLOCK)
               & (ci < n16s - 1) & (sj < -(-(past_len + 1) // SEL_BLOCK)))
    cn['cover_s'] = jnp.asarray(cover_s, BF16)
    n_pages = past_len // PAGE_SIZE
    bpp = PAGE_SIZE // SEL_BLOCK
    cn['amask'] = jnp.asarray(np.arange(LANES)[None, :] // bpp == np.arange(n_pages)[:, None], F32)
    cn['ex'] = jnp.asarray(np.arange(LANES)[:, None] % bpp == np.arange(PAGE_SIZE)[None, :] // SEL_BLOCK, BF16)
    cn['eye128'] = jnp.asarray(np.eye(LANES), BF16)
    tp = np.arange(PAGE_SIZE)
    cn['usuf'] = jnp.asarray(tp[:, None] > tp[None, :], BF16)
    cn['ones128'] = jnp.ones((PAGE_SIZE, PAGE_SIZE), BF16)
    jp = np.arange(n_pages)
    cn['lsu'] = jnp.asarray(np.kron(np.eye(FOX_HEADS), jp[None, :] > jp[:, None]), BF16)
    return cn


def prep_layer(l, P, lb_all):
    lw = {}
    w = P['w_in'][l]
    pts = np.concatenate([[0], np.cumsum(IN_SIZES)])
    seg = lambda i: w[:, pts[i]:pts[i + 1]]
    z = lambda n: jnp.zeros((D_MODEL, n), w.dtype)
    lw['w_in'] = jnp.concatenate([seg(0), seg(1), seg(2), z(LANES - IN_SIZES[2])] + [seg(i) for i in range(3, 12)]
                                 + [z(LANES - IN_SIZES[11])], axis=1).astype(BF16)
    lw['wg'] = seg(12).astype(BF16)
    lw['attn_norm'] = P['attn_norm'][l].reshape(1, D_MODEL)
    lw['gq'] = jnp.tile(P['nsa_q_norm'][l], 4).reshape(1, 256)
    kn = P['nsa_k_norm'][l]
    lw['gk'] = jnp.concatenate([jnp.tile(kn[1], 4), jnp.tile(kn[2], 2)]).reshape(1, 384)
    lw['gc'] = kn[0].reshape(1, HEAD_DIM)
    lw['gfq'] = jnp.tile(P['fox_q_norm'][l], 4).reshape(1, 256)
    lw['gfk'] = jnp.tile(P['fox_k_norm'][l], 4).reshape(1, 256)
    lw['fbias'] = jnp.pad(P['fox_f_bias'][l], (0, LANES - FOX_HEADS)).reshape(1, LANES)
    wk = P['nsa_cmp_wk'][l].reshape(2, CMP_STRIDE, HEAD_DIM, HEAD_DIM)
    wv = P['nsa_cmp_wv'][l].reshape(2, CMP_STRIDE, HEAD_DIM, HEAD_DIM)
    we = jnp.zeros((CMP_STRIDE, 4, HEAD_DIM, 4, HEAD_DIM), F32)
    we = we.at[:, 0, :, 0, :].set(wk[0]).at[:, 1, :, 1, :].set(wv[0])
    we = we.at[:, 0, :, 2, :].set(wk[1]).at[:, 1, :, 3, :].set(wv[1])
    lw['we'] = we.reshape(CMP_STRIDE * 256, 256).astype(BF16)
    lw['wcmp'] = we.reshape(CMP_STRIDE, 256, 256)[:, 0:2 * HEAD_DIM, :].astype(BF16)
    pe = P['nsa_cmp_pe'][l].reshape(1, CMP_LEN * HEAD_DIM)
    pe_hi = pe.astype(BF16)
    pe_lo = (pe - pe_hi.astype(F32)).astype(BF16)
    lw['pe8'] = jnp.concatenate([pe_hi, pe_lo, jnp.zeros((SUBLANES - 2, CMP_LEN * HEAD_DIM), BF16)], axis=0)
    lw['wkv'] = jnp.concatenate([P['nsa_cmp_wk'][l], P['nsa_cmp_wv'][l]], axis=1).astype(BF16)
    lw['lb'] = lb_all[l].reshape(1, 256)
    lw['go'] = jnp.tile(P['hgrn_o_norm'][l], 4).reshape(1, 256)
    lw['lb_t'] = lb_all[l].reshape(HG_HEADS, HG_DK, 1)
    lw['go_t'] = P['hgrn_o_norm'][l].reshape(HG_DV, 1)
    dt = jnp.exp(P['s5_log_dt'][l].astype(F32))[:, None]
    ar, ai = P['s5_a_re'][l].astype(F32), P['s5_a_im'][l].astype(F32)
    mag = jnp.exp(dt * ar)
    abr, abi = mag * jnp.cos(dt * ai), mag * jnp.sin(dt * ai)
    den = ar * ar + ai * ai
    zr = ((abr - 1.0) * ar + abi * ai) / den
    zi = (abi * ar - (abr - 1.0) * ai) / den
    br_, bi_ = P['s5_b_re'][l].astype(F32), P['s5_b_im'][l].astype(F32)
    bbr = zr[..., None] * br_ - zi[..., None] * bi_
    bbi = zr[..., None] * bi_ + zi[..., None] * br_
    eye_g = jnp.eye(S5_GROUPS, dtype=F32)
    lw['s5_bre'] = jnp.einsum('gpc,gh->gchp', bbr, eye_g).reshape(MIX_W, S5_LANES).astype(BF16)
    lw['s5_bim'] = jnp.einsum('gpc,gh->gchp', bbi, eye_g).reshape(MIX_W, S5_LANES).astype(BF16)
    lw['s5_cre'] = jnp.einsum('gcp,gh->gphc', P['s5_c_re'][l].astype(F32), eye_g).reshape(S5_LANES, MIX_W).astype(BF16)
    lw['s5_cim'] = jnp.einsum('gcp,gh->gphc', P['s5_c_im'][l].astype(F32), eye_g).reshape(S5_LANES, MIX_W).astype(BF16)
    abr_f, abi_f = abr.reshape(1, S5_LANES), abi.reshape(1, S5_LANES)
    lw['s5_abr_c'], lw['s5_abi_c'] = abr.reshape(S5_LANES, 1), abi.reshape(S5_LANES, 1)
    for k in ('s5_bre', 's5_bim', 's5_cre', 's5_cim'):
        lw[k + '_t'] = lw[k].T
    lw['s5_ar8'] = jnp.tile(abr_f, (SUBLANES, 1))
    lw['s5_ai8'] = jnp.concatenate([jnp.tile(-abi_f, (S5_BATCH, 1)), jnp.tile(abi_f, (S5_BATCH, 1))], axis=0)
    lw['s5_d'] = P['s5_d'][l].reshape(1, MIX_W)
    lw['s5_gw'] = P['s5_glu_w'][l].astype(BF16)
    lw['s5_gb'] = P['s5_glu_b'][l].reshape(1, MIX_W)
    lw['wb'] = P['w_branch'][l].astype(BF16)
    lw['wo'] = P['w_out'][l].astype(BF16)
    lw['ffn_g'] = P['ffn_norm'][l].reshape(1, D_MODEL)
    lw['ffn_wu'] = P['ffn_w_up'][l].astype(BF16)
    lw['ffn_cw'] = P['ffn_conv_w'][l]
    lw['ffn_cb'] = P['ffn_conv_b'][l].reshape(1, D_FF)
    lw['ffn_wd'] = P['ffn_w_down'][l].astype(BF16)
    return lw


def prompt_layer(x, lw, cn, b, l):
    r3 = lambda a: a.reshape(b, l, a.shape[-1])
    (hb, qn, qr, rows, win, ng, hg, su, fq, foxr, flog, rows_b, win_b, fox_b, rows_t, win_t, fox_t, flog_t
     ) = in_proj(x, lw, cn['bd256'], cn['cos_p'], cn['sin_p'], 256, b)
    pparts = matmul(rows.reshape(b * l // CMP_STRIDE, CMP_STRIDE * 256), lw['we']).reshape(b, l // CMP_STRIDE, 256)
    oc, mask = nsa_cmp_prompt(r3(qn), pparts, lw, cn['cover'])
    o_nsa = nsa_flash_prompt(r3(qr), r3(rows_b), r3(win_b), mask, cn['expand'], r3(ng), oc)
    c = cumsum_time(r3(flog), cn['tri'])
    c_t = jnp.swapaxes(c[:, :, 0:SUBLANES], 1, 2)
    o_fox = fox_flash_prompt(r3(fq), r3(fox_b), c, c_t)
    o_h, hg_new = hgrn_prompt(r3(hg), lw, cn)
    o_s5, s5h = s5_prompt(r3(su), lw)
    flat = lambda a: a.reshape(b * l, a.shape[-1])
    x1 = merge(x, hb, (flat(o_nsa), flat(o_h), flat(o_s5), flat(o_fox)), lw, 256)
    x2, conv_new = ffn_prompt(x1, lw, b, l)
    nw = min(WINDOW, l)
    s5_new = jnp.moveaxis(s5h.reshape(2, S5_BATCH, S5_GROUPS, S5_STATE), 0, -1)
    return (x2,) + _kv_outputs(rows_t, win_t[:, :, l - nw:], fox_t, flog_t) + (hg_new, s5_new, conv_new)


def _kv_outputs(rows_t, win_t, fox_t, flog_t):
    b, _, l = rows_t.shape
    return (jnp.transpose(rows_t.reshape(b, 4, HEAD_DIM, l), (0, 3, 1, 2)),
            jnp.transpose(win_t.reshape(b, 2, HEAD_DIM, win_t.shape[-1]), (0, 3, 1, 2)),
            jnp.transpose(fox_t.reshape(b, 2, FOX_HEADS, HEAD_DIM, l), (0, 4, 1, 2, 3)),
            jnp.transpose(flog_t[:, 0:FOX_HEADS, :], (0, 2, 1)))


def rmsnorm(x, g):
    xf = x.astype(F32)
    y = xf * lax.rsqrt(jnp.mean(xf * xf, axis=-1, keepdims=True) + EPS)
    return (y * g.astype(F32)).astype(x.dtype)


def rope(x, pos):
    half = HEAD_DIM // 2
    inv = ROPE_THETA ** (-jnp.arange(half, dtype=F32) / half)
    ang = pos.astype(F32)[:, None] * inv[None, :]
    shp = (pos.shape[0],) + (1,) * (x.ndim - 3) + (half,)
    cos, sin = jnp.cos(ang).reshape(shp), jnp.sin(ang).reshape(shp)
    xf = x.astype(F32)
    x1, x2 = xf[..., :half], xf[..., half:]
    return jnp.concatenate([x1 * cos - x2 * sin, x2 * cos + x1 * sin], axis=-1).astype(x.dtype)


def masked_softmax(logits, mask):
    lg = jnp.where(mask, logits.astype(F32), -jnp.inf)
    m = jnp.max(lg, axis=-1, keepdims=True)
    m = jnp.where(jnp.isfinite(m), m, 0.0)
    e = jnp.where(mask, jnp.exp(lg - m), 0.0)
    return e / jnp.maximum(jnp.sum(e, axis=-1, keepdims=True), 1e-30)


def gather_pages(pool, page_table):
    g = pool[page_table]
    return g.reshape((g.shape[0], g.shape[1] * g.shape[2]) + g.shape[3:])


def nsa_compress(rows, pe, w):
    B, T, _ = rows.shape
    n_c = (T - CMP_LEN) // CMP_STRIDE + 1
    idx = jnp.arange(n_c)[:, None] * CMP_STRIDE + jnp.arange(CMP_LEN)[None, :]
    blocks = rows[:, idx] + pe.astype(rows.dtype)
    return blocks.reshape(B, n_c, CMP_LEN * HEAD_DIM) @ w


def nsa_step_jax(q, rows, win, gates, past_len, win_pos0, k_norm_c, pe, wk, wv):
    B, Lq, H, _ = q.shape
    T = rows.shape[1]
    Tw = win.shape[1]
    q_rot = rope(q, past_len + jnp.arange(Lq))
    k_c = rmsnorm(nsa_compress(rows[:, :, 0], pe, wk), k_norm_c)
    v_c = nsa_compress(rows[:, :, 1], pe, wv).astype(F32)
    n_c = k_c.shape[1]
    c_start = jnp.arange(n_c) * CMP_STRIDE
    c_end = c_start + CMP_LEN - 1
    n_s = -(-T // SEL_BLOCK)
    s_start = jnp.arange(n_s) * SEL_BLOCK
    cover = ((c_start[:, None] < s_start[None, :] + SEL_BLOCK) & (c_end[:, None] >= s_start[None, :])).astype(F32)
    sel_rows = jnp.pad(rows[:, :, 2:4], ((0, 0), (0, n_s * SEL_BLOCK - T), (0, 0), (0, 0)))
    sel_rows = sel_rows.reshape(B, n_s, SEL_BLOCK, 2, HEAD_DIM)
    k_top = min(N_SEL, n_s)
    qb = Lq
    kw_all = jnp.pad(win, ((0, 0), (WINDOW, 0), (0, 0), (0, 0)))
    buf_len = past_len - win_pos0
    bidx = jnp.arange(B)[:, None, None]
    pos = past_len + jnp.arange(qb)
    pc = masked_softmax(jnp.einsum('bqhd,bcd->bhqc', q, k_c) * SCALE, (c_end[None, :] <= pos[:, None])[None, None])
    o_c = jnp.einsum('bhqc,bcd->bqhd', pc, v_c)
    imp = jnp.einsum('bhqc,cs->bqs', pc, cover)
    cur = pos // SEL_BLOCK
    j = jnp.arange(n_s)[None, :]
    forced = (j == 0) | (j == cur[:, None]) | (j == cur[:, None] - 1)
    score = jnp.where(j > cur[:, None], -jnp.inf, imp + SEL_FORCE * forced)
    _, sel = lax.top_k(score, k_top)
    chosen = jnp.any(sel[..., None] == jnp.arange(n_s), axis=2)
    tpos = jnp.arange(T)
    ms = chosen[:, :, tpos // SEL_BLOCK] & (tpos[None, None, :] <= pos[None, :, None])
    ls = jnp.einsum('bqhd,btd->bhqt', q_rot, rows[:, :, 2]) * SCALE
    ps = masked_softmax(ls, ms[:, None])
    o_s = jnp.einsum('bhqt,btd->bqhd', ps, rows[:, :, 3].astype(F32))
    start = buf_len
    kwin = lax.dynamic_slice_in_dim(kw_all, start, WINDOW + qb, axis=1)
    wpos = win_pos0 - WINDOW + start + jnp.arange(WINDOW + qb)
    mw = ((wpos[None, :] >= win_pos0) & (wpos[None, :] < win_pos0 + Tw)
          & (wpos[None, :] <= pos[:, None]) & (pos[:, None] - wpos[None, :] < WINDOW))
    pw = masked_softmax(jnp.einsum('bqhd,bkd->bhqk', q_rot, kwin[:, :, 0]) * SCALE, mw[None, None])
    o_w = jnp.einsum('bhqk,bkd->bqhd', pw, kwin[:, :, 1].astype(F32))
    gs = jax.nn.sigmoid(gates.astype(F32))[..., None]
    o = gs[:, :, :, 0] * o_c + gs[:, :, :, 1] * o_s + gs[:, :, :, 2] * o_w
    return o.reshape(B, Lq, H * HEAD_DIM)


def fox_step_jax(q, k, v, logf, past_len):
    B, Lq, H, _ = q.shape
    T = k.shape[1]
    c = jnp.cumsum(logf.astype(F32), axis=1)
    cT = jnp.moveaxis(c, 2, 1)
    pos = past_len + jnp.arange(Lq)
    bias = jnp.moveaxis(c[:, past_len:], 2, 1)[..., None] - cT[:, :, None, :]
    lg = jnp.einsum('bqhd,bkhd->bhqk', q, k).astype(F32) * SCALE + bias
    p = masked_softmax(lg, (jnp.arange(T)[None, :] <= pos[:, None])[None, None])
    return jnp.einsum('bhqk,bkhd->bqhd', p, v.astype(F32))


def sample_layer(x, lw, cn, l, views, pt):
    n = x.shape[0]
    kv_t, win_t, fkv_t, flf_t, hg_t, s5_t, conv = views
    (hb, qn, qr, rows, win, ng, hg, su, fq, foxr, flog, _, _, _, rows_t, _, fox_t, flog_t
     ) = in_proj(x, lw, cn['bd256'], cn['cos_s'], cn['sin_s'], n, 1)
    o_nsa, nwin_t = nsa_step(pt, kv_t, win_t, l, qn, qr, rows, win, ng, lw, cn)
    o_fox = fox_step(pt, fkv_t, flf_t, l, fq, foxr, flog, cn)
    oh_t, hgn_t = hgrn_step(hg, hg_t, l, lw)
    o_s5, s5n_t = s5_step(su, s5_t, l, lw)
    x1 = merge(x, hb, (o_nsa, oh_t.T, o_s5, o_fox), lw, n)
    x2, conv_new = ffn_step(x1, conv, l, lw)
    wb = nwin_t.shape[-1]
    new_win = jnp.transpose(nwin_t.reshape(n, 2, HEAD_DIM, wb), (0, 3, 1, 2))
    hg_new = jnp.transpose(hgn_t, (3, 0, 1, 2))
    s5_new = jnp.transpose(s5n_t.reshape(S5_GROUPS, S5_STATE, 2, n), (3, 0, 1, 2))
    rows_o, _, fox_o, flog_o = _kv_outputs(rows_t, rows_t[:, 0:2 * HEAD_DIM], fox_t, flog_t)
    seq_first = lambda a: jnp.swapaxes(a, 0, 1)
    return (x2, seq_first(rows_o), new_win, seq_first(fox_o), seq_first(flog_o), hg_new, s5_new, conv_new)


def sample_views(caches, page_table):
    cache_nsa_kv, cache_nsa_win, cache_fox_kv, cache_fox_logf, state_hgrn, state_s5, state_conv = caches
    d, n_pool = cache_nsa_kv.shape[0:2]
    n = cache_nsa_win.shape[1]
    kv_t = jnp.transpose(cache_nsa_kv, (0, 1, 3, 4, 2)).reshape(d, n_pool, 4 * HEAD_DIM, PAGE_SIZE)
    win_t = jnp.transpose(cache_nsa_win, (0, 1, 3, 4, 2)).reshape(d, n, 2 * HEAD_DIM, cache_nsa_win.shape[2])
    fkv_t = jnp.transpose(cache_fox_kv, (0, 1, 3, 4, 5, 2)).reshape(d, n_pool, 2 * FOX_HEADS * HEAD_DIM, PAGE_SIZE)
    flf_t = jnp.transpose(cache_fox_logf, (0, 1, 3, 2))
    hg_t = jnp.transpose(state_hgrn, (0, 2, 3, 4, 1))
    s5_t = jnp.transpose(state_s5, (0, 2, 3, 4, 1)).reshape(d, S5_LANES, 2, n)
    pt = jnp.transpose(page_table).reshape(-1)
    return (kv_t, win_t, fkv_t, flf_t, hg_t, s5_t, state_conv), pt


def _old_sample_layer_jax(x, lw, cn, P, l, caches, page_table, lb):
    n = x.shape[0]
    cache_nsa_kv, cache_nsa_win, cache_fox_kv, cache_fox_logf, state_hgrn, state_s5, state_conv = caches
    hb, qn, qr, rows, win, ng, hg, su, fq, foxr, flog = in_proj(x, lw, cn['bd256'], cn['cos_s'], cn['sin_s'], n)
    nsa_past = gather_pages(cache_nsa_kv, page_table)
    past_len = nsa_past.shape[1]
    rows_full = jnp.concatenate([nsa_past, rows.reshape(n, 1, 4, HEAD_DIM)], axis=1)
    win_full = jnp.concatenate([cache_nsa_win, win.reshape(n, 1, 2, HEAD_DIM)], axis=1)
    o_nsa = nsa_step_jax(qn.reshape(n, 1, NSA_HEADS, HEAD_DIM), rows_full, win_full,
                         ng[:, 0:3 * NSA_HEADS].reshape(n, 1, NSA_HEADS, 3), past_len,
                         past_len - cache_nsa_win.shape[1], P['nsa_k_norm'][l][0], P['nsa_cmp_pe'][l],
                         P['nsa_cmp_wk'][l], P['nsa_cmp_wv'][l]).reshape(n, MIX_W)
    new_win = win_full[:, win_full.shape[1] - min(WINDOW, win_full.shape[1]):]
    hq, hf, hv, hgate = (hg[:, i * 256:(i + 1) * 256].reshape(n, HG_HEADS, HG_DK) for i in range(4))
    lbh = lb.reshape(HG_HEADS, HG_DK)
    f = lbh + (1.0 - lbh) * jax.nn.sigmoid(hf)
    hg_new = f[..., None] * state_hgrn + (1.0 - f)[..., None] * hv[:, :, None, :]
    o_h = jnp.einsum('bhk,bhkv->bhv', hq, hg_new)
    o_h = (rmsnorm(o_h, P['hgrn_o_norm'][l]) * jax.nn.silu(hgate)).reshape(n, MIX_W)
    abr, abi = lw['s5_abr'], lw['s5_abi']
    h0r, h0i = state_s5[..., 0].reshape(n, S5_LANES), state_s5[..., 1].reshape(n, S5_LANES)
    hr = abr * h0r - abi * h0i + su @ lw['s5_bre'].astype(F32)
    hi = abr * h0i + abi * h0r + su @ lw['s5_bim'].astype(F32)
    y5 = hr @ lw['s5_cre'].astype(F32) - hi @ lw['s5_cim'].astype(F32) + lw['s5_d'] * su
    z5 = jax.nn.gelu(y5)
    o_s5 = z5 * jax.nn.sigmoid(z5 @ lw['s5_gw'].astype(F32) + lw['s5_gb'])
    s5_new = jnp.stack([hr.reshape(n, S5_GROUPS, S5_STATE), hi.reshape(n, S5_GROUPS, S5_STATE)], axis=-1)
    fox_rows = foxr.reshape(n, 1, 2, FOX_HEADS, HEAD_DIM)
    kv_full = jnp.concatenate([gather_pages(cache_fox_kv, page_table), fox_rows], axis=1)
    flog4 = flog[:, 0:FOX_HEADS].reshape(n, 1, FOX_HEADS)
    lf_full = jnp.concatenate([gather_pages(cache_fox_logf, page_table), flog4], axis=1)
    o_fox = fox_step_jax(fq.reshape(n, 1, FOX_HEADS, HEAD_DIM), kv_full[:, :, 0], kv_full[:, :, 1], lf_full,
                         past_len).reshape(n, MIX_W)
    x1 = merge(x, hb, (o_nsa, o_h, o_s5, o_fox), lw, n)
    x2, a_new = ffn_step(x1, state_conv[:, 0], state_conv[:, 1], lw)
    conv_new = jnp.stack([state_conv[:, 1], a_new], axis=1)
    return (x2, rows.reshape(n, 1, 4, HEAD_DIM), new_win, fox_rows, flog4, hg_new, s5_new, conv_new)


PARAM_NAMES = ('attn_norm', 'w_in', 'nsa_q_norm', 'nsa_k_norm', 'nsa_cmp_pe', 'nsa_cmp_wk', 'nsa_cmp_wv',
               'hgrn_o_norm', 's5_a_re', 's5_a_im', 's5_log_dt', 's5_b_re', 's5_b_im', 's5_c_re', 's5_c_im',
               's5_d', 's5_glu_w', 's5_glu_b', 'fox_q_norm', 'fox_k_norm', 'fox_f_bias', 'w_branch', 'w_out',
               'ffn_norm', 'ffn_w_up', 'ffn_conv_w', 'ffn_conv_b', 'ffn_w_down')


def trunk(x_prompt, x_sample, caches, page_table, hgrn_lb_logits, P):
    depth = P['w_in'].shape[0]
    bp, lp, _ = x_prompt.shape
    ns = x_sample.shape[0]
    past_len = page_table.shape[1] * PAGE_SIZE
    sm = jax.nn.softmax(hgrn_lb_logits.astype(F32), axis=0)
    lb_all = jnp.cumsum(sm, axis=0) - sm[0]
    cn = make_consts(lp, ns, past_len)
    yp = x_prompt.reshape(bp * lp, D_MODEL)
    ys = x_sample.reshape(ns, D_MODEL)
    outs_p, outs_s = [], []
    views, pt = sample_views(caches, page_table)
    for l in range(depth):
        lw = prep_layer(l, P, lb_all)
        res_p = prompt_layer(yp, lw, cn, bp, lp)
        res_s = sample_layer(ys, lw, cn, l, views, pt)
        yp, ys = res_p[0], res_s[0]
        outs_p.append(res_p[1:])
        outs_s.append(res_s[1:])
    sp = [jnp.stack([o[i] for o in outs_p], axis=0) for i in range(7)]
    ss = [jnp.stack([o[i] for o in outs_s], axis=0) for i in range(7)]
    nsa_kv_p, nsa_win_p, fox_kv_p, fox_logf_p, hgrn_p, s5_p, conv_p = sp
    nsa_kv_s, nsa_win_s, fox_kv_s, fox_logf_s, hgrn_s, s5_s, conv_s = ss
    return (yp.reshape(bp, lp, D_MODEL), ys.reshape(ns, 1, D_MODEL), nsa_kv_p, nsa_kv_s, nsa_win_p, nsa_win_s,
            fox_kv_p, fox_kv_s, fox_logf_p, fox_logf_s, hgrn_p, hgrn_s, s5_p, s5_s, conv_p, conv_s)


def kernel(x_prompt, x_sample, cache_nsa_kv, cache_nsa_win, cache_fox_kv, cache_fox_logf, state_hgrn, state_s5, state_ffn_conv, page_table, attn_norm, w_in, nsa_q_norm, nsa_k_norm, nsa_cmp_pe, nsa_cmp_wk, nsa_cmp_wv, hgrn_lb_logits, hgrn_o_norm, s5_a_re, s5_a_im, s5_log_dt, s5_b_re, s5_b_im, s5_c_re, s5_c_im, s5_d, s5_glu_w, s5_glu_b, fox_q_norm, fox_k_norm, fox_f_bias, w_branch, w_out, ffn_norm, ffn_w_up, ffn_conv_w, ffn_conv_b, ffn_w_down):
    P = dict(attn_norm=attn_norm, w_in=w_in, nsa_q_norm=nsa_q_norm, nsa_k_norm=nsa_k_norm,
             nsa_cmp_pe=nsa_cmp_pe, nsa_cmp_wk=nsa_cmp_wk, nsa_cmp_wv=nsa_cmp_wv, hgrn_o_norm=hgrn_o_norm,
             s5_a_re=s5_a_re, s5_a_im=s5_a_im, s5_log_dt=s5_log_dt, s5_b_re=s5_b_re, s5_b_im=s5_b_im,
             s5_c_re=s5_c_re, s5_c_im=s5_c_im, s5_d=s5_d, s5_glu_w=s5_glu_w, s5_glu_b=s5_glu_b,
             fox_q_norm=fox_q_norm, fox_k_norm=fox_k_norm, fox_f_bias=fox_f_bias, w_branch=w_branch,
             w_out=w_out, ffn_norm=ffn_norm, ffn_w_up=ffn_w_up, ffn_conv_w=ffn_conv_w,
             ffn_conv_b=ffn_conv_b, ffn_w_down=ffn_w_down)
    caches = (cache_nsa_kv, cache_nsa_win, cache_fox_kv, cache_fox_logf, state_hgrn, state_s5, state_ffn_conv)
    return trunk(x_prompt, x_sample, caches, page_table, hgrn_lb_logits, P)
```

```python
import math
from functools import partial

import numpy as np
import jax
import jax.numpy as jnp
from jax import lax
from jax.experimental import pallas as pl
from jax.experimental.pallas import tpu as pltpu

D_MODEL = 1024
DEPTH = 4
PAGE_SIZE = 128
HEAD_DIM = 64
N_BRANCH = 4
MIX_W = D_MODEL // N_BRANCH
NSA_HEADS = MIX_W // HEAD_DIM
CMP_LEN = 32
CMP_STRIDE = 16
SEL_BLOCK = 64
N_SEL = 16
WINDOW = 512
SEL_FORCE = 1000.0
HG_HEADS = 4
HG_DK = MIX_W // HG_HEADS
HG_DV = MIX_W // HG_HEADS
HG_CHUNK = 64
S5_GROUP = 16
S5_GROUPS = MIX_W // S5_GROUP
S5_STATE = 64
S5_LANES = S5_GROUPS * S5_STATE
FOX_HEADS = MIX_W // HEAD_DIM
Q_BLOCK = 128
D_FF = ((8 * D_MODEL // 3 + 127) // 128) * 128
CONV_W = 3
ROPE_THETA = 10000.0
EPS = 1e-6
SCALE = HEAD_DIM ** -0.5
LOG2E = math.log2(math.e)
NEG = -1e30

IN_SIZES = (NSA_HEADS * HEAD_DIM, 6 * HEAD_DIM, 3 * NSA_HEADS,
            HG_HEADS * HG_DK, HG_HEADS * HG_DK, HG_HEADS * HG_DV, HG_HEADS * HG_DV,
            MIX_W,
            FOX_HEADS * HEAD_DIM, FOX_HEADS * HEAD_DIM, FOX_HEADS * HEAD_DIM, FOX_HEADS,
            N_BRANCH * D_MODEL)

LANES = 128
SUBLANES = 8
VMEM_LIMIT_BYTES = 56 * 1024 * 1024

F32 = jnp.float32
BF16 = jnp.bfloat16

C_NQ, C_NKV, C_NG, C_HG, C_SU, C_FQ, C_FK, C_FV, C_FF, C_END = 0, 256, 640, 768, 1792, 2048, 2304, 2560, 2816, 2944


def _cparams(sem):
    return pltpu.CompilerParams(dimension_semantics=sem, vmem_limit_bytes=VMEM_LIMIT_BYTES)


def _split2(x):
    hi = x.astype(BF16)
    lo = (x - hi.astype(F32)).astype(BF16)
    return hi, lo


def _split3(x):
    hi = x.astype(BF16)
    r = x - hi.astype(F32)
    mid = r.astype(BF16)
    lo = (r - mid.astype(F32)).astype(BF16)
    return hi, mid, lo


def _dot(a, b):
    return jnp.dot(a, b, preferred_element_type=F32)


def _dot_nt(a, b):
    return lax.dot_general(a, b, (((1,), (1,)), ((), ())), preferred_element_type=F32)


def _dot_tn(a, b):
    return lax.dot_general(a, b, (((0,), (0,)), ((), ())), preferred_element_type=F32)


def _dot_exact_lhs(a01, x):
    hi, mid, lo = _split3(x)
    return _dot(a01, hi) + _dot(a01, mid) + _dot(a01, lo)


def _dot_exact_rhs(x, b01):
    hi, mid, lo = _split3(x)
    return _dot(hi, b01) + _dot(mid, b01) + _dot(lo, b01)


def _seg_rms(x, bd, gain):
    ms = _dot_exact_rhs(x * x, bd) * (1.0 / HEAD_DIM)
    return x * lax.rsqrt(ms + EPS) * gain


def _rope(x, cos, sin_signed):
    w = x.shape[-1]
    half = HEAD_DIM // 2
    lane = lax.broadcasted_iota(jnp.int32, x.shape, 1)
    first = (lane % HEAD_DIM) < half
    swapped = jnp.where(first, pltpu.roll(x, w - half, 1), pltpu.roll(x, half, 1))
    return x * cos + swapped * sin_signed


def _log_sigmoid(x):
    return jnp.minimum(x, 0.0) - jnp.log(1.0 + jnp.exp(-jnp.abs(x)))


def _sigmoid(x):
    return 1.0 / (1.0 + jnp.exp(-x))


def _silu(x):
    return x * _sigmoid(x)


def _mm_kernel(x_ref, w_ref, o_ref):
    o_ref[...] = _dot(x_ref[...].astype(BF16), w_ref[...])


def matmul(x, w, tm=256):
    m, k = x.shape
    n = w.shape[1]
    tm = min(tm, m)
    return pl.pallas_call(
        _mm_kernel,
        grid=(m // tm,),
        in_specs=[pl.BlockSpec((tm, k), lambda i: (i, 0)),
                  pl.BlockSpec((k, n), lambda i: (0, 0))],
        out_specs=pl.BlockSpec((tm, n), lambda i: (i, 0)),
        out_shape=jax.ShapeDtypeStruct((m, n), F32),
        compiler_params=_cparams(("arbitrary",)),
        name="matmul",
    )(x, w.astype(BF16))


def _in_proj_kernel(x_ref, g_ref, w_ref, cos_ref, sin_ref, bd_ref, gq_ref, gk_ref, gfq_ref, gfk_ref, fb_ref,
                    h_ref, qn_ref, qr_ref, rows_ref, win_ref, ng_ref, hg_ref, su_ref, fq_ref, fox_ref, flog_ref,
                    rows_b_ref, win_b_ref, fox_b_ref, rows_t_ref, win_t_ref, fox_t_ref, flog_t_ref):
    x = x_ref[...]
    h = x * lax.rsqrt(jnp.mean(x * x, axis=-1, keepdims=True) + EPS) * g_ref[...]
    hb = h.astype(BF16)
    h_ref[...] = hb
    p = _dot(hb, w_ref[...])
    bd = bd_ref[...]
    cos, sin = cos_ref[...], sin_ref[...]
    qn = _seg_rms(p[:, C_NQ:C_NKV], bd, gq_ref[...])
    qn_ref[...] = qn
    qr_ref[...] = _rope(qn, cos, sin)
    raw = p[:, C_NKV:C_NKV + 256]
    lane = lax.broadcasted_iota(jnp.int32, raw.shape, 1)
    nr = _rope(_seg_rms(raw, bd, gk_ref[:, 0:256]), cos, sin)
    rows = jnp.where((lane >= 2 * HEAD_DIM) & (lane < 3 * HEAD_DIM), nr, raw)
    raww = p[:, C_NKV + 256:C_NG]
    nw = _rope(_seg_rms(raww, bd[0:128, 0:128], gk_ref[:, 256:384]), cos[:, 0:128], sin[:, 0:128])
    lane_w = lax.broadcasted_iota(jnp.int32, raww.shape, 1)
    win = jnp.where(lane_w < HEAD_DIM, nw, raww)
    ng_ref[...] = p[:, C_NG:C_HG]
    hg_ref[...] = p[:, C_HG:C_SU]
    su_ref[...] = p[:, C_SU:C_FQ]
    fq_ref[...] = _seg_rms(p[:, C_FQ:C_FK], bd, gfq_ref[...])
    fk = _seg_rms(p[:, C_FK:C_FV], bd, gfk_ref[...])
    fv = p[:, C_FV:C_FF]
    flog = _log_sigmoid(p[:, C_FF:C_END] + fb_ref[...])
    rows_ref[...] = rows
    win_ref[...] = win
    fox_ref[:, 0:256] = fk
    fox_ref[:, 256:512] = fv
    flog_ref[...] = flog
    rows_b_ref[...] = rows.astype(BF16)
    win_b_ref[...] = win.astype(BF16)
    fox_b_ref[:, 0:256] = fk.astype(BF16)
    fox_b_ref[:, 256:512] = fv.astype(BF16)
    rows_t_ref[...] = rows.T
    win_t_ref[...] = win.T
    fox_t_ref[0:256, :] = fk.T
    fox_t_ref[256:512, :] = fv.T
    flog_t_ref[...] = flog.T[0:SUBLANES, :]


def in_proj(x, lw, bd256, cos_t, sin_t, tm, nb):
    t = x.shape[0]
    seg = t // nb
    nt = seg // tm
    row = lambda i: (i, 0)
    const = lambda i: (0, 0)
    tab = lambda i: (i % nt, 0)
    tmap = lambda i: (i // nt, 0, i % nt)
    widths = (256, 256, 256, 128, 128, 1024, 256, 256, 512, 128)
    bwidths = (256, 128, 512)
    twidths = (256, 128, 512, SUBLANES)
    out_shape = ([jax.ShapeDtypeStruct((t, D_MODEL), BF16)] + [jax.ShapeDtypeStruct((t, w), F32) for w in widths]
                 + [jax.ShapeDtypeStruct((t, w), BF16) for w in bwidths]
                 + [jax.ShapeDtypeStruct((nb, w, seg), F32) for w in twidths])
    out_specs = ([pl.BlockSpec((tm, D_MODEL), row)] + [pl.BlockSpec((tm, w), row) for w in widths]
                 + [pl.BlockSpec((tm, w), row) for w in bwidths]
                 + [pl.BlockSpec((None, w, tm), tmap) for w in twidths])
    return pl.pallas_call(
        _in_proj_kernel,
        grid=(t // tm,),
        in_specs=[pl.BlockSpec((tm, D_MODEL), row), pl.BlockSpec((1, D_MODEL), const),
                  pl.BlockSpec((D_MODEL, C_END), const), pl.BlockSpec((tm, 256), tab), pl.BlockSpec((tm, 256), tab),
                  pl.BlockSpec((256, 256), const), pl.BlockSpec((1, 256), const), pl.BlockSpec((1, 384), const),
                  pl.BlockSpec((1, 256), const), pl.BlockSpec((1, 256), const), pl.BlockSpec((1, 128), const)],
        out_specs=out_specs,
        out_shape=out_shape,
        compiler_params=_cparams(("arbitrary",)),
        name="in_proj",
    )(x, lw['attn_norm'], lw['w_in'], cos_t, sin_t, bd256, lw['gq'], lw['gk'], lw['gfq'], lw['gfk'], lw['fbias'])


def _topk_mask(score, k, n_cand):
    r = -(-n_cand // SUBLANES) * SUBLANES
    s = score.T[0:r, :]
    idx_f = lax.broadcasted_iota(jnp.int32, s.shape, 0).astype(F32)
    sel = jnp.zeros(s.shape, F32)
    for _ in range(k):
        m = jnp.max(s, axis=0, keepdims=True)
        first = jnp.min(jnp.where(s == m, idx_f, float(LANES)), axis=0, keepdims=True)
        pick = idx_f == first
        sel = jnp.where(pick, 1.0, sel)
        s = jnp.where(pick, NEG, s)
    if r < LANES:
        sel = jnp.concatenate([sel, jnp.zeros((LANES - r, s.shape[1]), F32)], axis=0)
    return sel.T


def _topk_rank(score_row, k, eye, ones):
    n = score_row.shape[1]
    row_b = jnp.broadcast_to(score_row, (n, n))
    col_b = _dot_exact_rhs(eye.astype(F32) * row_b, ones)
    ii = lax.broadcasted_iota(jnp.int32, (n, n), 0)
    jj = lax.broadcasted_iota(jnp.int32, (n, n), 1)
    beats = (row_b > col_b) | ((row_b == col_b) & (jj < ii))
    return _dot(jnp.where(beats, 1.0, 0.0).astype(BF16), ones) < k


def _pe_term(pe_ref, wkv_ref):
    pew = _dot(pe_ref[...], wkv_ref[...])
    return pew[0:1, :] + pew[1:2, :]


def _cmp_tokens(p, pew, gain_c, n_c):
    n16 = p.shape[0]
    nxt = pltpu.roll(p, n16 - 1, 0)
    kc = p[:, 0:64] + nxt[:, 128:192] + pew[0:1, 0:64]
    vc = p[:, 64:128] + nxt[:, 192:256] + pew[0:1, 64:128]
    kc = kc * lax.rsqrt(jnp.mean(kc * kc, axis=-1, keepdims=True) + EPS) * gain_c
    return kc, vc


def _cmp_attend(q, kc, vc, valid):
    qh, ql = _split2(q)
    kh, kl = _split2(kc)
    s = _dot_nt(qh, kh) + _dot_nt(ql, kh) + _dot_nt(qh, kl)
    s = jnp.where(valid, s, NEG)
    m = jnp.max(s, axis=1, keepdims=True)
    e = jnp.where(valid, jnp.exp(s - m), 0.0)
    pc = e / jnp.maximum(jnp.sum(e, axis=1, keepdims=True), 1e-30)
    return pc, _dot(pc.astype(BF16), vc.astype(BF16))


def _cmp_kernel(q_ref, p_ref, pe_ref, wkv_ref, gc_ref, cover_ref, oc_ref, mask_ref, kc_ref, vc_ref, *, tq, n_c):
    qi = pl.program_id(1)

    @pl.when(qi == 0)
    def _():
        kc, vc = _cmp_tokens(p_ref[...], _pe_term(pe_ref, wkv_ref), gc_ref[...], n_c)
        kc_ref[...] = kc
        vc_ref[...] = vc

    kc, vc = kc_ref[...], vc_ref[...]
    n16 = kc.shape[0]
    pos = qi * tq + lax.broadcasted_iota(jnp.int32, (tq, n16), 0)
    ci = lax.broadcasted_iota(jnp.int32, (tq, n16), 1)
    valid = (ci * CMP_STRIDE + CMP_LEN - 1 <= pos) & (ci < n_c)
    q = q_ref[...]
    pcs = jnp.zeros((tq, n16), F32)
    outs = []
    for h in range(NSA_HEADS):
        pc, o = _cmp_attend(q[:, h * HEAD_DIM:(h + 1) * HEAD_DIM] * SCALE, kc, vc, valid)
        pcs = pcs + pc
        outs.append(o)
    oc_ref[...] = jnp.concatenate(outs, axis=1)
    imp = _dot_exact_rhs(pcs, cover_ref[...])
    j = lax.broadcasted_iota(jnp.int32, (tq, LANES), 1)
    cur = (qi * tq + lax.broadcasted_iota(jnp.int32, (tq, LANES), 0)) // SEL_BLOCK
    forced = (j == 0) | (j == cur) | (j == cur - 1)
    score = jnp.where(j > cur, NEG, imp + jnp.where(forced, SEL_FORCE, 0.0))
    mask_ref[...] = _topk_mask(score, N_SEL, (n_c + 1) * CMP_STRIDE // SEL_BLOCK)


def nsa_cmp_prompt(qn, pparts, lw, cover, tq=256):
    b, l, _ = qn.shape
    n16 = l // CMP_STRIDE
    n_c = n16 - 1
    return pl.pallas_call(
        partial(_cmp_kernel, tq=tq, n_c=n_c),
        grid=(b, l // tq),
        in_specs=[pl.BlockSpec((None, tq, 256), lambda bi, qi: (bi, qi, 0)),
                  pl.BlockSpec((None, n16, 256), lambda bi, qi: (bi, 0, 0)),
                  pl.BlockSpec((SUBLANES, CMP_LEN * HEAD_DIM), lambda bi, qi: (0, 0)),
                  pl.BlockSpec((CMP_LEN * HEAD_DIM, 128), lambda bi, qi: (0, 0)),
                  pl.BlockSpec((1, HEAD_DIM), lambda bi, qi: (0, 0)),
                  pl.BlockSpec((n16, LANES), lambda bi, qi: (0, 0))],
        out_specs=[pl.BlockSpec((None, tq, 256), lambda bi, qi: (bi, qi, 0)),
                   pl.BlockSpec((None, tq, LANES), lambda bi, qi: (bi, qi, 0))],
        out_shape=[jax.ShapeDtypeStruct((b, l, 256), F32), jax.ShapeDtypeStruct((b, l, LANES), F32)],
        scratch_shapes=[pltpu.VMEM((n16, HEAD_DIM), F32), pltpu.VMEM((n16, HEAD_DIM), F32)],
        compiler_params=_cparams(("arbitrary", "arbitrary")),
        name="nsa_cmp",
    )(qn, pparts, lw['pe8'], lw['wkv'], lw['gc'], cover)


def _online_update(s, v, m_ref, l_ref, acc_ref, h):
    m_old = m_ref[h]
    m_new = jnp.maximum(m_old, jnp.max(s, axis=1, keepdims=True))
    alpha = jnp.exp2(m_old - m_new)
    p = jnp.exp2(s - jnp.concatenate([m_new] * (s.shape[1] // LANES), axis=1))
    l_ref[h] = alpha * l_ref[h] + jnp.sum(p, axis=1, keepdims=True)
    acc_ref[h] = alpha[:, 0:HEAD_DIM] * acc_ref[h] + _dot(p.astype(BF16), v)
    m_ref[h] = m_new


def _flash_init(m_refs, l_refs, acc_refs):
    for r in m_refs:
        r[...] = jnp.full(r.shape, NEG, F32)
    for r in l_refs + acc_refs:
        r[...] = jnp.zeros(r.shape, F32)


def _nsa_flash_kernel(q_ref, rows_ref, win_ref, mask_ref, e_ref, ng_ref, oc_ref, o_ref,
                      qs_ref, ms_ref, ls_ref, as_ref, mw_ref, lw_ref, aw_ref, *, tq, tk):
    qi, kj = pl.program_id(1), pl.program_id(2)
    nk = pl.num_programs(2)

    @pl.when(kj == 0)
    def _():
        _flash_init([ms_ref, mw_ref], [ls_ref, lw_ref], [as_ref, aw_ref])
        q = q_ref[...]
        for h in range(NSA_HEADS):
            qs_ref[h] = (q[:, h * HEAD_DIM:(h + 1) * HEAD_DIM] * (SCALE * LOG2E)).astype(BF16)

    q_lo = qi * tq
    k_lo = kj * tk
    qpos = q_lo + lax.broadcasted_iota(jnp.int32, (tq, tk), 0)
    kpos = k_lo + lax.broadcasted_iota(jnp.int32, (tq, tk), 1)
    causal = kpos <= qpos

    @pl.when(k_lo <= q_lo + tq - 1)
    def _():
        rows = rows_ref[...]
        ks = rows[:, 2 * HEAD_DIM:3 * HEAD_DIM]
        vs = rows[:, 3 * HEAD_DIM:4 * HEAD_DIM]
        sel = _dot(mask_ref[...].astype(BF16), e_ref[...]) > 0.5
        bias = jnp.where(sel & causal, 0.0, NEG)
        for h in range(NSA_HEADS):
            _online_update(_dot_nt(qs_ref[h], ks) + bias, vs, ms_ref, ls_ref, as_ref, h)

    @pl.when((k_lo <= q_lo + tq - 1) & (k_lo + tk - 1 > q_lo - WINDOW))
    def _():
        win = win_ref[...]
        kw = win[:, 0:HEAD_DIM]
        vw = win[:, HEAD_DIM:2 * HEAD_DIM]
        bias = jnp.where(causal & (qpos - kpos < WINDOW), 0.0, NEG)
        for h in range(NSA_HEADS):
            _online_update(_dot_nt(qs_ref[h], kw) + bias, vw, mw_ref, lw_ref, aw_ref, h)

    @pl.when(kj == nk - 1)
    def _():
        gs = _sigmoid(ng_ref[...])
        oc = oc_ref[...]
        outs = []
        for h in range(NSA_HEADS):
            o_s = as_ref[h] / ls_ref[h][:, 0:HEAD_DIM]
            o_w = aw_ref[h] / lw_ref[h][:, 0:HEAD_DIM]
            outs.append(gs[:, 3 * h:3 * h + 1] * oc[:, h * HEAD_DIM:(h + 1) * HEAD_DIM]
                        + gs[:, 3 * h + 1:3 * h + 2] * o_s + gs[:, 3 * h + 2:3 * h + 3] * o_w)
        o_ref[...] = jnp.concatenate(outs, axis=1)


def nsa_flash_prompt(qr, rows, win, mask, expand, ng, oc, tq=512, tk=512):
    b, l, _ = qr.shape
    tq, tk = min(tq, l), min(tk, l)
    nq, nk = l // tq, l // tk

    def kv_idx(bi, qi, kj):
        return (bi, jnp.minimum(kj, (qi * tq + tq - 1) // tk), 0)

    def win_idx(bi, qi, kj):
        hi = (qi * tq + tq - 1) // tk
        lo = jnp.maximum((qi * tq - WINDOW + 1) // tk, 0)
        return (bi, jnp.clip(kj, lo, hi), 0)

    qmap = lambda bi, qi, kj: (bi, qi, 0)
    hshape = (NSA_HEADS, tq, LANES)
    return pl.pallas_call(
        partial(_nsa_flash_kernel, tq=tq, tk=tk),
        grid=(b, nq, nk),
        in_specs=[pl.BlockSpec((None, tq, 256), qmap),
                  pl.BlockSpec((None, tk, 256), kv_idx),
                  pl.BlockSpec((None, tk, 128), win_idx),
                  pl.BlockSpec((None, tq, LANES), qmap),
                  pl.BlockSpec((LANES, tk), lambda bi, qi, kj: (0, jnp.minimum(kj, (qi * tq + tq - 1) // tk))),
                  pl.BlockSpec((None, tq, LANES), qmap),
                  pl.BlockSpec((None, tq, 256), qmap)],
        out_specs=pl.BlockSpec((None, tq, 256), qmap),
        out_shape=jax.ShapeDtypeStruct((b, l, 256), F32),
        scratch_shapes=[pltpu.VMEM((NSA_HEADS, tq, HEAD_DIM), BF16),
                        pltpu.VMEM(hshape, F32), pltpu.VMEM(hshape, F32), pltpu.VMEM((NSA_HEADS, tq, HEAD_DIM), F32),
                        pltpu.VMEM(hshape, F32), pltpu.VMEM(hshape, F32), pltpu.VMEM((NSA_HEADS, tq, HEAD_DIM), F32)],
        compiler_params=_cparams(("arbitrary", "arbitrary", "arbitrary")),
        name="nsa_flash",
    )(qr, rows, win, mask, expand, ng, oc)


def _cumsum_kernel(x_ref, tri_ref, o_ref, carry_ref):
    @pl.when(pl.program_id(1) == 0)
    def _():
        carry_ref[...] = jnp.zeros(carry_ref.shape, F32)

    c = _dot_exact_lhs(tri_ref[...], x_ref[...]) + carry_ref[0:1, :]
    o_ref[...] = c
    carry_ref[0:1, :] = c[c.shape[0] - 1:c.shape[0], :]


def cumsum_time(x, tri):
    b, l, w = x.shape
    tc = tri.shape[0]
    return pl.pallas_call(
        _cumsum_kernel,
        grid=(b, l // tc),
        in_specs=[pl.BlockSpec((None, tc, w), lambda bi, i: (bi, i, 0)),
                  pl.BlockSpec((tc, tc), lambda bi, i: (0, 0))],
        out_specs=pl.BlockSpec((None, tc, w), lambda bi, i: (bi, i, 0)),
        out_shape=jax.ShapeDtypeStruct((b, l, w), F32),
        scratch_shapes=[pltpu.VMEM((SUBLANES, w), F32)],
        compiler_params=_cparams(("arbitrary", "arbitrary")),
        name="cumsum_time",
    )(x, tri)


def _fox_flash_kernel(q_ref, kv_ref, cq_ref, ck_ref, o_ref, qs_ref, cqb_ref, m_ref, l_ref, acc_ref, *, tq, tk):
    qi, kj = pl.program_id(1), pl.program_id(2)
    nk = pl.num_programs(2)

    @pl.when(kj == 0)
    def _():
        _flash_init([m_ref], [l_ref], [acc_ref])
        q = q_ref[...]
        cq = cq_ref[...]
        for h in range(FOX_HEADS):
            qs_ref[h] = (q[:, h * HEAD_DIM:(h + 1) * HEAD_DIM] * (SCALE * LOG2E)).astype(BF16)
            cqb_ref[h] = jnp.broadcast_to(cq[:, h:h + 1] * LOG2E, (tq, LANES))

    q_lo = qi * tq
    k_lo = kj * tk

    def step(on_diagonal):
        kv = kv_ref[...]
        ck = ck_ref[...] * LOG2E
        if on_diagonal:
            qpos = q_lo + lax.broadcasted_iota(jnp.int32, (tq, tk), 0)
            kpos = k_lo + lax.broadcasted_iota(jnp.int32, (tq, tk), 1)
            causal = jnp.where(kpos <= qpos, 0.0, NEG)
        for h in range(FOX_HEADS):
            k = kv[:, h * HEAD_DIM:(h + 1) * HEAD_DIM]
            v = kv[:, 256 + h * HEAD_DIM:256 + (h + 1) * HEAD_DIM]
            bias = jnp.concatenate([cqb_ref[h]] * (tk // LANES), axis=1) - ck[h:h + 1, :]
            if on_diagonal:
                bias = bias + causal
            _online_update(_dot_nt(qs_ref[h], k) + bias, v, m_ref, l_ref, acc_ref, h)

    @pl.when(k_lo + tk - 1 <= q_lo)
    def _():
        step(False)

    @pl.when((k_lo <= q_lo + tq - 1) & (k_lo + tk - 1 > q_lo))
    def _():
        step(True)

    @pl.when(kj == nk - 1)
    def _():
        o_ref[...] = jnp.concatenate(
            [acc_ref[h] / l_ref[h][:, 0:HEAD_DIM] for h in range(FOX_HEADS)], axis=1)


def fox_flash_prompt(fq, fox_rows, c, c_t, tq=512, tk=512):
    b, l, _ = fq.shape
    tq, tk = min(tq, l), min(tk, l)
    nq, nk = l // tq, l // tk
    qmap = lambda bi, qi, kj: (bi, qi, 0)
    last = lambda qi: (qi * tq + tq - 1) // tk
    hshape = (FOX_HEADS, tq, LANES)
    return pl.pallas_call(
        partial(_fox_flash_kernel, tq=tq, tk=tk),
        grid=(b, nq, nk),
        in_specs=[pl.BlockSpec((None, tq, 256), qmap),
                  pl.BlockSpec((None, tk, 512), lambda bi, qi, kj: (bi, jnp.minimum(kj, last(qi)), 0)),
                  pl.BlockSpec((None, tq, LANES), qmap),
                  pl.BlockSpec((None, SUBLANES, tk), lambda bi, qi, kj: (bi, 0, jnp.minimum(kj, last(qi))))],
        out_specs=pl.BlockSpec((None, tq, 256), qmap),
        out_shape=jax.ShapeDtypeStruct((b, l, 256), F32),
        scratch_shapes=[pltpu.VMEM((FOX_HEADS, tq, HEAD_DIM), BF16), pltpu.VMEM(hshape, F32),
                        pltpu.VMEM(hshape, F32), pltpu.VMEM(hshape, F32), pltpu.VMEM((FOX_HEADS, tq, HEAD_DIM), F32)],
        compiler_params=_cparams(("arbitrary", "arbitrary", "arbitrary")),
        name="fox_flash",
    )(fq, fox_rows, c, c_t)


HG_LEVELS = int(math.log2(HG_CHUNK))


def _hgrn_constants():
    c = HG_CHUNK
    t = np.arange(c)[:, None]
    u = np.arange(c)[None, :]
    mats = [(u <= t), (u > t)]
    masks = []
    for lv in range(HG_LEVELS):
        bit = (t >> lv) & 1
        p_t = (t >> lv) << lv
        mats.append((bit == 1) & (u >= p_t) & (u <= t))
    for lv in range(HG_LEVELS):
        bit = (t >> lv) & 1
        p_s = ((t >> lv) + 1) << lv
        mats.append((bit == 0) & (u > t) & (u <= p_s - 1))
    for lv in range(HG_LEVELS):
        masks.append(((t >> (lv + 1)) == (u >> (lv + 1))) & (((t >> lv) & 1) == 1) & (((u >> lv) & 1) == 0))
    masks.append(t == u)
    cst = np.concatenate([m.astype(np.float32) for m in mats], axis=0)
    msk = np.stack([m.astype(np.float32) for m in masks], axis=0)
    return cst, msk


def _hgrn_kernel(hg_ref, lb_ref, cst_ref, msk_ref, eye_ref, bd_ref, go_ref, o_ref, st_ref, s_ref, *, tt):
    i = pl.program_id(1)
    c = HG_CHUNK

    @pl.when(i == 0)
    def _():
        s_ref[...] = jnp.zeros(s_ref.shape, F32)

    lb = lb_ref[...]
    nbb = hg_ref.shape[0]
    bd = bd_ref[...] > 0
    heads_bd = lambda x: jnp.where(bd, jnp.concatenate([x] * HG_HEADS, axis=0), jnp.zeros((), x.dtype))
    msk = [jnp.concatenate([msk_ref[lv]] * HG_HEADS, axis=1) for lv in range(HG_LEVELS + 1)]
    for ch, bb in [(ch, bb) for ch in range(tt // c) for bb in range(nbb)]:
        r = ch * c
        hq = hg_ref[bb, r:r + c, 0:256]
        hf = hg_ref[bb, r:r + c, 256:512]
        hv = hg_ref[bb, r:r + c, 512:768]
        hgate = hg_ref[bb, r:r + c, 768:1024]
        f = lb + (1.0 - lb) * _sigmoid(hf)
        lf = jnp.log(f)
        kk = 1.0 - f
        e = _dot(cst_ref[...], jnp.concatenate(_split3(lf), axis=0))
        eb = jnp.exp(e[0:c])
        esu = jnp.exp(e[c:2 * c])
        ebl = eb[c - 1:c, :]
        qf = [(hq * jnp.exp(e[(2 + lv) * c:(3 + lv) * c])).astype(BF16) for lv in range(HG_LEVELS)]
        kg = [(kk * jnp.exp(e[(2 + HG_LEVELS + lv) * c:(3 + HG_LEVELS + lv) * c])).astype(BF16)
              for lv in range(HG_LEVELS)]
        qb, kb, vb = hq.astype(BF16), kk.astype(BF16), hv.astype(BF16)
        qe, ke = (hq * eb).astype(BF16), (kk * esu).astype(BF16)
        att = msk[HG_LEVELS] * _dot_nt(qb, heads_bd(kb))
        for lv in range(HG_LEVELS):
            att = att + msk[lv] * _dot_nt(qf[lv], heads_bd(kg[lv]))
        st = s_ref[bb]
        o = _dot(att.astype(BF16), heads_bd(vb)) + _dot_nt(qe, st.astype(BF16))
        s_ref[bb] = st * ebl + jnp.where(bd, _dot_tn(vb, ke), 0.0)
        o_ref[bb, r:r + c, :] = _seg_rms(o, bd_ref[...], go_ref[...]) * _silu(hgate)

    @pl.when(i == pl.num_programs(1) - 1)
    def _():
        for bb in range(nbb):
            for h in range(HG_HEADS):
                blk = s_ref[bb, h * HG_DV:(h + 1) * HG_DV, h * HG_DK:(h + 1) * HG_DK]
                hi, mid, lo = _split3(blk)
                eye = eye_ref[...]
                st_ref[bb, h] = _dot_nt(eye, hi) + _dot_nt(eye, mid) + _dot_nt(eye, lo)


def hgrn_prompt(hg, lw, cn, tt=256, nbb=2):
    b, l, _ = hg.shape
    const2 = lambda bi, i: (0, 0)
    return pl.pallas_call(
        partial(_hgrn_kernel, tt=tt),
        grid=(b // nbb, l // tt),
        in_specs=[pl.BlockSpec((nbb, tt, 1024), lambda bi, i: (bi, i, 0)),
                  pl.BlockSpec((1, 256), const2),
                  pl.BlockSpec(cn['hg_cst'].shape, const2),
                  pl.BlockSpec(cn['hg_msk'].shape, lambda bi, i: (0, 0, 0)),
                  pl.BlockSpec((HG_DK, HG_DK), const2),
                  pl.BlockSpec((256, 256), const2),
                  pl.BlockSpec((1, 256), const2)],
        out_specs=[pl.BlockSpec((nbb, tt, 256), lambda bi, i: (bi, i, 0)),
                   pl.BlockSpec((nbb, HG_HEADS, HG_DK, HG_DV), lambda bi, i: (bi, 0, 0, 0))],
        out_shape=[jax.ShapeDtypeStruct((b, l, 256), F32),
                   jax.ShapeDtypeStruct((b, HG_HEADS, HG_DK, HG_DV), F32)],
        scratch_shapes=[pltpu.VMEM((nbb, HG_HEADS * HG_DV, HG_HEADS * HG_DK), F32)],
        compiler_params=_cparams(("arbitrary", "arbitrary")),
        name="hgrn",
    )(hg, lw['lb'], cn['hg_cst'], cn['hg_msk'], cn['eye64'], cn['bd256'], lw['go'])


S5_BATCH = 4


def _gelu_tanh(x):
    return 0.5 * x * (1.0 + jnp.tanh(math.sqrt(2.0 / math.pi) * (x + 0.044715 * (x * x * x))))


def _s5_kernel(u_ref, bre_ref, bim_ref, ar_ref, ai_ref, cre_ref, cim_ref, d_ref, gw_ref, gb_ref,
               o_ref, hout_ref, xs_ref, h_ref, *, tc):
    i = pl.program_id(0)

    @pl.when(i == 0)
    def _():
        h_ref[...] = jnp.zeros(h_ref.shape, F32)

    nlt = S5_LANES // LANES
    for b in range(S5_BATCH):
        ub = u_ref[b].astype(BF16)
        xre, xim = _dot(ub, bre_ref[...]), _dot(ub, bim_ref[...])
        for j in range(nlt):
            xs_ref[j, pl.ds(b, tc, stride=SUBLANES), :] = xre[:, j * LANES:(j + 1) * LANES]
            xs_ref[j, pl.ds(S5_BATCH + b, tc, stride=SUBLANES), :] = xim[:, j * LANES:(j + 1) * LANES]
    ar = [ar_ref[:, j * LANES:(j + 1) * LANES] for j in range(nlt)]
    ai = [ai_ref[:, j * LANES:(j + 1) * LANES] for j in range(nlt)]

    def body(t, hcur):
        off = pl.multiple_of(t * SUBLANES, SUBLANES)
        hnew = []
        for j in range(nlt):
            hj = ar[j] * hcur[j] + ai[j] * pltpu.roll(hcur[j], S5_BATCH, 0) + xs_ref[j, pl.ds(off, SUBLANES), :]
            xs_ref[j, pl.ds(off, SUBLANES), :] = hj
            hnew.append(hj)
        return tuple(hnew)

    h0 = tuple(h_ref[:, j * LANES:(j + 1) * LANES] for j in range(nlt))
    hfin = lax.fori_loop(0, tc, body, h0, unroll=8)
    for j in range(nlt):
        h_ref[:, j * LANES:(j + 1) * LANES] = hfin[j]
    for b in range(S5_BATCH):
        hre = jnp.concatenate([xs_ref[j, pl.ds(b, tc, stride=SUBLANES), :] for j in range(nlt)], axis=1)
        him = jnp.concatenate([xs_ref[j, pl.ds(S5_BATCH + b, tc, stride=SUBLANES), :] for j in range(nlt)], axis=1)
        y = _dot(hre.astype(BF16), cre_ref[...]) - _dot(him.astype(BF16), cim_ref[...]) + d_ref[...] * u_ref[b]
        z = _gelu_tanh(y)
        o_ref[b] = z * _sigmoid(_dot(z.astype(BF16), gw_ref[...]) + gb_ref[...])

    @pl.when(i == pl.num_programs(0) - 1)
    def _():
        hout_ref[...] = h_ref[...]


def s5_prompt(u, lw, tc=256):
    b, l, _ = u.shape
    assert b == S5_BATCH
    const = lambda i: (0, 0)
    return pl.pallas_call(
        partial(_s5_kernel, tc=tc),
        grid=(l // tc,),
        in_specs=[pl.BlockSpec((b, tc, 256), lambda i: (0, i, 0)),
                  pl.BlockSpec((256, S5_LANES), const), pl.BlockSpec((256, S5_LANES), const),
                  pl.BlockSpec((SUBLANES, S5_LANES), const), pl.BlockSpec((SUBLANES, S5_LANES), const),
                  pl.BlockSpec((S5_LANES, 256), const), pl.BlockSpec((S5_LANES, 256), const),
                  pl.BlockSpec((1, 256), const), pl.BlockSpec((256, 256), const), pl.BlockSpec((1, 256), const)],
        out_specs=[pl.BlockSpec((b, tc, 256), lambda i: (0, i, 0)),
                   pl.BlockSpec((SUBLANES, S5_LANES), const)],
        out_shape=[jax.ShapeDtypeStruct((b, l, 256), F32), jax.ShapeDtypeStruct((SUBLANES, S5_LANES), F32)],
        scratch_shapes=[pltpu.VMEM((S5_LANES // LANES, tc * SUBLANES, LANES), F32),
                        pltpu.VMEM((SUBLANES, S5_LANES), F32)],
        compiler_params=_cparams(("arbitrary",)),
        name="s5_scan",
    )(u, lw['s5_bre'], lw['s5_bim'], lw['s5_ar8'], lw['s5_ai8'], lw['s5_cre'], lw['s5_cim'],
      lw['s5_d'], lw['s5_gw'], lw['s5_gb'])


def _merge_kernel(x_ref, h_ref, b0_ref, b1_ref, b2_ref, b3_ref, wg_ref, wb_ref, wo_ref, o_ref):
    hb = h_ref[...]
    merged = None
    for n, br in enumerate((b0_ref, b1_ref, b2_ref, b3_ref)):
        g = _sigmoid(_dot(hb, wg_ref[:, n * D_MODEL:(n + 1) * D_MODEL]))
        term = g * _dot(br[...].astype(BF16), wb_ref[n])
        merged = term if merged is None else merged + term
    o_ref[...] = x_ref[...] + _dot(merged.astype(BF16), wo_ref[...])


def merge(x, hb, branches, lw, tm):
    t = x.shape[0]
    row = lambda i: (i, 0)
    const = lambda i: (0, 0)
    return pl.pallas_call(
        _merge_kernel,
        grid=(t // tm,),
        in_specs=[pl.BlockSpec((tm, D_MODEL), row), pl.BlockSpec((tm, D_MODEL), row)]
                 + [pl.BlockSpec((tm, MIX_W), row)] * N_BRANCH
                 + [pl.BlockSpec((D_MODEL, N_BRANCH * D_MODEL), const),
                    pl.BlockSpec((N_BRANCH, MIX_W, D_MODEL), lambda i: (0, 0, 0)),
                    pl.BlockSpec((D_MODEL, D_MODEL), const)],
        out_specs=pl.BlockSpec((tm, D_MODEL), row),
        out_shape=jax.ShapeDtypeStruct((t, D_MODEL), F32),
        compiler_params=_cparams(("arbitrary",)),
        name="merge",
    )(x, hb, *branches, lw['wg'], lw['wb'], lw['wo'])


def _ffn_core(x, g, wu, cw, cb, wd, a_m2, a_m1):
    h = x * lax.rsqrt(jnp.mean(x * x, axis=-1, keepdims=True) + EPS) * g
    up = _dot(h.astype(BF16), wu)
    a, bb = up[:, 0:D_FF], up[:, D_FF:2 * D_FF]
    ac = cb + a_m2(a) * cw[0:1, :] + a_m1(a) * cw[1:2, :] + a * cw[2:3, :]
    return a, x + _dot((_silu(ac) * bb).astype(BF16), wd)


def _ffn_seq_kernel(x_ref, g_ref, wu_ref, cw_ref, cb_ref, wd_ref, o_ref, cs_ref, prev_ref, *, tm):
    @pl.when(pl.program_id(1) == 0)
    def _():
        prev_ref[...] = jnp.zeros(prev_ref.shape, F32)

    prev = prev_ref[...]
    row = lax.broadcasted_iota(jnp.int32, (tm, D_FF), 0)
    p1, p2 = prev[SUBLANES - 1:SUBLANES, :], prev[SUBLANES - 2:SUBLANES - 1, :]
    a_m1 = lambda a: jnp.where(row == 0, p1, pltpu.roll(a, 1, 0))
    a_m2 = lambda a: jnp.where(row == 0, p2, jnp.where(row == 1, p1, pltpu.roll(a, 2, 0)))
    a, o = _ffn_core(x_ref[...], g_ref[...], wu_ref[...], cw_ref[...], cb_ref[...], wd_ref[...], a_m2, a_m1)
    o_ref[...] = o
    tail = a[tm - SUBLANES:tm, :]
    prev_ref[...] = tail
    cs_ref[...] = tail[SUBLANES - (CONV_W - 1):SUBLANES, :]


def _ffn_step_kernel(x_ref, cs_ref, g_ref, wu_ref, cw_ref, cb_ref, wd_ref, o_ref, cso_ref):
    b0, b1 = cs_ref[:, 0, :], cs_ref[:, 1, :]
    a, o = _ffn_core(x_ref[...], g_ref[...], wu_ref[...], cw_ref[...], cb_ref[...], wd_ref[...],
                     lambda a: b0, lambda a: b1)
    o_ref[...] = o
    cso_ref[:, 0, :] = b1
    cso_ref[:, 1, :] = a


def _ffn_weight_specs(const):
    return [pl.BlockSpec((1, D_MODEL), const), pl.BlockSpec((D_MODEL, 2 * D_FF), const),
            pl.BlockSpec((CONV_W, D_FF), const), pl.BlockSpec((1, D_FF), const), pl.BlockSpec((D_FF, D_MODEL), const)]


def ffn_prompt(x, lw, b, l, tm=256):
    nt = l // tm
    const = lambda bi, i: (0, 0)
    row = lambda bi, i: (bi * nt + i, 0)
    return pl.pallas_call(
        partial(_ffn_seq_kernel, tm=tm),
        grid=(b, nt),
        in_specs=[pl.BlockSpec((tm, D_MODEL), row)] + _ffn_weight_specs(const),
        out_specs=[pl.BlockSpec((tm, D_MODEL), row),
                   pl.BlockSpec((None, CONV_W - 1, D_FF), lambda bi, i: (bi, 0, 0))],
        out_shape=[jax.ShapeDtypeStruct((b * l, D_MODEL), F32), jax.ShapeDtypeStruct((b, CONV_W - 1, D_FF), F32)],
        scratch_shapes=[pltpu.VMEM((SUBLANES, D_FF), F32)],
        compiler_params=_cparams(("arbitrary", "arbitrary")),
        name="ffn_seq",
    )(x, lw['ffn_g'], lw['ffn_wu'], lw['ffn_cw'], lw['ffn_cb'], lw['ffn_wd'])


def ffn_step(x, conv_state, l, lw):
    n = x.shape[0]
    const = lambda i: (0, 0)
    return pl.pallas_call(
        _ffn_step_kernel,
        grid=(1,),
        in_specs=[pl.BlockSpec((n, D_MODEL), const),
                  pl.BlockSpec((None, n, CONV_W - 1, D_FF), lambda i: (l, 0, 0, 0))] + _ffn_weight_specs(const),
        out_specs=[pl.BlockSpec((n, D_MODEL), const), pl.BlockSpec((n, CONV_W - 1, D_FF), lambda i: (0, 0, 0))],
        out_shape=[jax.ShapeDtypeStruct((n, D_MODEL), F32), jax.ShapeDtypeStruct((n, CONV_W - 1, D_FF), F32)],
        compiler_params=_cparams(("arbitrary",)),
        name="ffn_step",
    )(x, conv_state, lw['ffn_g'], lw['ffn_wu'], lw['ffn_cw'], lw['ffn_cb'], lw['ffn_wd'])


def _heads_to_rows(row):
    return jnp.concatenate([row[:, h * HEAD_DIM:(h + 1) * HEAD_DIM] for h in range(NSA_HEADS)]
                           + [jnp.zeros((SUBLANES - NSA_HEADS, HEAD_DIM), F32)], axis=0)


def _rows_to_heads(o8):
    return jnp.concatenate([o8[h:h + 1, :] for h in range(NSA_HEADS)], axis=1)


def _col8(cells):
    return jnp.concatenate(list(cells) + [jnp.zeros((SUBLANES - len(cells), 1), F32)], axis=0)


def _all_max(x):
    return jnp.max(jnp.max(x, axis=1, keepdims=True), axis=0, keepdims=True)


def _all_sum(x):
    return jnp.sum(jnp.sum(x, axis=1, keepdims=True), axis=0, keepdims=True)


def _paged_softmax(s, valid, s_new):
    m = jnp.maximum(_all_max(jnp.where(valid, s, NEG)), s_new)
    e = jnp.where(valid, jnp.exp(s - m), 0.0)
    e_new = jnp.exp(s_new - m)
    inv = 1.0 / (_all_sum(e) + e_new)
    return e * inv, e_new * inv


SEQ_PER_STEP = 2


def _nsa_step_kernel(pt_ref, *refs, n_pages, wb, past_len):
    ns = SEQ_PER_STEP
    o = ns * n_pages
    winc_ref = refs[o]
    shared = refs[o + 1:o + 15]
    o_ref, nwin_ref = refs[o + 15], refs[o + 16]
    scr = refs[o + 17:]
    for s in range(ns):
        _nsa_step_one(pl.program_id(0) * ns + s, refs[s * n_pages:(s + 1) * n_pages], winc_ref.at[s], *shared,
                      o_ref, nwin_ref.at[s], *scr[3 * s:3 * s + 3], n_pages=n_pages, wb=wb, past_len=past_len)


def _nsa_step_one(b, kv, winc_ref, qn_ref, qr_ref, rows_ref, win_ref, ng_ref, wcmp_ref, pe_ref, wkv_ref, gc_ref,
                  cover_ref, amask_ref, ex_ref, eye_ref, ones_ref, o_ref, nwin_ref, xc_ref, sl_ref, pl_ref,
                  *, n_pages, wb, past_len):
    row = lambda ref: ref[pl.ds(b, 1), :]
    rnew, wnew = row(rows_ref), row(win_ref)
    gs = _sigmoid(row(ng_ref))
    for j in range(n_pages):
        xc_ref[j * PAGE_SIZE:(j + 1) * PAGE_SIZE, :] = kv[j][0:2 * HEAD_DIM, :].T
    n16 = n_pages * PAGE_SIZE // CMP_STRIDE
    p = jnp.zeros((n16, 256), F32)
    for r in range(CMP_STRIDE):
        p = p + _dot(xc_ref[pl.ds(r, n16, stride=CMP_STRIDE), :].astype(BF16), wcmp_ref[r])
    n_c = n16 - 1
    kc, vc = _cmp_tokens(p, _pe_term(pe_ref, wkv_ref), gc_ref[...], n_c)
    ci = lax.broadcasted_iota(jnp.int32, (SUBLANES, n16), 1)
    ri = lax.broadcasted_iota(jnp.int32, (SUBLANES, n16), 0)
    pc, o_c = _cmp_attend(_heads_to_rows(row(qn_ref)) * SCALE, kc, vc, ci < n_c)
    pcs = jnp.sum(jnp.where(ri < NSA_HEADS, pc, 0.0), axis=0, keepdims=True)
    imp = _dot_exact_rhs(jnp.broadcast_to(pcs, (SUBLANES, n16)), cover_ref[...])
    j_l = lax.broadcasted_iota(jnp.int32, (SUBLANES, LANES), 1)
    cur = past_len // SEL_BLOCK
    forced = (j_l == 0) | (j_l == cur) | (j_l == cur - 1)
    score = jnp.where(j_l > cur, NEG, imp + jnp.where(forced, SEL_FORCE, 0.0))
    sel_b = _topk_rank(score[0:1, :], N_SEL, eye_ref[...], ones_ref[...])
    selx = _dot(amask_ref[...].astype(BF16), jnp.where(sel_b, ex_ref[...], 0.0).astype(BF16)) > 0.5
    qr8 = _heads_to_rows(row(qr_ref)) * SCALE
    qrb = qr8.astype(BF16)
    for j in range(n_pages):
        sl_ref[:, j, :] = _dot(qrb, kv[j][2 * HEAD_DIM:3 * HEAD_DIM, :].astype(BF16))
    s_new = jnp.sum(qr8 * rnew[:, 2 * HEAD_DIM:3 * HEAD_DIM], axis=1, keepdims=True)
    p_new = []
    for h in range(NSA_HEADS):
        ph, pn = _paged_softmax(sl_ref[h], selx, s_new[h:h + 1, :])
        pl_ref[h] = ph
        p_new.append(pn)
    pl_ref[NSA_HEADS:SUBLANES] = jnp.zeros((SUBLANES - NSA_HEADS, n_pages, PAGE_SIZE), F32)
    o_s = _col8(p_new) * rnew[:, 3 * HEAD_DIM:4 * HEAD_DIM]
    for j in range(n_pages):
        o_s = o_s + _dot_nt(pl_ref[:, j, :].astype(BF16), kv[j][3 * HEAD_DIM:4 * HEAD_DIM, :].astype(BF16))
    wc = winc_ref[...]
    s_w = _dot(qrb, wc[0:HEAD_DIM, :].astype(BF16))
    valid_w = lax.broadcasted_iota(jnp.int32, s_w.shape, 1) > wb - WINDOW
    s_wn = jnp.sum(qr8 * wnew[:, 0:HEAD_DIM], axis=1, keepdims=True)
    m_w = jnp.maximum(jnp.max(jnp.where(valid_w, s_w, NEG), axis=1, keepdims=True), s_wn)
    e_w = jnp.where(valid_w, jnp.exp(s_w - m_w), 0.0)
    e_wn = jnp.exp(s_wn - m_w)
    o_w = (_dot_nt(e_w.astype(BF16), wc[HEAD_DIM:2 * HEAD_DIM, :].astype(BF16)) + e_wn * wnew[:, HEAD_DIM:2 * HEAD_DIM]
           ) / (jnp.sum(e_w, axis=1, keepdims=True) + e_wn)
    gate = lambda k: _col8([gs[:, 3 * h + k:3 * h + k + 1] for h in range(NSA_HEADS)])
    o_ref[pl.ds(b, 1), :] = _rows_to_heads(gate(0) * o_c + gate(1) * o_s + gate(2) * o_w)
    wn8 = jnp.broadcast_to(wnew, (SUBLANES, 2 * HEAD_DIM))
    hi, mid, lo = _split3(wn8)
    eye = eye_ref[...]
    col = (_dot_nt(eye, hi) + _dot_nt(eye, mid) + _dot_nt(eye, lo))[:, 0:1]
    lane = lax.broadcasted_iota(jnp.int32, wc.shape, 1)
    nwin_ref[...] = jnp.where(lane == wb - 1, col, pltpu.roll(wc, wb - 1, 1))


def nsa_step(pt, kv_t, win_t, l, qn, qr, rows, win, ng, lw, cn):
    n = qn.shape[0]
    wb = win_t.shape[-1]
    n_pages = pt.shape[0] // n
    past_len = n_pages * PAGE_SIZE
    assert wb == WINDOW
    ns = SEQ_PER_STEP
    res = lambda w: pl.BlockSpec((n, w), lambda b, pt: (0, 0))
    c2 = lambda shape: pl.BlockSpec(shape, lambda b, pt: (0,) * len(shape))
    page = lambda s, j: pl.BlockSpec((None, None, 256, PAGE_SIZE), lambda b, pt: (l, pt[j * n + b * ns + s], 0, 0))
    seqs = range(ns)
    return pl.pallas_call(
        partial(_nsa_step_kernel, n_pages=n_pages, wb=wb, past_len=past_len),
        grid_spec=pltpu.PrefetchScalarGridSpec(
            num_scalar_prefetch=1,
            grid=(n // ns,),
            in_specs=[page(s, j) for s in seqs for j in range(n_pages)]
                     + [pl.BlockSpec((None, ns, 2 * HEAD_DIM, wb), lambda b, pt: (l, b, 0, 0)),
                        res(256), res(256), res(256), res(128), res(128),
                        c2((CMP_STRIDE, 128, 256)), c2((SUBLANES, CMP_LEN * HEAD_DIM)),
                        c2((CMP_LEN * HEAD_DIM, 128)), c2((1, HEAD_DIM)), c2((past_len // CMP_STRIDE, LANES)),
                        c2((n_pages, LANES)), c2((LANES, PAGE_SIZE)), c2((LANES, LANES)), c2((LANES, LANES))],
            out_specs=[pl.BlockSpec((n, 256), lambda b, pt: (0, 0)),
                       pl.BlockSpec((ns, 2 * HEAD_DIM, wb), lambda b, pt: (b, 0, 0))],
            scratch_shapes=[pltpu.VMEM((past_len, 2 * HEAD_DIM), F32),
                            pltpu.VMEM((SUBLANES, n_pages, PAGE_SIZE), F32),
                            pltpu.VMEM((SUBLANES, n_pages, PAGE_SIZE), F32)] * ns),
        out_shape=[jax.ShapeDtypeStruct((n, 256), F32), jax.ShapeDtypeStruct((n, 2 * HEAD_DIM, wb), F32)],
        compiler_params=_cparams(("arbitrary",)),
        name="nsa_step",
    )(pt, *([kv_t] * (ns * n_pages)), win_t, qn, qr, rows, win, ng, lw['wcmp'], lw['pe8'], lw['wkv'],
      lw['gc'], cn['cover_s'], cn['amask'], cn['ex'], cn['eye128'], cn['ones128'])


def _fox_step_kernel(pt_ref, *refs, n_pages):
    ns = SEQ_PER_STEP
    o = 2 * ns * n_pages
    shared = refs[o:o + 7]
    scr = refs[o + 7:]
    for s in range(ns):
        _fox_step_one(pl.program_id(0) * ns + s, refs[s * n_pages:(s + 1) * n_pages],
                      refs[(ns + s) * n_pages:(ns + s + 1) * n_pages], *shared, *scr[3 * s:3 * s + 3],
                      n_pages=n_pages)


def _fox_step_one(b, kv, lf, fq_ref, fox_ref, flog_ref, usuf_ref, ones_ref, lsu_ref, o_ref, lfs_ref, sl_ref, pl_ref,
                  *, n_pages):
    fq = fq_ref[pl.ds(b, 1), :]
    kvnew = fox_ref[pl.ds(b, 1), :]
    fl = flog_ref[pl.ds(b, 1), :]
    ri = lax.broadcasted_iota(jnp.int32, (SUBLANES, 256), 0)
    li = lax.broadcasted_iota(jnp.int32, (SUBLANES, 256), 1)
    bdm = (li // HEAD_DIM) == ri
    qbd = jnp.where(bdm, jnp.broadcast_to(fq * SCALE, (SUBLANES, 256)), 0.0)
    qb = qbd.astype(BF16)
    for j in range(n_pages):
        lfs_ref[:, j, :] = lf[j][...]
        sl_ref[:, j, :] = _dot(qb, kv[j][0:256, :].astype(BF16))
    s_new = jnp.sum(qbd * kvnew[:, 0:256], axis=1, keepdims=True)
    always = jnp.full((n_pages, PAGE_SIZE), True)
    p_new = []
    lf_all = lfs_ref[...].reshape(FOX_HEADS * n_pages, PAGE_SIZE)
    suf = _dot_exact_rhs(lf_all, usuf_ref[...])
    tail = _dot_exact_lhs(lsu_ref[...], _dot_exact_rhs(lf_all, ones_ref[...]))
    bias_all = suf + tail
    for h in range(FOX_HEADS):
        bias = bias_all[h * n_pages:(h + 1) * n_pages, :] + fl[:, h:h + 1]
        ph, pn = _paged_softmax(sl_ref[h] + bias, always, s_new[h:h + 1, :])
        pl_ref[h] = ph
        p_new.append(pn)
    pl_ref[FOX_HEADS:SUBLANES] = jnp.zeros((SUBLANES - FOX_HEADS, n_pages, PAGE_SIZE), F32)
    acc = _col8(p_new) * kvnew[:, 256:512]
    for j in range(n_pages):
        acc = acc + _dot_nt(pl_ref[:, j, :].astype(BF16), kv[j][256:512, :].astype(BF16))
    o_ref[pl.ds(b, 1), :] = jnp.sum(jnp.where(bdm, acc, 0.0), axis=0, keepdims=True)


def fox_step(pt, kv_t, lf_t, l, fq, foxr, flog, cn):
    n = fq.shape[0]
    n_pages = pt.shape[0] // n
    ns = SEQ_PER_STEP
    res = lambda w: pl.BlockSpec((n, w), lambda b, pt: (0, 0))
    c2 = lambda shape: pl.BlockSpec(shape, lambda b, pt: (0,) * len(shape))
    pidx = lambda s, j: (lambda b, pt: (l, pt[j * n + b * ns + s], 0, 0))
    page = lambda s, j: pl.BlockSpec((None, None, 512, PAGE_SIZE), pidx(s, j))
    lpage = lambda s, j: pl.BlockSpec((None, None, FOX_HEADS, PAGE_SIZE), pidx(s, j))
    scr = lambda r: pltpu.VMEM((r, n_pages, PAGE_SIZE), F32)
    every = [(s, j) for s in range(ns) for j in range(n_pages)]
    return pl.pallas_call(
        partial(_fox_step_kernel, n_pages=n_pages),
        grid_spec=pltpu.PrefetchScalarGridSpec(
            num_scalar_prefetch=1,
            grid=(n // ns,),
            in_specs=[page(s, j) for s, j in every] + [lpage(s, j) for s, j in every]
                     + [res(256), res(512), res(128), c2((PAGE_SIZE, PAGE_SIZE)), c2((PAGE_SIZE, PAGE_SIZE)),
                        c2((FOX_HEADS * n_pages, FOX_HEADS * n_pages))],
            out_specs=pl.BlockSpec((n, 256), lambda b, pt: (0, 0)),
            scratch_shapes=[scr(FOX_HEADS), scr(SUBLANES), scr(SUBLANES)] * ns),
        out_shape=jax.ShapeDtypeStruct((n, 256), F32),
        compiler_params=_cparams(("arbitrary",)),
        name="fox_step",
    )(pt, *([kv_t] * (ns * n_pages)), *([lf_t] * (ns * n_pages)), fq, foxr, flog, cn['usuf'], cn['ones128'],
      cn['lsu'])


def _hgrn_step_kernel(q_ref, f_ref, v_ref, g_ref, lb_ref, go_ref, s_ref, o_ref, so_ref, t_ref):
    off = pl.multiple_of((pl.program_id(0) % 2) * HG_DK, HG_DK)

    def head_t(ref):
        t_ref[...] = ref[...].T
        return t_ref[pl.ds(off, HG_DK), :]

    q, hf, v, hgate = head_t(q_ref), head_t(f_ref), head_t(v_ref), head_t(g_ref)
    lb = lb_ref[...]
    f = lb + (1.0 - lb) * _sigmoid(hf)
    sn = f[:, None, :] * s_ref[...] + (1.0 - f)[:, None, :] * v[None, :, :]
    so_ref[...] = sn
    o = jnp.sum(q[:, None, :] * sn, axis=0)
    o = o * lax.rsqrt(jnp.mean(o * o, axis=0, keepdims=True) + EPS) * go_ref[...]
    o_ref[...] = o * _silu(hgate)


def hgrn_step(hg, state_t, l, lw):
    n = hg.shape[0]
    col = lambda base: pl.BlockSpec((n, LANES), lambda h: (0, base + h // 2))
    return pl.pallas_call(
        _hgrn_step_kernel,
        grid=(HG_HEADS,),
        in_specs=[col(0), col(2), col(4), col(6),
                  pl.BlockSpec((None, HG_DK, 1), lambda h: (h, 0, 0)),
                  pl.BlockSpec((HG_DV, 1), lambda h: (0, 0)),
                  pl.BlockSpec((None, None, HG_DK, HG_DV, n), lambda h: (l, h, 0, 0, 0))],
        out_specs=[pl.BlockSpec((HG_DV, n), lambda h: (h, 0)),
                   pl.BlockSpec((None, HG_DK, HG_DV, n), lambda h: (h, 0, 0, 0))],
        out_shape=[jax.ShapeDtypeStruct((HG_HEADS * HG_DV, n), F32),
                   jax.ShapeDtypeStruct((HG_HEADS, HG_DK, HG_DV, n), F32)],
        scratch_shapes=[pltpu.VMEM((LANES, n), F32)],
        compiler_params=_cparams(("arbitrary",)),
        name="hgrn_step",
    )(hg, hg, hg, hg, lw['lb_t'], lw['go_t'], state_t)


def _t2(x):
    return jnp.concatenate([x[:, 0:LANES].T, x[:, LANES:2 * LANES].T], axis=0)


def _s5_step_kernel(u_ref, h_ref, bre_ref, bim_ref, ar_ref, ai_ref, cre_ref, cim_ref, d_ref, gw_ref, gb_ref,
                    o_ref, ho_ref):
    u = u_ref[...]
    ut = _t2(u).astype(BF16)
    h0r, h0i = h_ref[:, 0, :], h_ref[:, 1, :]
    ar, ai = ar_ref[...], ai_ref[...]
    hr = ar * h0r - ai * h0i + _dot(bre_ref[...], ut)
    hi = ar * h0i + ai * h0r + _dot(bim_ref[...], ut)
    ho_ref[:, 0, :] = hr
    ho_ref[:, 1, :] = hi
    yt = _dot(cre_ref[...], hr.astype(BF16)) - _dot(cim_ref[...], hi.astype(BF16))
    y = jnp.concatenate([yt[0:LANES, :].T, yt[LANES:2 * LANES, :].T], axis=1) + d_ref[...] * u
    z = _gelu_tanh(y)
    o_ref[...] = z * _sigmoid(_dot(z.astype(BF16), gw_ref[...]) + gb_ref[...])


def s5_step(u, state_t, l, lw):
    n = u.shape[0]
    c2 = lambda shape: pl.BlockSpec(shape, lambda i: (0,) * len(shape))
    return pl.pallas_call(
        _s5_step_kernel,
        grid=(1,),
        in_specs=[c2((n, MIX_W)), pl.BlockSpec((None, S5_LANES, 2, n), lambda i: (l, 0, 0, 0)),
                  c2((S5_LANES, MIX_W)), c2((S5_LANES, MIX_W)), c2((S5_LANES, 1)), c2((S5_LANES, 1)),
                  c2((MIX_W, S5_LANES)), c2((MIX_W, S5_LANES)), c2((1, MIX_W)), c2((MIX_W, MIX_W)), c2((1, MIX_W))],
        out_specs=[c2((n, MIX_W)), c2((S5_LANES, 2, n))],
        out_shape=[jax.ShapeDtypeStruct((n, MIX_W), F32), jax.ShapeDtypeStruct((S5_LANES, 2, n), F32)],
        compiler_params=_cparams(("arbitrary",)),
        name="s5_step",
    )(u, state_t, lw['s5_bre_t'], lw['s5_bim_t'], lw['s5_abr_c'], lw['s5_abi_c'], lw['s5_cre_t'], lw['s5_cim_t'],
      lw['s5_d'], lw['s5_gw'], lw['s5_gb'])


def _rope_tables(pos, reps):
    half = HEAD_DIM // 2
    inv = ROPE_THETA ** (-jnp.arange(half, dtype=F32) / half)
    ang = pos.astype(F32)[:, None] * inv[None, :]
    cos, sin = jnp.cos(ang), jnp.sin(ang)
    return (jnp.tile(jnp.concatenate([cos, cos], axis=1), (1, reps)),
            jnp.tile(jnp.concatenate([-sin, sin], axis=1), (1, reps)))


def make_consts(l_prompt, n_sample, past_len):
    cn = {}
    seg = np.arange(256) // HEAD_DIM
    cn['bd256'] = jnp.asarray(seg[:, None] == seg[None, :], BF16)
    cn['eye64'] = jnp.asarray(np.eye(HG_DK), BF16)
    cst, msk = _hgrn_constants()
    cn['hg_cst'] = jnp.asarray(np.concatenate([cst] * 3, axis=1), BF16)
    cn['hg_msk'] = jnp.asarray(msk, F32)
    tc = min(512, l_prompt)
    cn['tri'] = jnp.asarray(np.tril(np.ones((tc, tc))), BF16)
    n16 = l_prompt // CMP_STRIDE
    ci = np.arange(n16)[:, None]
    sj = np.arange(LANES)[None, :]
    cover = ((ci * CMP_STRIDE < sj * SEL_BLOCK + SEL_BLOCK) & (ci * CMP_STRIDE + CMP_LEN - 1 >= sj * SEL_BLOCK)
             & (ci < n16 - 1) & (sj < -(-l_prompt // SEL_BLOCK)))
    cn['cover'] = jnp.asarray(cover, BF16)
    cn['expand'] = jnp.asarray(np.arange(LANES)[:, None] == (np.arange(l_prompt)[None, :] // SEL_BLOCK), BF16)
    cn['cos_p'], cn['sin_p'] = _rope_tables(jnp.arange(l_prompt), 4)
    cn['cos_s'], cn['sin_s'] = _rope_tables(jnp.full((n_sample,), past_len), 4)
    n16s = past_len // CMP_STRIDE
    ci = np.arange(n16s)[:, None]
    cover_s = ((ci * CMP_STRIDE < sj * SEL_BLOCK + SEL_BLOCK) & (ci * CMP_STRIDE + CMP_LEN - 1 >= sj * SEL_BLOCK)
               & (ci < n16s - 1) & (sj < -(-(past_len + 1) // SEL_BLOCK)))
    cn['cover_s'] = jnp.asarray(cover_s, BF16)
    n_pages = past_len // PAGE_SIZE
    bpp = PAGE_SIZE // SEL_BLOCK
    cn['amask'] = jnp.asarray(np.arange(LANES)[None, :] // bpp == np.arange(n_pages)[:, None], F32)
    cn['ex'] = jnp.asarray(np.arange(LANES)[:, None] % bpp == np.arange(PAGE_SIZE)[None, :] // SEL_BLOCK, BF16)
    cn['eye128'] = jnp.asarray(np.eye(LANES), BF16)
    tp = np.arange(PAGE_SIZE)
    cn['usuf'] = jnp.asarray(tp[:, None] > tp[None, :], BF16)
    cn['ones128'] = jnp.ones((PAGE_SIZE, PAGE_SIZE), BF16)
    jp = np.arange(n_pages)
    cn['lsu'] = jnp.asarray(np.kron(np.eye(FOX_HEADS), jp[None, :] > jp[:, None]), BF16)
    return cn


def prep_layer(l, P, lb_all):
    lw = {}
    w = P['w_in'][l]
    pts = np.concatenate([[0], np.cumsum(IN_SIZES)])
    seg = lambda i: w[:, pts[i]:pts[i + 1]]
    z = lambda n: jnp.zeros((D_MODEL, n), w.dtype)
    lw['w_in'] = jnp.concatenate([seg(0), seg(1), seg(2), z(LANES - IN_SIZES[2])] + [seg(i) for i in range(3, 12)]
                                 + [z(LANES - IN_SIZES[11])], axis=1).astype(BF16)
    lw['wg'] = seg(12).astype(BF16)
    lw['attn_norm'] = P['attn_norm'][l].reshape(1, D_MODEL)
    lw['gq'] = jnp.tile(P['nsa_q_norm'][l], 4).reshape(1, 256)
    kn = P['nsa_k_norm'][l]
    lw['gk'] = jnp.concatenate([jnp.tile(kn[1], 4), jnp.tile(kn[2], 2)]).reshape(1, 384)
    lw['gc'] = kn[0].reshape(1, HEAD_DIM)
    lw['gfq'] = jnp.tile(P['fox_q_norm'][l], 4).reshape(1, 256)
    lw['gfk'] = jnp.tile(P['fox_k_norm'][l], 4).reshape(1, 256)
    lw['fbias'] = jnp.pad(P['fox_f_bias'][l], (0, LANES - FOX_HEADS)).reshape(1, LANES)
    wk = P['nsa_cmp_wk'][l].reshape(2, CMP_STRIDE, HEAD_DIM, HEAD_DIM)
    wv = P['nsa_cmp_wv'][l].reshape(2, CMP_STRIDE, HEAD_DIM, HEAD_DIM)
    we = jnp.zeros((CMP_STRIDE, 4, HEAD_DIM, 4, HEAD_DIM), F32)
    we = we.at[:, 0, :, 0, :].set(wk[0]).at[:, 1, :, 1, :].set(wv[0])
    we = we.at[:, 0, :, 2, :].set(wk[1]).at[:, 1, :, 3, :].set(wv[1])
    lw['we'] = we.reshape(CMP_STRIDE * 256, 256).astype(BF16)
    lw['wcmp'] = we.reshape(CMP_STRIDE, 256, 256)[:, 0:2 * HEAD_DIM, :].astype(BF16)
    pe = P['nsa_cmp_pe'][l].reshape(1, CMP_LEN * HEAD_DIM)
    pe_hi = pe.astype(BF16)
    pe_lo = (pe - pe_hi.astype(F32)).astype(BF16)
    lw['pe8'] = jnp.concatenate([pe_hi, pe_lo, jnp.zeros((SUBLANES - 2, CMP_LEN * HEAD_DIM), BF16)], axis=0)
    lw['wkv'] = jnp.concatenate([P['nsa_cmp_wk'][l], P['nsa_cmp_wv'][l]], axis=1).astype(BF16)
    lw['lb'] = lb_all[l].reshape(1, 256)
    lw['go'] = jnp.tile(P['hgrn_o_norm'][l], 4).reshape(1, 256)
    lw['lb_t'] = lb_all[l].reshape(HG_HEADS, HG_DK, 1)
    lw['go_t'] = P['hgrn_o_norm'][l].reshape(HG_DV, 1)
    dt = jnp.exp(P['s5_log_dt'][l].astype(F32))[:, None]
    ar, ai = P['s5_a_re'][l].astype(F32), P['s5_a_im'][l].astype(F32)
    mag = jnp.exp(dt * ar)
    abr, abi = mag * jnp.cos(dt * ai), mag * jnp.sin(dt * ai)
    den = ar * ar + ai * ai
    zr = ((abr - 1.0) * ar + abi * ai) / den
    zi = (abi * ar - (abr - 1.0) * ai) / den
    br_, bi_ = P['s5_b_re'][l].astype(F32), P['s5_b_im'][l].astype(F32)
    bbr = zr[..., None] * br_ - zi[..., None] * bi_
    bbi = zr[..., None] * bi_ + zi[..., None] * br_
    eye_g = jnp.eye(S5_GROUPS, dtype=F32)
    lw['s5_bre'] = jnp.einsum('gpc,gh->gchp', bbr, eye_g).reshape(MIX_W, S5_LANES).astype(BF16)
    lw['s5_bim'] = jnp.einsum('gpc,gh->gchp', bbi, eye_g).reshape(MIX_W, S5_LANES).astype(BF16)
    lw['s5_cre'] = jnp.einsum('gcp,gh->gphc', P['s5_c_re'][l].astype(F32), eye_g).reshape(S5_LANES, MIX_W).astype(BF16)
    lw['s5_cim'] = jnp.einsum('gcp,gh->gphc', P['s5_c_im'][l].astype(F32), eye_g).reshape(S5_LANES, MIX_W).astype(BF16)
    abr_f, abi_f = abr.reshape(1, S5_LANES), abi.reshape(1, S5_LANES)
    lw['s5_abr_c'], lw['s5_abi_c'] = abr.reshape(S5_LANES, 1), abi.reshape(S5_LANES, 1)
    for k in ('s5_bre', 's5_bim', 's5_cre', 's5_cim'):
        lw[k + '_t'] = lw[k].T
    lw['s5_ar8'] = jnp.tile(abr_f, (SUBLANES, 1))
    lw['s5_ai8'] = jnp.concatenate([jnp.tile(-abi_f, (S5_BATCH, 1)), jnp.tile(abi_f, (S5_BATCH, 1))], axis=0)
    lw['s5_d'] = P['s5_d'][l].reshape(1, MIX_W)
    lw['s5_gw'] = P['s5_glu_w'][l].astype(BF16)
    lw['s5_gb'] = P['s5_glu_b'][l].reshape(1, MIX_W)
    lw['wb'] = P['w_branch'][l].astype(BF16)
    lw['wo'] = P['w_out'][l].astype(BF16)
    lw['ffn_g'] = P['ffn_norm'][l].reshape(1, D_MODEL)
    lw['ffn_wu'] = P['ffn_w_up'][l].astype(BF16)
    lw['ffn_cw'] = P['ffn_conv_w'][l]
    lw['ffn_cb'] = P['ffn_conv_b'][l].reshape(1, D_FF)
    lw['ffn_wd'] = P['ffn_w_down'][l].astype(BF16)
    return lw


def prompt_layer(x, lw, cn, b, l):
    r3 = lambda a: a.reshape(b, l, a.shape[-1])
    (hb, qn, qr, rows, win, ng, hg, su, fq, foxr, flog, rows_b, win_b, fox_b, rows_t, win_t, fox_t, flog_t
     ) = in_proj(x, lw, cn['bd256'], cn['cos_p'], cn['sin_p'], 256, b)
    pparts = matmul(rows.reshape(b * l // CMP_STRIDE, CMP_STRIDE * 256), lw['we']).reshape(b, l // CMP_STRIDE, 256)
    oc, mask = nsa_cmp_prompt(r3(qn), pparts, lw, cn['cover'])
    o_nsa = nsa_flash_prompt(r3(qr), r3(rows_b), r3(win_b), mask, cn['expand'], r3(ng), oc)
    c = cumsum_time(r3(flog), cn['tri'])
    c_t = jnp.swapaxes(c[:, :, 0:SUBLANES], 1, 2)
    o_fox = fox_flash_prompt(r3(fq), r3(fox_b), c, c_t)
    o_h, hg_new = hgrn_prompt(r3(hg), lw, cn)
    o_s5, s5h = s5_prompt(r3(su), lw)
    flat = lambda a: a.reshape(b * l, a.shape[-1])
    x1 = merge(x, hb, (flat(o_nsa), flat(o_h), flat(o_s5), flat(o_fox)), lw, 256)
    x2, conv_new = ffn_prompt(x1, lw, b, l)
    nw = min(WINDOW, l)
    s5_new = jnp.moveaxis(s5h.reshape(2, S5_BATCH, S5_GROUPS, S5_STATE), 0, -1)
    return (x2,) + _kv_outputs(rows_t, win_t[:, :, l - nw:], fox_t, flog_t) + (hg_new, s5_new, conv_new)


def _kv_outputs(rows_t, win_t, fox_t, flog_t):
    b, _, l = rows_t.shape
    return (jnp.transpose(rows_t.reshape(b, 4, HEAD_DIM, l), (0, 3, 1, 2)),
            jnp.transpose(win_t.reshape(b, 2, HEAD_DIM, win_t.shape[-1]), (0, 3, 1, 2)),
            jnp.transpose(fox_t.reshape(b, 2, FOX_HEADS, HEAD_DIM, l), (0, 4, 1, 2, 3)),
            jnp.transpose(flog_t[:, 0:FOX_HEADS, :], (0, 2, 1)))


def sample_layer(x, lw, cn, l, views, pt):
    n = x.shape[0]
    kv_t, win_t, fkv_t, flf_t, hg_t, s5_t, conv = views
    (hb, qn, qr, rows, win, ng, hg, su, fq, foxr, flog, _, _, _, rows_t, _, fox_t, flog_t
     ) = in_proj(x, lw, cn['bd256'], cn['cos_s'], cn['sin_s'], n, 1)
    o_nsa, nwin_t = nsa_step(pt, kv_t, win_t, l, qn, qr, rows, win, ng, lw, cn)
    o_fox = fox_step(pt, fkv_t, flf_t, l, fq, foxr, flog, cn)
    oh_t, hgn_t = hgrn_step(hg, hg_t, l, lw)
    o_s5, s5n_t = s5_step(su, s5_t, l, lw)
    x1 = merge(x, hb, (o_nsa, oh_t.T, o_s5, o_fox), lw, n)
    x2, conv_new = ffn_step(x1, conv, l, lw)
    wb = nwin_t.shape[-1]
    new_win = jnp.transpose(nwin_t.reshape(n, 2, HEAD_DIM, wb), (0, 3, 1, 2))
    hg_new = jnp.transpose(hgn_t, (3, 0, 1, 2))
    s5_new = jnp.transpose(s5n_t.reshape(S5_GROUPS, S5_STATE, 2, n), (3, 0, 1, 2))
    rows_o, _, fox_o, flog_o = _kv_outputs(rows_t, rows_t[:, 0:2 * HEAD_DIM], fox_t, flog_t)
    seq_first = lambda a: jnp.swapaxes(a, 0, 1)
    return (x2, seq_first(rows_o), new_win, seq_first(fox_o), seq_first(flog_o), hg_new, s5_new, conv_new)


def sample_views(caches, page_table):
    cache_nsa_kv, cache_nsa_win, cache_fox_kv, cache_fox_logf, state_hgrn, state_s5, state_conv = caches
    d, n_pool = cache_nsa_kv.shape[0:2]
    n = cache_nsa_win.shape[1]
    kv_t = jnp.transpose(cache_nsa_kv, (0, 1, 3, 4, 2)).reshape(d, n_pool, 4 * HEAD_DIM, PAGE_SIZE)
    win_t = jnp.transpose(cache_nsa_win, (0, 1, 3, 4, 2)).reshape(d, n, 2 * HEAD_DIM, cache_nsa_win.shape[2])
    fkv_t = jnp.transpose(cache_fox_kv, (0, 1, 3, 4, 5, 2)).reshape(d, n_pool, 2 * FOX_HEADS * HEAD_DIM, PAGE_SIZE)
    flf_t = jnp.transpose(cache_fox_logf, (0, 1, 3, 2))
    hg_t = jnp.transpose(state_hgrn, (0, 2, 3, 4, 1))
    s5_t = jnp.transpose(state_s5, (0, 2, 3, 4, 1)).reshape(d, S5_LANES, 2, n)
    pt = jnp.transpose(page_table).reshape(-1)
    return (kv_t, win_t, fkv_t, flf_t, hg_t, s5_t, state_conv), pt


PARAM_NAMES =('attn_norm', 'w_in', 'nsa_q_norm', 'nsa_k_norm', 'nsa_cmp_pe', 'nsa_cmp_wk', 'nsa_cmp_wv',
               'hgrn_o_norm', 's5_a_re', 's5_a_im', 's5_log_dt', 's5_b_re', 's5_b_im', 's5_c_re', 's5_c_im',
               's5_d', 's5_glu_w', 's5_glu_b', 'fox_q_norm', 'fox_k_norm', 'fox_f_bias', 'w_branch', 'w_out',
               'ffn_norm', 'ffn_w_up', 'ffn_conv_w', 'ffn_conv_b', 'ffn_w_down')


def trunk(x_prompt, x_sample, caches, page_table, hgrn_lb_logits, P):
    depth = P['w_in'].shape[0]
    bp, lp, _ = x_prompt.shape
    ns = x_sample.shape[0]
    past_len = page_table.shape[1] * PAGE_SIZE
    sm = jax.nn.softmax(hgrn_lb_logits.astype(F32), axis=0)
    lb_all = jnp.cumsum(sm, axis=0) - sm[0]
    cn = make_consts(lp, ns, past_len)
    yp = x_prompt.reshape(bp * lp, D_MODEL)
    ys = x_sample.reshape(ns, D_MODEL)
    outs_p, outs_s = [], []
    views, pt = sample_views(caches, page_table)
    for l in range(depth):
        lw = prep_layer(l, P, lb_all)
        res_p = prompt_layer(yp, lw, cn, bp, lp)
        res_s = sample_layer(ys, lw, cn, l, views, pt)
        yp, ys = res_p[0], res_s[0]
        outs_p.append(res_p[1:])
        outs_s.append(res_s[1:])
    sp = [jnp.stack([o[i] for o in outs_p], axis=0) for i in range(7)]
    ss = [jnp.stack([o[i] for o in outs_s], axis=0) for i in range(7)]
    nsa_kv_p, nsa_win_p, fox_kv_p, fox_logf_p, hgrn_p, s5_p, conv_p = sp
    nsa_kv_s, nsa_win_s, fox_kv_s, fox_logf_s, hgrn_s, s5_s, conv_s = ss
    return (yp.reshape(bp, lp, D_MODEL), ys.reshape(ns, 1, D_MODEL), nsa_kv_p, nsa_kv_s, nsa_win_p, nsa_win_s,
            fox_kv_p, fox_kv_s, fox_logf_p, fox_logf_s, hgrn_p, hgrn_s, s5_p, s5_s, conv_p, conv_s)


def kernel(x_prompt, x_sample, cache_nsa_kv, cache_nsa_win, cache_fox_kv, cache_fox_logf, state_hgrn, state_s5, state_ffn_conv, page_table, attn_norm, w_in, nsa_q_norm, nsa_k_norm, nsa_cmp_pe, nsa_cmp_wk, nsa_cmp_wv, hgrn_lb_logits, hgrn_o_norm, s5_a_re, s5_a_im, s5_log_dt, s5_b_re, s5_b_im, s5_c_re, s5_c_im, s5_d, s5_glu_w, s5_glu_b, fox_q_norm, fox_k_norm, fox_f_bias, w_branch, w_out, ffn_norm, ffn_w_up, ffn_conv_w, ffn_conv_b, ffn_w_down):
    P = dict(attn_norm=attn_norm, w_in=w_in, nsa_q_norm=nsa_q_norm, nsa_k_norm=nsa_k_norm,
             nsa_cmp_pe=nsa_cmp_pe, nsa_cmp_wk=nsa_cmp_wk, nsa_cmp_wv=nsa_cmp_wv, hgrn_o_norm=hgrn_o_norm,
             s5_a_re=s5_a_re, s5_a_im=s5_a_im, s5_log_dt=s5_log_dt, s5_b_re=s5_b_re, s5_b_im=s5_b_im,
             s5_c_re=s5_c_re, s5_c_im=s5_c_im, s5_d=s5_d, s5_glu_w=s5_glu_w, s5_glu_b=s5_glu_b,
             fox_q_norm=fox_q_norm, fox_k_norm=fox_k_norm, fox_f_bias=fox_f_bias, w_branch=w_branch,
             w_out=w_out, ffn_norm=ffn_norm, ffn_w_up=ffn_w_up, ffn_conv_w=ffn_conv_w,
             ffn_conv_b=ffn_conv_b, ffn_w_down=ffn_w_down)
    caches = (cache_nsa_kv, cache_nsa_win, cache_fox_kv, cache_fox_logf, state_hgrn, state_s5, state_ffn_conv)
    return trunk(x_prompt, x_sample, caches, page_table, hgrn_lb_logits, P)
```

```python
import math
from functools import partial

import numpy as np
import jax
import jax.numpy as jnp
from jax import lax
from jax.experimental import pallas as pl
from jax.experimental.pallas import tpu as pltpu

D_MODEL = 1024
DEPTH = 4
PAGE_SIZE = 128
HEAD_DIM = 64
N_BRANCH = 4
MIX_W = D_MODEL // N_BRANCH
NSA_HEADS = MIX_W // HEAD_DIM
CMP_LEN = 32
CMP_STRIDE = 16
SEL_BLOCK = 64
N_SEL = 16
WINDOW = 512
SEL_FORCE = 1000.0
HG_HEADS = 4
HG_DK = MIX_W // HG_HEADS
HG_DV = MIX_W // HG_HEADS
HG_CHUNK = 64
S5_GROUP = 16
S5_GROUPS = MIX_W // S5_GROUP
S5_STATE = 64
S5_LANES = S5_GROUPS * S5_STATE
FOX_HEADS = MIX_W // HEAD_DIM
Q_BLOCK = 128
D_FF = ((8 * D_MODEL // 3 + 127) // 128) * 128
CONV_W = 3
ROPE_THETA = 10000.0
EPS = 1e-6
SCALE = HEAD_DIM ** -0.5
LOG2E = math.log2(math.e)
NEG = -1e30

IN_SIZES = (NSA_HEADS * HEAD_DIM, 6 * HEAD_DIM, 3 * NSA_HEADS,
            HG_HEADS * HG_DK, HG_HEADS * HG_DK, HG_HEADS * HG_DV, HG_HEADS * HG_DV,
            MIX_W,
            FOX_HEADS * HEAD_DIM, FOX_HEADS * HEAD_DIM, FOX_HEADS * HEAD_DIM, FOX_HEADS,
            N_BRANCH * D_MODEL)

LANES = 128
SUBLANES = 8
VMEM_LIMIT_BYTES = 56 * 1024 * 1024

F32 = jnp.float32
BF16 = jnp.bfloat16

C_NQ, C_NKV, C_NG, C_HG, C_SU, C_FQ, C_FK, C_FV, C_FF, C_END = 0, 256, 640, 768, 1792, 2048, 2304, 2560, 2816, 2944


def _cparams(sem):
    return pltpu.CompilerParams(dimension_semantics=sem, vmem_limit_bytes=VMEM_LIMIT_BYTES)


def _split2(x):
    hi = x.astype(BF16)
    lo = (x - hi.astype(F32)).astype(BF16)
    return hi, lo


def _split3(x):
    hi = x.astype(BF16)
    r = x - hi.astype(F32)
    mid = r.astype(BF16)
    lo = (r - mid.astype(F32)).astype(BF16)
    return hi, mid, lo


def _dot(a, b):
    return jnp.dot(a, b, preferred_element_type=F32)


def _dot_nt(a, b):
    return lax.dot_general(a, b, (((1,), (1,)), ((), ())), preferred_element_type=F32)


def _dot_tn(a, b):
    return lax.dot_general(a, b, (((0,), (0,)), ((), ())), preferred_element_type=F32)


def _dot_exact_lhs(a01, x):
    hi, mid, lo = _split3(x)
    return _dot(a01, hi) + _dot(a01, mid) + _dot(a01, lo)


def _dot_exact_rhs(x, b01):
    hi, mid, lo = _split3(x)
    return _dot(hi, b01) + _dot(mid, b01) + _dot(lo, b01)


def _seg_rms(x, bd, gain):
    ms = _dot_exact_rhs(x * x, bd) * (1.0 / HEAD_DIM)
    return x * lax.rsqrt(ms + EPS) * gain


def _rope(x, cos, sin_signed):
    w = x.shape[-1]
    half = HEAD_DIM // 2
    lane = lax.broadcasted_iota(jnp.int32, x.shape, 1)
    first = (lane % HEAD_DIM) < half
    swapped = jnp.where(first, pltpu.roll(x, w - half, 1), pltpu.roll(x, half, 1))
    return x * cos + swapped * sin_signed


def _log_sigmoid(x):
    return jnp.minimum(x, 0.0) - jnp.log(1.0 + jnp.exp(-jnp.abs(x)))


def _sigmoid(x):
    return 1.0 / (1.0 + jnp.exp(-x))


def _silu(x):
    return x * _sigmoid(x)


def _mm_kernel(x_ref, w_ref, o_ref):
    o_ref[...] = _dot(x_ref[...].astype(BF16), w_ref[...])


def matmul(x, w, tm=256):
    m, k = x.shape
    n = w.shape[1]
    tm = min(tm, m)
    return pl.pallas_call(
        _mm_kernel,
        grid=(m // tm,),
        in_specs=[pl.BlockSpec((tm, k), lambda i: (i, 0)),
                  pl.BlockSpec((k, n), lambda i: (0, 0))],
        out_specs=pl.BlockSpec((tm, n), lambda i: (i, 0)),
        out_shape=jax.ShapeDtypeStruct((m, n), F32),
        compiler_params=_cparams(("arbitrary",)),
        name="matmul",
    )(x, w.astype(BF16))


def _in_proj_kernel(x_ref, g_ref, w_ref, cos_ref, sin_ref, bd_ref, gq_ref, gk_ref, gfq_ref, gfk_ref, fb_ref,
                    h_ref, qn_ref, qr_ref, rows_ref, win_ref, ng_ref, hg_ref, su_ref, fq_ref, fox_ref, flog_ref,
                    rows_b_ref, win_b_ref, fox_b_ref, rows_t_ref, win_t_ref, fox_t_ref, flog_t_ref):
    x = x_ref[...]
    h = x * lax.rsqrt(jnp.mean(x * x, axis=-1, keepdims=True) + EPS) * g_ref[...]
    hb = h.astype(BF16)
    h_ref[...] = hb
    p = _dot(hb, w_ref[...])
    bd = bd_ref[...]
    cos, sin = cos_ref[...], sin_ref[...]
    qn = _seg_rms(p[:, C_NQ:C_NKV], bd, gq_ref[...])
    qn_ref[...] = qn
    qr_ref[...] = _rope(qn, cos, sin)
    raw = p[:, C_NKV:C_NKV + 256]
    lane = lax.broadcasted_iota(jnp.int32, raw.shape, 1)
    nr = _rope(_seg_rms(raw, bd, gk_ref[:, 0:256]), cos, sin)
    rows = jnp.where((lane >= 2 * HEAD_DIM) & (lane < 3 * HEAD_DIM), nr, raw)
    raww = p[:, C_NKV + 256:C_NG]
    nw = _rope(_seg_rms(raww, bd[0:128, 0:128], gk_ref[:, 256:384]), cos[:, 0:128], sin[:, 0:128])
    lane_w = lax.broadcasted_iota(jnp.int32, raww.shape, 1)
    win = jnp.where(lane_w < HEAD_DIM, nw, raww)
    ng_ref[...] = p[:, C_NG:C_HG]
    hg_ref[...] = p[:, C_HG:C_SU]
    su_ref[...] = p[:, C_SU:C_FQ]
    fq_ref[...] = _seg_rms(p[:, C_FQ:C_FK], bd, gfq_ref[...])
    fk = _seg_rms(p[:, C_FK:C_FV], bd, gfk_ref[...])
    fv = p[:, C_FV:C_FF]
    flog = _log_sigmoid(p[:, C_FF:C_END] + fb_ref[...])
    rows_ref[...] = rows
    win_ref[...] = win
    fox_ref[:, 0:256] = fk
    fox_ref[:, 256:512] = fv
    flog_ref[...] = flog
    rows_b_ref[...] = rows.astype(BF16)
    win_b_ref[...] = win.astype(BF16)
    fox_b_ref[:, 0:256] = fk.astype(BF16)
    fox_b_ref[:, 256:512] = fv.astype(BF16)
    rows_t_ref[...] = rows.T
    win_t_ref[...] = win.T
    fox_t_ref[0:256, :] = fk.T
    fox_t_ref[256:512, :] = fv.T
    flog_t_ref[...] = flog.T[0:SUBLANES, :]


def in_proj(x, lw, bd256, cos_t, sin_t, tm, nb):
    t = x.shape[0]
    seg = t // nb
    nt = seg // tm
    row = lambda i: (i, 0)
    const = lambda i: (0, 0)
    tab = lambda i: (i % nt, 0)
    tmap = lambda i: (i // nt, 0, i % nt)
    widths = (256, 256, 256, 128, 128, 1024, 256, 256, 512, 128)
    bwidths = (256, 128, 512)
    twidths = (256, 128, 512, SUBLANES)
    out_shape = ([jax.ShapeDtypeStruct((t, D_MODEL), BF16)] + [jax.ShapeDtypeStruct((t, w), F32) for w in widths]
                 + [jax.ShapeDtypeStruct((t, w), BF16) for w in bwidths]
                 + [jax.ShapeDtypeStruct((nb, w, seg), F32) for w in twidths])
    out_specs = ([pl.BlockSpec((tm, D_MODEL), row)] + [pl.BlockSpec((tm, w), row) for w in widths]
                 + [pl.BlockSpec((tm, w), row) for w in bwidths]
                 + [pl.BlockSpec((None, w, tm), tmap) for w in twidths])
    return pl.pallas_call(
        _in_proj_kernel,
        grid=(t // tm,),
        in_specs=[pl.BlockSpec((tm, D_MODEL), row), pl.BlockSpec((1, D_MODEL), const),
                  pl.BlockSpec((D_MODEL, C_END), const), pl.BlockSpec((tm, 256), tab), pl.BlockSpec((tm, 256), tab),
                  pl.BlockSpec((256, 256), const), pl.BlockSpec((1, 256), const), pl.BlockSpec((1, 384), const),
                  pl.BlockSpec((1, 256), const), pl.BlockSpec((1, 256), const), pl.BlockSpec((1, 128), const)],
        out_specs=out_specs,
        out_shape=out_shape,
        compiler_params=_cparams(("arbitrary",)),
        name="in_proj",
    )(x, lw['attn_norm'], lw['w_in'], cos_t, sin_t, bd256, lw['gq'], lw['gk'], lw['gfq'], lw['gfk'], lw['fbias'])


def _topk_mask(score, k, n_cand):
    r = -(-n_cand // SUBLANES) * SUBLANES
    s = score.T[0:r, :]
    idx_f = lax.broadcasted_iota(jnp.int32, s.shape, 0).astype(F32)
    sel = jnp.zeros(s.shape, F32)
    for _ in range(k):
        m = jnp.max(s, axis=0, keepdims=True)
        first = jnp.min(jnp.where(s == m, idx_f, float(LANES)), axis=0, keepdims=True)
        pick = idx_f == first
        sel = jnp.where(pick, 1.0, sel)
        s = jnp.where(pick, NEG, s)
    if r < LANES:
        sel = jnp.concatenate([sel, jnp.zeros((LANES - r, s.shape[1]), F32)], axis=0)
    return sel.T


def _topk_rank(score_row, k, eye, ones):
    n = score_row.shape[1]
    row_b = jnp.broadcast_to(score_row, (n, n))
    col_b = _dot_exact_rhs(eye.astype(F32) * row_b, ones)
    ii = lax.broadcasted_iota(jnp.int32, (n, n), 0)
    jj = lax.broadcasted_iota(jnp.int32, (n, n), 1)
    beats = (row_b > col_b) | ((row_b == col_b) & (jj < ii))
    return _dot(jnp.where(beats, 1.0, 0.0).astype(BF16), ones) < k


def _pe_term(pe_ref, wkv_ref):
    pew = _dot(pe_ref[...], wkv_ref[...])
    return pew[0:1, :] + pew[1:2, :]


def _cmp_tokens(p, pew, gain_c, n_c):
    n16 = p.shape[0]
    nxt = pltpu.roll(p, n16 - 1, 0)
    kc = p[:, 0:64] + nxt[:, 128:192] + pew[0:1, 0:64]
    vc = p[:, 64:128] + nxt[:, 192:256] + pew[0:1, 64:128]
    kc = kc * lax.rsqrt(jnp.mean(kc * kc, axis=-1, keepdims=True) + EPS) * gain_c
    return kc, vc


def _cmp_attend(q, kc, vc, valid):
    qh, ql = _split2(q)
    kh, kl = _split2(kc)
    s = _dot_nt(qh, kh) + _dot_nt(ql, kh) + _dot_nt(qh, kl)
    s = jnp.where(valid, s, NEG)
    m = jnp.max(s, axis=1, keepdims=True)
    e = jnp.where(valid, jnp.exp(s - m), 0.0)
    pc = e / jnp.maximum(jnp.sum(e, axis=1, keepdims=True), 1e-30)
    return pc, _dot(pc.astype(BF16), vc.astype(BF16))


def _cmp_kernel(q_ref, p_ref, pe_ref, wkv_ref, gc_ref, cover_ref, oc_ref, mask_ref, kc_ref, vc_ref, *, tq, n_c):
    qi = pl.program_id(1)

    @pl.when(qi == 0)
    def _():
        kc, vc = _cmp_tokens(p_ref[...], _pe_term(pe_ref, wkv_ref), gc_ref[...], n_c)
        kc_ref[...] = kc
        vc_ref[...] = vc

    kc, vc = kc_ref[...], vc_ref[...]
    n16 = kc.shape[0]
    pos = qi * tq + lax.broadcasted_iota(jnp.int32, (tq, n16), 0)
    ci = lax.broadcasted_iota(jnp.int32, (tq, n16), 1)
    valid = (ci * CMP_STRIDE + CMP_LEN - 1 <= pos) & (ci < n_c)
    q = q_ref[...]
    pcs = jnp.zeros((tq, n16), F32)
    outs = []
    for h in range(NSA_HEADS):
        pc, o = _cmp_attend(q[:, h * HEAD_DIM:(h + 1) * HEAD_DIM] * SCALE, kc, vc, valid)
        pcs = pcs + pc
        outs.append(o)
    oc_ref[...] = jnp.concatenate(outs, axis=1)
    imp = _dot_exact_rhs(pcs, cover_ref[...])
    j = lax.broadcasted_iota(jnp.int32, (tq, LANES), 1)
    cur = (qi * tq + lax.broadcasted_iota(jnp.int32, (tq, LANES), 0)) // SEL_BLOCK
    forced = (j == 0) | (j == cur) | (j == cur - 1)
    score = jnp.where(j > cur, NEG, imp + jnp.where(forced, SEL_FORCE, 0.0))
    mask_ref[...] = _topk_mask(score, N_SEL, (n_c + 1) * CMP_STRIDE // SEL_BLOCK)


def nsa_cmp_prompt(qn, pparts, lw, cover, tq=512):
    b, l, _ = qn.shape
    tq = min(tq, l)
    n16 = l // CMP_STRIDE
    n_c = n16 - 1
    return pl.pallas_call(
        partial(_cmp_kernel, tq=tq, n_c=n_c),
        grid=(b, l // tq),
        in_specs=[pl.BlockSpec((None, tq, 256), lambda bi, qi: (bi, qi, 0)),
                  pl.BlockSpec((None, n16, 256), lambda bi, qi: (bi, 0, 0)),
                  pl.BlockSpec((SUBLANES, CMP_LEN * HEAD_DIM), lambda bi, qi: (0, 0)),
                  pl.BlockSpec((CMP_LEN * HEAD_DIM, 128), lambda bi, qi: (0, 0)),
                  pl.BlockSpec((1, HEAD_DIM), lambda bi, qi: (0, 0)),
                  pl.BlockSpec((n16, LANES), lambda bi, qi: (0, 0))],
        out_specs=[pl.BlockSpec((None, tq, 256), lambda bi, qi: (bi, qi, 0)),
                   pl.BlockSpec((None, tq, LANES), lambda bi, qi: (bi, qi, 0))],
        out_shape=[jax.ShapeDtypeStruct((b, l, 256), F32), jax.ShapeDtypeStruct((b, l, LANES), F32)],
        scratch_shapes=[pltpu.VMEM((n16, HEAD_DIM), F32), pltpu.VMEM((n16, HEAD_DIM), F32)],
        compiler_params=_cparams(("arbitrary", "arbitrary")),
        name="nsa_cmp",
    )(qn, pparts, lw['pe8'], lw['wkv'], lw['gc'], cover)


def _online_update(s, v, m_ref, l_ref, acc_ref, h):
    m_old = m_ref[h]
    m_new = jnp.maximum(m_old, jnp.max(s, axis=1, keepdims=True))
    alpha = jnp.exp2(m_old - m_new)
    p = jnp.exp2(s - jnp.concatenate([m_new] * (s.shape[1] // LANES), axis=1))
    l_ref[h] = alpha * l_ref[h] + jnp.sum(p, axis=1, keepdims=True)
    acc_ref[h] = alpha[:, 0:HEAD_DIM] * acc_ref[h] + _dot(p.astype(BF16), v)
    m_ref[h] = m_new


def _flash_init(m_refs, l_refs, acc_refs):
    for r in m_refs:
        r[...] = jnp.full(r.shape, NEG, F32)
    for r in l_refs + acc_refs:
        r[...] = jnp.zeros(r.shape, F32)


def _nsa_flash_kernel(q_ref, rows_ref, win_ref, mask_ref, e_ref, ng_ref, oc_ref, o_ref,
                      qs_ref, ms_ref, ls_ref, as_ref, mw_ref, lw_ref, aw_ref, *, tq, tk):
    qi, kj = pl.program_id(1), pl.program_id(2)
    nk = pl.num_programs(2)

    @pl.when(kj == 0)
    def _():
        _flash_init([ms_ref, mw_ref], [ls_ref, lw_ref], [as_ref, aw_ref])
        q = q_ref[...]
        for h in range(NSA_HEADS):
            qs_ref[h] = (q[:, h * HEAD_DIM:(h + 1) * HEAD_DIM] * (SCALE * LOG2E)).astype(BF16)

    q_lo = qi * tq
    k_lo = kj * tk
    qpos = q_lo + lax.broadcasted_iota(jnp.int32, (tq, tk), 0)
    kpos = k_lo + lax.broadcasted_iota(jnp.int32, (tq, tk), 1)
    causal = kpos <= qpos

    @pl.when(k_lo <= q_lo + tq - 1)
    def _():
        rows = rows_ref[...]
        ks = rows[:, 2 * HEAD_DIM:3 * HEAD_DIM]
        vs = rows[:, 3 * HEAD_DIM:4 * HEAD_DIM]
        sel = _dot(mask_ref[...].astype(BF16), e_ref[...]) > 0.5
        bias = jnp.where(sel & causal, 0.0, NEG)
        for h in range(NSA_HEADS):
            _online_update(_dot_nt(qs_ref[h], ks) + bias, vs, ms_ref, ls_ref, as_ref, h)

    @pl.when((k_lo <= q_lo + tq - 1) & (k_lo + tk - 1 > q_lo - WINDOW))
    def _():
        win = win_ref[...]
        kw = win[:, 0:HEAD_DIM]
        vw = win[:, HEAD_DIM:2 * HEAD_DIM]
        bias = jnp.where(causal & (qpos - kpos < WINDOW), 0.0, NEG)
        for h in range(NSA_HEADS):
            _online_update(_dot_nt(qs_ref[h], kw) + bias, vw, mw_ref, lw_ref, aw_ref, h)

    @pl.when(kj == nk - 1)
    def _():
        gs = _sigmoid(ng_ref[...])
        oc = oc_ref[...]
        outs = []
        for h in range(NSA_HEADS):
            o_s = as_ref[h] / ls_ref[h][:, 0:HEAD_DIM]
            o_w = aw_ref[h] / lw_ref[h][:, 0:HEAD_DIM]
            outs.append(gs[:, 3 * h:3 * h + 1] * oc[:, h * HEAD_DIM:(h + 1) * HEAD_DIM]
                        + gs[:, 3 * h + 1:3 * h + 2] * o_s + gs[:, 3 * h + 2:3 * h + 3] * o_w)
        o_ref[...] = jnp.concatenate(outs, axis=1)


def nsa_flash_prompt(qr, rows, win, mask, expand, ng, oc, tq=512, tk=512):
    b, l, _ = qr.shape
    tq, tk = min(tq, l), min(tk, l)
    nq, nk = l // tq, l // tk

    def kv_idx(bi, qi, kj):
        return (bi, jnp.minimum(kj, (qi * tq + tq - 1) // tk), 0)

    def win_idx(bi, qi, kj):
        hi = (qi * tq + tq - 1) // tk
        lo = jnp.maximum((qi * tq - WINDOW + 1) // tk, 0)
        return (bi, jnp.clip(kj, lo, hi), 0)

    qmap = lambda bi, qi, kj: (bi, qi, 0)
    hshape = (NSA_HEADS, tq, LANES)
    return pl.pallas_call(
        partial(_nsa_flash_kernel, tq=tq, tk=tk),
        grid=(b, nq, nk),
        in_specs=[pl.BlockSpec((None, tq, 256), qmap),
                  pl.BlockSpec((None, tk, 256), kv_idx),
                  pl.BlockSpec((None, tk, 128), win_idx),
                  pl.BlockSpec((None, tq, LANES), qmap),
                  pl.BlockSpec((LANES, tk), lambda bi, qi, kj: (0, jnp.minimum(kj, (qi * tq + tq - 1) // tk))),
                  pl.BlockSpec((None, tq, LANES), qmap),
                  pl.BlockSpec((None, tq, 256), qmap)],
        out_specs=pl.BlockSpec((None, tq, 256), qmap),
        out_shape=jax.ShapeDtypeStruct((b, l, 256), F32),
        scratch_shapes=[pltpu.VMEM((NSA_HEADS, tq, HEAD_DIM), BF16),
                        pltpu.VMEM(hshape, F32), pltpu.VMEM(hshape, F32), pltpu.VMEM((NSA_HEADS, tq, HEAD_DIM), F32),
                        pltpu.VMEM(hshape, F32), pltpu.VMEM(hshape, F32), pltpu.VMEM((NSA_HEADS, tq, HEAD_DIM), F32)],
        compiler_params=_cparams(("arbitrary", "arbitrary", "arbitrary")),
        name="nsa_flash",
    )(qr, rows, win, mask, expand, ng, oc)


def _cumsum_kernel(x_ref, tri_ref, o_ref, carry_ref):
    @pl.when(pl.program_id(1) == 0)
    def _():
        carry_ref[...] = jnp.zeros(carry_ref.shape, F32)

    c = _dot_exact_lhs(tri_ref[...], x_ref[...]) + carry_ref[0:1, :]
    o_ref[...] = c
    carry_ref[0:1, :] = c[c.shape[0] - 1:c.shape[0], :]


def cumsum_time(x, tri):
    b, l, w = x.shape
    tc = tri.shape[0]
    return pl.pallas_call(
        _cumsum_kernel,
        grid=(b, l // tc),
        in_specs=[pl.BlockSpec((None, tc, w), lambda bi, i: (bi, i, 0)),
                  pl.BlockSpec((tc, tc), lambda bi, i: (0, 0))],
        out_specs=pl.BlockSpec((None, tc, w), lambda bi, i: (bi, i, 0)),
        out_shape=jax.ShapeDtypeStruct((b, l, w), F32),
        scratch_shapes=[pltpu.VMEM((SUBLANES, w), F32)],
        compiler_params=_cparams(("arbitrary", "arbitrary")),
        name="cumsum_time",
    )(x, tri)


def _fox_flash_kernel(q_ref, kv_ref, cq_ref, ck_ref, o_ref, qs_ref, cqb_ref, m_ref, l_ref, acc_ref, *, tq, tk):
    qi, kj = pl.program_id(1), pl.program_id(2)
    nk = pl.num_programs(2)

    @pl.when(kj == 0)
    def _():
        _flash_init([m_ref], [l_ref], [acc_ref])
        q = q_ref[...]
        cq = cq_ref[...]
        for h in range(FOX_HEADS):
            qs_ref[h] = (q[:, h * HEAD_DIM:(h + 1) * HEAD_DIM] * (SCALE * LOG2E)).astype(BF16)
            cqb_ref[h] = jnp.broadcast_to(cq[:, h:h + 1] * LOG2E, (tq, LANES))

    q_lo = qi * tq
    k_lo = kj * tk

    def step(on_diagonal):
        kv = kv_ref[...]
        ck = ck_ref[...] * LOG2E
        if on_diagonal:
            qpos = q_lo + lax.broadcasted_iota(jnp.int32, (tq, tk), 0)
            kpos = k_lo + lax.broadcasted_iota(jnp.int32, (tq, tk), 1)
            causal = jnp.where(kpos <= qpos, 0.0, NEG)
        for h in range(FOX_HEADS):
            k = kv[:, h * HEAD_DIM:(h + 1) * HEAD_DIM]
            v = kv[:, 256 + h * HEAD_DIM:256 + (h + 1) * HEAD_DIM]
            bias = jnp.concatenate([cqb_ref[h]] * (tk // LANES), axis=1) - ck[h:h + 1, :]
            if on_diagonal:
                bias = bias + causal
            _online_update(_dot_nt(qs_ref[h], k) + bias, v, m_ref, l_ref, acc_ref, h)

    @pl.when(k_lo + tk - 1 <= q_lo)
    def _():
        step(False)

    @pl.when((k_lo <= q_lo + tq - 1) & (k_lo + tk - 1 > q_lo))
    def _():
        step(True)

    @pl.when(kj == nk - 1)
    def _():
        o_ref[...] = jnp.concatenate(
            [acc_ref[h] / l_ref[h][:, 0:HEAD_DIM] for h in range(FOX_HEADS)], axis=1)


def fox_flash_prompt(fq, fox_rows, c, c_t, tq=512, tk=512):
    b, l, _ = fq.shape
    tq, tk = min(tq, l), min(tk, l)
    nq, nk = l // tq, l // tk
    qmap = lambda bi, qi, kj: (bi, qi, 0)
    last = lambda qi: (qi * tq + tq - 1) // tk
    hshape = (FOX_HEADS, tq, LANES)
    return pl.pallas_call(
        partial(_fox_flash_kernel, tq=tq, tk=tk),
        grid=(b, nq, nk),
        in_specs=[pl.BlockSpec((None, tq, 256), qmap),
                  pl.BlockSpec((None, tk, 512), lambda bi, qi, kj: (bi, jnp.minimum(kj, last(qi)), 0)),
                  pl.BlockSpec((None, tq, LANES), qmap),
                  pl.BlockSpec((None, SUBLANES, tk), lambda bi, qi, kj: (bi, 0, jnp.minimum(kj, last(qi))))],
        out_specs=pl.BlockSpec((None, tq, 256), qmap),
        out_shape=jax.ShapeDtypeStruct((b, l, 256), F32),
        scratch_shapes=[pltpu.VMEM((FOX_HEADS, tq, HEAD_DIM), BF16), pltpu.VMEM(hshape, F32),
                        pltpu.VMEM(hshape, F32), pltpu.VMEM(hshape, F32), pltpu.VMEM((FOX_HEADS, tq, HEAD_DIM), F32)],
        compiler_params=_cparams(("arbitrary", "arbitrary", "arbitrary")),
        name="fox_flash",
    )(fq, fox_rows, c, c_t)


HG_LEVELS = int(math.log2(HG_CHUNK))


def _hgrn_constants():
    c = HG_CHUNK
    t = np.arange(c)[:, None]
    u = np.arange(c)[None, :]
    mats = [(u <= t), (u > t)]
    masks = []
    for lv in range(HG_LEVELS):
        bit = (t >> lv) & 1
        p_t = (t >> lv) << lv
        mats.append((bit == 1) & (u >= p_t) & (u <= t))
    for lv in range(HG_LEVELS):
        bit = (t >> lv) & 1
        p_s = ((t >> lv) + 1) << lv
        mats.append((bit == 0) & (u > t) & (u <= p_s - 1))
    for lv in range(HG_LEVELS):
        masks.append(((t >> (lv + 1)) == (u >> (lv + 1))) & (((t >> lv) & 1) == 1) & (((u >> lv) & 1) == 0))
    masks.append(t == u)
    cst = np.concatenate([m.astype(np.float32) for m in mats], axis=0)
    msk = np.stack([m.astype(np.float32) for m in masks], axis=0)
    return cst, msk


def _hgrn_kernel(hg_ref, lb_ref, cst_ref, msk_ref, eye_ref, bd_ref, go_ref, o_ref, st_ref, s_ref, *, tt):
    i = pl.program_id(1)
    c = HG_CHUNK

    @pl.when(i == 0)
    def _():
        s_ref[...] = jnp.zeros(s_ref.shape, F32)

    lb = lb_ref[...]
    nbb = hg_ref.shape[0]
    bd = bd_ref[...] > 0
    heads_bd = lambda x: jnp.where(bd, jnp.concatenate([x] * HG_HEADS, axis=0), jnp.zeros((), x.dtype))
    msk = [jnp.concatenate([msk_ref[lv]] * HG_HEADS, axis=1) for lv in range(HG_LEVELS + 1)]
    for ch, bb in [(ch, bb) for ch in range(tt // c) for bb in range(nbb)]:
        r = ch * c
        hq = hg_ref[bb, r:r + c, 0:256]
        hf = hg_ref[bb, r:r + c, 256:512]
        hv = hg_ref[bb, r:r + c, 512:768]
        hgate = hg_ref[bb, r:r + c, 768:1024]
        f = lb + (1.0 - lb) * _sigmoid(hf)
        lf = jnp.log(f)
        kk = 1.0 - f
        e = _dot(cst_ref[...], jnp.concatenate(_split3(lf), axis=0))
        eb = jnp.exp(e[0:c])
        esu = jnp.exp(e[c:2 * c])
        ebl = eb[c - 1:c, :]
        qf = [(hq * jnp.exp(e[(2 + lv) * c:(3 + lv) * c])).astype(BF16) for lv in range(HG_LEVELS)]
        kg = [(kk * jnp.exp(e[(2 + HG_LEVELS + lv) * c:(3 + HG_LEVELS + lv) * c])).astype(BF16)
              for lv in range(HG_LEVELS)]
        qb, kb, vb = hq.astype(BF16), kk.astype(BF16), hv.astype(BF16)
        qe, ke = (hq * eb).astype(BF16), (kk * esu).astype(BF16)
        att = msk[HG_LEVELS] * _dot_nt(qb, heads_bd(kb))
        for lv in range(HG_LEVELS):
            att = att + msk[lv] * _dot_nt(qf[lv], heads_bd(kg[lv]))
        st = s_ref[bb]
        o = _dot(att.astype(BF16), heads_bd(vb)) + _dot_nt(qe, st.astype(BF16))
        s_ref[bb] = st * ebl + jnp.where(bd, _dot_tn(vb, ke), 0.0)
        o_ref[bb, r:r + c, :] = _seg_rms(o, bd_ref[...], go_ref[...]) * _silu(hgate)

    @pl.when(i == pl.num_programs(1) - 1)
    def _():
        for bb in range(nbb):
            for h in range(HG_HEADS):
                blk = s_ref[bb, h * HG_DV:(h + 1) * HG_DV, h * HG_DK:(h + 1) * HG_DK]
                hi, mid, lo = _split3(blk)
                eye = eye_ref[...]
                st_ref[bb, h] = _dot_nt(eye, hi) + _dot_nt(eye, mid) + _dot_nt(eye, lo)


def hgrn_prompt(hg, lw, cn, tt=512, nbb=2):
    b, l, _ = hg.shape
    tt = min(tt, l)
    const2 = lambda bi, i: (0, 0)
    return pl.pallas_call(
        partial(_hgrn_kernel, tt=tt),
        grid=(b // nbb, l // tt),
        in_specs=[pl.BlockSpec((nbb, tt, 1024), lambda bi, i: (bi, i, 0)),
                  pl.BlockSpec((1, 256), const2),
                  pl.BlockSpec(cn['hg_cst'].shape, const2),
                  pl.BlockSpec(cn['hg_msk'].shape, lambda bi, i: (0, 0, 0)),
                  pl.BlockSpec((HG_DK, HG_DK), const2),
                  pl.BlockSpec((256, 256), const2),
                  pl.BlockSpec((1, 256), const2)],
        out_specs=[pl.BlockSpec((nbb, tt, 256), lambda bi, i: (bi, i, 0)),
                   pl.BlockSpec((nbb, HG_HEADS, HG_DK, HG_DV), lambda bi, i: (bi, 0, 0, 0))],
        out_shape=[jax.ShapeDtypeStruct((b, l, 256), F32),
                   jax.ShapeDtypeStruct((b, HG_HEADS, HG_DK, HG_DV), F32)],
        scratch_shapes=[pltpu.VMEM((nbb, HG_HEADS * HG_DV, HG_HEADS * HG_DK), F32)],
        compiler_params=_cparams(("arbitrary", "arbitrary")),
        name="hgrn",
    )(hg, lw['lb'], cn['hg_cst'], cn['hg_msk'], cn['eye64'], cn['bd256'], lw['go'])


S5_BATCH = 4


def _gelu_tanh(x):
    return 0.5 * x * (1.0 + jnp.tanh(math.sqrt(2.0 / math.pi) * (x + 0.044715 * (x * x * x))))


def _s5_kernel(u_ref, bre_ref, bim_ref, ar_ref, ai_ref, cre_ref, cim_ref, d_ref, gw_ref, gb_ref,
               o_ref, hout_ref, xs_ref, h_ref, *, tc):
    i = pl.program_id(0)

    @pl.when(i == 0)
    def _():
        h_ref[...] = jnp.zeros(h_ref.shape, F32)

    nlt = S5_LANES // LANES
    for b in range(S5_BATCH):
        ub = u_ref[b].astype(BF16)
        xre, xim = _dot(ub, bre_ref[...]), _dot(ub, bim_ref[...])
        for j in range(nlt):
            xs_ref[j, pl.ds(b, tc, stride=SUBLANES), :] = xre[:, j * LANES:(j + 1) * LANES]
            xs_ref[j, pl.ds(S5_BATCH + b, tc, stride=SUBLANES), :] = xim[:, j * LANES:(j + 1) * LANES]
    ar = [ar_ref[:, j * LANES:(j + 1) * LANES] for j in range(nlt)]
    ai = [ai_ref[:, j * LANES:(j + 1) * LANES] for j in range(nlt)]

    def body(t, hcur):
        off = pl.multiple_of(t * SUBLANES, SUBLANES)
        hnew = []
        for j in range(nlt):
            hj = ar[j] * hcur[j] + ai[j] * pltpu.roll(hcur[j], S5_BATCH, 0) + xs_ref[j, pl.ds(off, SUBLANES), :]
            xs_ref[j, pl.ds(off, SUBLANES), :] = hj
            hnew.append(hj)
        return tuple(hnew)

    h0 = tuple(h_ref[:, j * LANES:(j + 1) * LANES] for j in range(nlt))
    hfin = lax.fori_loop(0, tc, body, h0, unroll=8)
    for j in range(nlt):
        h_ref[:, j * LANES:(j + 1) * LANES] = hfin[j]
    for b in range(S5_BATCH):
        hre = jnp.concatenate([xs_ref[j, pl.ds(b, tc, stride=SUBLANES), :] for j in range(nlt)], axis=1)
        him = jnp.concatenate([xs_ref[j, pl.ds(S5_BATCH + b, tc, stride=SUBLANES), :] for j in range(nlt)], axis=1)
        y = _dot(hre.astype(BF16), cre_ref[...]) - _dot(him.astype(BF16), cim_ref[...]) + d_ref[...] * u_ref[b]
        z = _gelu_tanh(y)
        o_ref[b] = z * _sigmoid(_dot(z.astype(BF16), gw_ref[...]) + gb_ref[...])

    @pl.when(i == pl.num_programs(0) - 1)
    def _():
        hout_ref[...] = h_ref[...]


def s5_prompt(u, lw, tc=256):
    b, l, _ = u.shape
    tc = min(tc, l)
    assert b == S5_BATCH
    const = lambda i: (0, 0)
    return pl.pallas_call(
        partial(_s5_kernel, tc=tc),
        grid=(l // tc,),
        in_specs=[pl.BlockSpec((b, tc, 256), lambda i: (0, i, 0)),
                  pl.BlockSpec((256, S5_LANES), const), pl.BlockSpec((256, S5_LANES), const),
                  pl.BlockSpec((SUBLANES, S5_LANES), const), pl.BlockSpec((SUBLANES, S5_LANES), const),
                  pl.BlockSpec((S5_LANES, 256), const), pl.BlockSpec((S5_LANES, 256), const),
                  pl.BlockSpec((1, 256), const), pl.BlockSpec((256, 256), const), pl.BlockSpec((1, 256), const)],
        out_specs=[pl.BlockSpec((b, tc, 256), lambda i: (0, i, 0)),
                   pl.BlockSpec((SUBLANES, S5_LANES), const)],
        out_shape=[jax.ShapeDtypeStruct((b, l, 256), F32), jax.ShapeDtypeStruct((SUBLANES, S5_LANES), F32)],
        scratch_shapes=[pltpu.VMEM((S5_LANES // LANES, tc * SUBLANES, LANES), F32),
                        pltpu.VMEM((SUBLANES, S5_LANES), F32)],
        compiler_params=_cparams(("arbitrary",)),
        name="s5_scan",
    )(u, lw['s5_bre'], lw['s5_bim'], lw['s5_ar8'], lw['s5_ai8'], lw['s5_cre'], lw['s5_cim'],
      lw['s5_d'], lw['s5_gw'], lw['s5_gb'])


def _merge_kernel(x_ref, h_ref, b0_ref, b1_ref, b2_ref, b3_ref, wg_ref, wb_ref, wo_ref, o_ref):
    hb = h_ref[...]
    merged = None
    for n, br in enumerate((b0_ref, b1_ref, b2_ref, b3_ref)):
        g = _sigmoid(_dot(hb, wg_ref[:, n * D_MODEL:(n + 1) * D_MODEL]))
        term = g * _dot(br[...].astype(BF16), wb_ref[n])
        merged = term if merged is None else merged + term
    o_ref[...] = x_ref[...] + _dot(merged.astype(BF16), wo_ref[...])


def merge(x, hb, branches, lw, tm):
    t = x.shape[0]
    row = lambda i: (i, 0)
    const = lambda i: (0, 0)
    return pl.pallas_call(
        _merge_kernel,
        grid=(t // tm,),
        in_specs=[pl.BlockSpec((tm, D_MODEL), row), pl.BlockSpec((tm, D_MODEL), row)]
                 + [pl.BlockSpec((tm, MIX_W), row)] * N_BRANCH
                 + [pl.BlockSpec((D_MODEL, N_BRANCH * D_MODEL), const),
                    pl.BlockSpec((N_BRANCH, MIX_W, D_MODEL), lambda i: (0, 0, 0)),
                    pl.BlockSpec((D_MODEL, D_MODEL), const)],
        out_specs=pl.BlockSpec((tm, D_MODEL), row),
        out_shape=jax.ShapeDtypeStruct((t, D_MODEL), F32),
        compiler_params=_cparams(("arbitrary",)),
        name="merge",
    )(x, hb, *branches, lw['wg'], lw['wb'], lw['wo'])


def _ffn_core(x, g, wu, cw, cb, wd, a_m2, a_m1):
    h = x * lax.rsqrt(jnp.mean(x * x, axis=-1, keepdims=True) + EPS) * g
    up = _dot(h.astype(BF16), wu)
    a, bb = up[:, 0:D_FF], up[:, D_FF:2 * D_FF]
    ac = cb + a_m2(a) * cw[0:1, :] + a_m1(a) * cw[1:2, :] + a * cw[2:3, :]
    return a, x + _dot((_silu(ac) * bb).astype(BF16), wd)


def _ffn_seq_kernel(x_ref, g_ref, wu_ref, cw_ref, cb_ref, wd_ref, o_ref, cs_ref, prev_ref, *, tm):
    @pl.when(pl.program_id(1) == 0)
    def _():
        prev_ref[...] = jnp.zeros(prev_ref.shape, F32)

    prev = prev_ref[...]
    row = lax.broadcasted_iota(jnp.int32, (tm, D_FF), 0)
    p1, p2 = prev[SUBLANES - 1:SUBLANES, :], prev[SUBLANES - 2:SUBLANES - 1, :]
    a_m1 = lambda a: jnp.where(row == 0, p1, pltpu.roll(a, 1, 0))
    a_m2 = lambda a: jnp.where(row == 0, p2, jnp.where(row == 1, p1, pltpu.roll(a, 2, 0)))
    a, o = _ffn_core(x_ref[...], g_ref[...], wu_ref[...], cw_ref[...], cb_ref[...], wd_ref[...], a_m2, a_m1)
    o_ref[...] = o
    tail = a[tm - SUBLANES:tm, :]
    prev_ref[...] = tail
    cs_ref[...] = tail[SUBLANES - (CONV_W - 1):SUBLANES, :]


def _ffn_step_kernel(x_ref, cs_ref, g_ref, wu_ref, cw_ref, cb_ref, wd_ref, o_ref, cso_ref):
    b0, b1 = cs_ref[:, 0, :], cs_ref[:, 1, :]
    a, o = _ffn_core(x_ref[...], g_ref[...], wu_ref[...], cw_ref[...], cb_ref[...], wd_ref[...],
                     lambda a: b0, lambda a: b1)
    o_ref[...] = o
    cso_ref[:, 0, :] = b1
    cso_ref[:, 1, :] = a


def _ffn_weight_specs(const):
    return [pl.BlockSpec((1, D_MODEL), const), pl.BlockSpec((D_MODEL, 2 * D_FF), const),
            pl.BlockSpec((CONV_W, D_FF), const), pl.BlockSpec((1, D_FF), const), pl.BlockSpec((D_FF, D_MODEL), const)]


def ffn_prompt(x, lw, b, l, tm=256):
    nt = l // tm
    const = lambda bi, i: (0, 0)
    row = lambda bi, i: (bi * nt + i, 0)
    return pl.pallas_call(
        partial(_ffn_seq_kernel, tm=tm),
        grid=(b, nt),
        in_specs=[pl.BlockSpec((tm, D_MODEL), row)] + _ffn_weight_specs(const),
        out_specs=[pl.BlockSpec((tm, D_MODEL), row),
                   pl.BlockSpec((None, CONV_W - 1, D_FF), lambda bi, i: (bi, 0, 0))],
        out_shape=[jax.ShapeDtypeStruct((b * l, D_MODEL), F32), jax.ShapeDtypeStruct((b, CONV_W - 1, D_FF), F32)],
        scratch_shapes=[pltpu.VMEM((SUBLANES, D_FF), F32)],
        compiler_params=_cparams(("arbitrary", "arbitrary")),
        name="ffn_seq",
    )(x, lw['ffn_g'], lw['ffn_wu'], lw['ffn_cw'], lw['ffn_cb'], lw['ffn_wd'])


def ffn_step(x, conv_state, l, lw):
    n = x.shape[0]
    const = lambda i: (0, 0)
    return pl.pallas_call(
        _ffn_step_kernel,
        grid=(1,),
        in_specs=[pl.BlockSpec((n, D_MODEL), const),
                  pl.BlockSpec((None, n, CONV_W - 1, D_FF), lambda i: (l, 0, 0, 0))] + _ffn_weight_specs(const),
        out_specs=[pl.BlockSpec((n, D_MODEL), const), pl.BlockSpec((n, CONV_W - 1, D_FF), lambda i: (0, 0, 0))],
        out_shape=[jax.ShapeDtypeStruct((n, D_MODEL), F32), jax.ShapeDtypeStruct((n, CONV_W - 1, D_FF), F32)],
        compiler_params=_cparams(("arbitrary",)),
        name="ffn_step",
    )(x, conv_state, lw['ffn_g'], lw['ffn_wu'], lw['ffn_cw'], lw['ffn_cb'], lw['ffn_wd'])


def _heads_to_rows(row):
    return jnp.concatenate([row[:, h * HEAD_DIM:(h + 1) * HEAD_DIM] for h in range(NSA_HEADS)]
                           + [jnp.zeros((SUBLANES - NSA_HEADS, HEAD_DIM), F32)], axis=0)


def _rows_to_heads(o8):
    return jnp.concatenate([o8[h:h + 1, :] for h in range(NSA_HEADS)], axis=1)


def _col8(cells):
    return jnp.concatenate(list(cells) + [jnp.zeros((SUBLANES - len(cells), 1), F32)], axis=0)


def _all_max(x):
    return jnp.max(jnp.max(x, axis=1, keepdims=True), axis=0, keepdims=True)


def _all_sum(x):
    return jnp.sum(jnp.sum(x, axis=1, keepdims=True), axis=0, keepdims=True)


def _paged_softmax(s, valid, s_new):
    m = jnp.maximum(_all_max(jnp.where(valid, s, NEG)), s_new)
    e = jnp.where(valid, jnp.exp(s - m), 0.0)
    e_new = jnp.exp(s_new - m)
    inv = 1.0 / (_all_sum(e) + e_new)
    return e * inv, e_new * inv


SEQ_PER_STEP = 2


def _nsa_step_kernel(pt_ref, *refs, n_pages, wb, past_len):
    ns = SEQ_PER_STEP
    o = ns * n_pages
    winc_ref = refs[o]
    shared = refs[o + 1:o + 15]
    o_ref, nwin_ref = refs[o + 15], refs[o + 16]
    scr = refs[o + 17:]
    for s in range(ns):
        _nsa_step_one(pl.program_id(0) * ns + s, refs[s * n_pages:(s + 1) * n_pages], winc_ref.at[s], *shared,
                      o_ref, nwin_ref.at[s], *scr[3 * s:3 * s + 3], n_pages=n_pages, wb=wb, past_len=past_len)


def _nsa_step_one(b, kv, winc_ref, qn_ref, qr_ref, rows_ref, win_ref, ng_ref, wcmp_ref, pe_ref, wkv_ref, gc_ref,
                  cover_ref, amask_ref, ex_ref, eye_ref, ones_ref, o_ref, nwin_ref, xc_ref, sl_ref, pl_ref,
                  *, n_pages, wb, past_len):
    row = lambda ref: ref[pl.ds(b, 1), :]
    rnew, wnew = row(rows_ref), row(win_ref)
    gs = _sigmoid(row(ng_ref))
    for j in range(n_pages):
        xc_ref[j * PAGE_SIZE:(j + 1) * PAGE_SIZE, :] = kv[j][0:2 * HEAD_DIM, :].T
    n16 = n_pages * PAGE_SIZE // CMP_STRIDE
    p = jnp.zeros((n16, 256), F32)
    for r in range(CMP_STRIDE):
        p = p + _dot(xc_ref[pl.ds(r, n16, stride=CMP_STRIDE), :].astype(BF16), wcmp_ref[r])
    n_c = n16 - 1
    kc, vc = _cmp_tokens(p, _pe_term(pe_ref, wkv_ref), gc_ref[...], n_c)
    ci = lax.broadcasted_iota(jnp.int32, (SUBLANES, n16), 1)
    ri = lax.broadcasted_iota(jnp.int32, (SUBLANES, n16), 0)
    pc, o_c = _cmp_attend(_heads_to_rows(row(qn_ref)) * SCALE, kc, vc, ci < n_c)
    pcs = jnp.sum(jnp.where(ri < NSA_HEADS, pc, 0.0), axis=0, keepdims=True)
    imp = _dot_exact_rhs(jnp.broadcast_to(pcs, (SUBLANES, n16)), cover_ref[...])
    j_l = lax.broadcasted_iota(jnp.int32, (SUBLANES, LANES), 1)
    cur = past_len // SEL_BLOCK
    forced = (j_l == 0) | (j_l == cur) | (j_l == cur - 1)
    score = jnp.where(j_l > cur, NEG, imp + jnp.where(forced, SEL_FORCE, 0.0))
    sel_b = _topk_rank(score[0:1, :], N_SEL, eye_ref[...], ones_ref[...])
    selx = _dot(amask_ref[...].astype(BF16), jnp.where(sel_b, ex_ref[...], 0.0).astype(BF16)) > 0.5
    qr8 = _heads_to_rows(row(qr_ref)) * SCALE
    qrb = qr8.astype(BF16)
    for j in range(n_pages):
        sl_ref[:, j, :] = _dot(qrb, kv[j][2 * HEAD_DIM:3 * HEAD_DIM, :].astype(BF16))
    s_new = jnp.sum(qr8 * rnew[:, 2 * HEAD_DIM:3 * HEAD_DIM], axis=1, keepdims=True)
    p_new = []
    for h in range(NSA_HEADS):
        ph, pn = _paged_softmax(sl_ref[h], selx, s_new[h:h + 1, :])
        pl_ref[h] = ph
        p_new.append(pn)
    pl_ref[NSA_HEADS:SUBLANES] = jnp.zeros((SUBLANES - NSA_HEADS, n_pages, PAGE_SIZE), F32)
    o_s = _col8(p_new) * rnew[:, 3 * HEAD_DIM:4 * HEAD_DIM]
    for j in range(n_pages):
        o_s = o_s + _dot_nt(pl_ref[:, j, :].astype(BF16), kv[j][3 * HEAD_DIM:4 * HEAD_DIM, :].astype(BF16))
    wc = winc_ref[...]
    s_w = _dot(qrb, wc[0:HEAD_DIM, :].astype(BF16))
    valid_w = lax.broadcasted_iota(jnp.int32, s_w.shape, 1) > wb - WINDOW
    s_wn = jnp.sum(qr8 * wnew[:, 0:HEAD_DIM], axis=1, keepdims=True)
    m_w = jnp.maximum(jnp.max(jnp.where(valid_w, s_w, NEG), axis=1, keepdims=True), s_wn)
    e_w = jnp.where(valid_w, jnp.exp(s_w - m_w), 0.0)
    e_wn = jnp.exp(s_wn - m_w)
    o_w = (_dot_nt(e_w.astype(BF16), wc[HEAD_DIM:2 * HEAD_DIM, :].astype(BF16)) + e_wn * wnew[:, HEAD_DIM:2 * HEAD_DIM]
           ) / (jnp.sum(e_w, axis=1, keepdims=True) + e_wn)
    gate = lambda k: _col8([gs[:, 3 * h + k:3 * h + k + 1] for h in range(NSA_HEADS)])
    o_ref[pl.ds(b, 1), :] = _rows_to_heads(gate(0) * o_c + gate(1) * o_s + gate(2) * o_w)
    wn8 = jnp.broadcast_to(wnew, (SUBLANES, 2 * HEAD_DIM))
    hi, mid, lo = _split3(wn8)
    eye = eye_ref[...]
    col = (_dot_nt(eye, hi) + _dot_nt(eye, mid) + _dot_nt(eye, lo))[:, 0:1]
    lane = lax.broadcasted_iota(jnp.int32, wc.shape, 1)
    nwin_ref[...] = jnp.where(lane == wb - 1, col, pltpu.roll(wc, wb - 1, 1))


def nsa_step(pt, kv_t, win_t, l, qn, qr, rows, win, ng, lw, cn):
    n = qn.shape[0]
    wb = win_t.shape[-1]
    n_pages = pt.shape[0] // n
    past_len = n_pages * PAGE_SIZE
    assert wb == WINDOW
    ns = SEQ_PER_STEP
    res = lambda w: pl.BlockSpec((n, w), lambda b, pt: (0, 0))
    c2 = lambda shape: pl.BlockSpec(shape, lambda b, pt: (0,) * len(shape))
    page = lambda s, j: pl.BlockSpec((None, None, 256, PAGE_SIZE), lambda b, pt: (l, pt[j * n + b * ns + s], 0, 0))
    seqs = range(ns)
    return pl.pallas_call(
        partial(_nsa_step_kernel, n_pages=n_pages, wb=wb, past_len=past_len),
        grid_spec=pltpu.PrefetchScalarGridSpec(
            num_scalar_prefetch=1,
            grid=(n // ns,),
            in_specs=[page(s, j) for s in seqs for j in range(n_pages)]
                     + [pl.BlockSpec((None, ns, 2 * HEAD_DIM, wb), lambda b, pt: (l, b, 0, 0)),
                        res(256), res(256), res(256), res(128), res(128),
                        c2((CMP_STRIDE, 128, 256)), c2((SUBLANES, CMP_LEN * HEAD_DIM)),
                        c2((CMP_LEN * HEAD_DIM, 128)), c2((1, HEAD_DIM)), c2((past_len // CMP_STRIDE, LANES)),
                        c2((n_pages, LANES)), c2((LANES, PAGE_SIZE)), c2((LANES, LANES)), c2((LANES, LANES))],
            out_specs=[pl.BlockSpec((n, 256), lambda b, pt: (0, 0)),
                       pl.BlockSpec((ns, 2 * HEAD_DIM, wb), lambda b, pt: (b, 0, 0))],
            scratch_shapes=[pltpu.VMEM((past_len, 2 * HEAD_DIM), F32),
                            pltpu.VMEM((SUBLANES, n_pages, PAGE_SIZE), F32),
                            pltpu.VMEM((SUBLANES, n_pages, PAGE_SIZE), F32)] * ns),
        out_shape=[jax.ShapeDtypeStruct((n, 256), F32), jax.ShapeDtypeStruct((n, 2 * HEAD_DIM, wb), F32)],
        compiler_params=_cparams(("arbitrary",)),
        name="nsa_step",
    )(pt, *([kv_t] * (ns * n_pages)), win_t, qn, qr, rows, win, ng, lw['wcmp'], lw['pe8'], lw['wkv'],
      lw['gc'], cn['cover_s'], cn['amask'], cn['ex'], cn['eye128'], cn['ones128'])


def _fox_step_kernel(pt_ref, *refs, n_pages):
    ns = SEQ_PER_STEP
    o = 2 * ns * n_pages
    shared = refs[o:o + 7]
    scr = refs[o + 7:]
    for s in range(ns):
        _fox_step_one(pl.program_id(0) * ns + s, refs[s * n_pages:(s + 1) * n_pages],
                      refs[(ns + s) * n_pages:(ns + s + 1) * n_pages], *shared, *scr[3 * s:3 * s + 3],
                      n_pages=n_pages)


def _fox_step_one(b, kv, lf, fq_ref, fox_ref, flog_ref, usuf_ref, ones_ref, lsu_ref, o_ref, lfs_ref, sl_ref, pl_ref,
                  *, n_pages):
    fq = fq_ref[pl.ds(b, 1), :]
    kvnew = fox_ref[pl.ds(b, 1), :]
    fl = flog_ref[pl.ds(b, 1), :]
    ri = lax.broadcasted_iota(jnp.int32, (SUBLANES, 256), 0)
    li = lax.broadcasted_iota(jnp.int32, (SUBLANES, 256), 1)
    bdm = (li // HEAD_DIM) == ri
    qbd = jnp.where(bdm, jnp.broadcast_to(fq * SCALE, (SUBLANES, 256)), 0.0)
    qb = qbd.astype(BF16)
    for j in range(n_pages):
        lfs_ref[:, j, :] = lf[j][...]
        sl_ref[:, j, :] = _dot(qb, kv[j][0:256, :].astype(BF16))
    s_new = jnp.sum(qbd * kvnew[:, 0:256], axis=1, keepdims=True)
    always = jnp.full((n_pages, PAGE_SIZE), True)
    p_new = []
    lf_all = lfs_ref[...].reshape(FOX_HEADS * n_pages, PAGE_SIZE)
    suf = _dot_exact_rhs(lf_all, usuf_ref[...])
    tail = _dot_exact_lhs(lsu_ref[...], _dot_exact_rhs(lf_all, ones_ref[...]))
    bias_all = suf + tail
    for h in range(FOX_HEADS):
        bias = bias_all[h * n_pages:(h + 1) * n_pages, :] + fl[:, h:h + 1]
        ph, pn = _paged_softmax(sl_ref[h] + bias, always, s_new[h:h + 1, :])
        pl_ref[h] = ph
        p_new.append(pn)
    pl_ref[FOX_HEADS:SUBLANES] = jnp.zeros((SUBLANES - FOX_HEADS, n_pages, PAGE_SIZE), F32)
    acc = _col8(p_new) * kvnew[:, 256:512]
    for j in range(n_pages):
        acc = acc + _dot_nt(pl_ref[:, j, :].astype(BF16), kv[j][256:512, :].astype(BF16))
    o_ref[pl.ds(b, 1), :] = jnp.sum(jnp.where(bdm, acc, 0.0), axis=0, keepdims=True)


def fox_step(pt, kv_t, lf_t, l, fq, foxr, flog, cn):
    n = fq.shape[0]
    n_pages = pt.shape[0] // n
    ns = SEQ_PER_STEP
    res = lambda w: pl.BlockSpec((n, w), lambda b, pt: (0, 0))
    c2 = lambda shape: pl.BlockSpec(shape, lambda b, pt: (0,) * len(shape))
    pidx = lambda s, j: (lambda b, pt: (l, pt[j * n + b * ns + s], 0, 0))
    page = lambda s, j: pl.BlockSpec((None, None, 512, PAGE_SIZE), pidx(s, j))
    lpage = lambda s, j: pl.BlockSpec((None, None, FOX_HEADS, PAGE_SIZE), pidx(s, j))
    scr = lambda r: pltpu.VMEM((r, n_pages, PAGE_SIZE), F32)
    every = [(s, j) for s in range(ns) for j in range(n_pages)]
    return pl.pallas_call(
        partial(_fox_step_kernel, n_pages=n_pages),
        grid_spec=pltpu.PrefetchScalarGridSpec(
            num_scalar_prefetch=1,
            grid=(n // ns,),
            in_specs=[page(s, j) for s, j in every] + [lpage(s, j) for s, j in every]
                     + [res(256), res(512), res(128), c2((PAGE_SIZE, PAGE_SIZE)), c2((PAGE_SIZE, PAGE_SIZE)),
                        c2((FOX_HEADS * n_pages, FOX_HEADS * n_pages))],
            out_specs=pl.BlockSpec((n, 256), lambda b, pt: (0, 0)),
            scratch_shapes=[scr(FOX_HEADS), scr(SUBLANES), scr(SUBLANES)] * ns),
        out_shape=jax.ShapeDtypeStruct((n, 256), F32),
        compiler_params=_cparams(("arbitrary",)),
        name="fox_step",
    )(pt, *([kv_t] * (ns * n_pages)), *([lf_t] * (ns * n_pages)), fq, foxr, flog, cn['usuf'], cn['ones128'],
      cn['lsu'])


def _hgrn_step_kernel(q_ref, f_ref, v_ref, g_ref, lb_ref, go_ref, s_ref, o_ref, so_ref, t_ref):
    off = pl.multiple_of((pl.program_id(0) % 2) * HG_DK, HG_DK)

    def head_t(ref):
        t_ref[...] = ref[...].T
        return t_ref[pl.ds(off, HG_DK), :]

    q, hf, v, hgate = head_t(q_ref), head_t(f_ref), head_t(v_ref), head_t(g_ref)
    lb = lb_ref[...]
    f = lb + (1.0 - lb) * _sigmoid(hf)
    sn = f[:, None, :] * s_ref[...] + (1.0 - f)[:, None, :] * v[None, :, :]
    so_ref[...] = sn
    o = jnp.sum(q[:, None, :] * sn, axis=0)
    o = o * lax.rsqrt(jnp.mean(o * o, axis=0, keepdims=True) + EPS) * go_ref[...]
    o_ref[...] = o * _silu(hgate)


def hgrn_step(hg, state_t, l, lw):
    n = hg.shape[0]
    col = lambda base: pl.BlockSpec((n, LANES), lambda h: (0, base + h // 2))
    return pl.pallas_call(
        _hgrn_step_kernel,
        grid=(HG_HEADS,),
        in_specs=[col(0), col(2), col(4), col(6),
                  pl.BlockSpec((None, HG_DK, 1), lambda h: (h, 0, 0)),
                  pl.BlockSpec((HG_DV, 1), lambda h: (0, 0)),
                  pl.BlockSpec((None, None, HG_DK, HG_DV, n), lambda h: (l, h, 0, 0, 0))],
        out_specs=[pl.BlockSpec((HG_DV, n), lambda h: (h, 0)),
                   pl.BlockSpec((None, HG_DK, HG_DV, n), lambda h: (h, 0, 0, 0))],
        out_shape=[jax.ShapeDtypeStruct((HG_HEADS * HG_DV, n), F32),
                   jax.ShapeDtypeStruct((HG_HEADS, HG_DK, HG_DV, n), F32)],
        scratch_shapes=[pltpu.VMEM((LANES, n), F32)],
        compiler_params=_cparams(("arbitrary",)),
        name="hgrn_step",
    )(hg, hg, hg, hg, lw['lb_t'], lw['go_t'], state_t)


def _t2(x):
    return jnp.concatenate([x[:, 0:LANES].T, x[:, LANES:2 * LANES].T], axis=0)


def _s5_step_kernel(u_ref, h_ref, bre_ref, bim_ref, ar_ref, ai_ref, cre_ref, cim_ref, d_ref, gw_ref, gb_ref,
                    o_ref, ho_ref):
    u = u_ref[...]
    ut = _t2(u).astype(BF16)
    h0r, h0i = h_ref[:, 0, :], h_ref[:, 1, :]
    ar, ai = ar_ref[...], ai_ref[...]
    hr = ar * h0r - ai * h0i + _dot(bre_ref[...], ut)
    hi = ar * h0i + ai * h0r + _dot(bim_ref[...], ut)
    ho_ref[:, 0, :] = hr
    ho_ref[:, 1, :] = hi
    yt = _dot(cre_ref[...], hr.astype(BF16)) - _dot(cim_ref[...], hi.astype(BF16))
    y = jnp.concatenate([yt[0:LANES, :].T, yt[LANES:2 * LANES, :].T], axis=1) + d_ref[...] * u
    z = _gelu_tanh(y)
    o_ref[...] = z * _sigmoid(_dot(z.astype(BF16), gw_ref[...]) + gb_ref[...])


def s5_step(u, state_t, l, lw):
    n = u.shape[0]
    c2 = lambda shape: pl.BlockSpec(shape, lambda i: (0,) * len(shape))
    return pl.pallas_call(
        _s5_step_kernel,
        grid=(1,),
        in_specs=[c2((n, MIX_W)), pl.BlockSpec((None, S5_LANES, 2, n), lambda i: (l, 0, 0, 0)),
                  c2((S5_LANES, MIX_W)), c2((S5_LANES, MIX_W)), c2((S5_LANES, 1)), c2((S5_LANES, 1)),
                  c2((MIX_W, S5_LANES)), c2((MIX_W, S5_LANES)), c2((1, MIX_W)), c2((MIX_W, MIX_W)), c2((1, MIX_W))],
        out_specs=[c2((n, MIX_W)), c2((S5_LANES, 2, n))],
        out_shape=[jax.ShapeDtypeStruct((n, MIX_W), F32), jax.ShapeDtypeStruct((S5_LANES, 2, n), F32)],
        compiler_params=_cparams(("arbitrary",)),
        name="s5_step",
    )(u, state_t, lw['s5_bre_t'], lw['s5_bim_t'], lw['s5_abr_c'], lw['s5_abi_c'], lw['s5_cre_t'], lw['s5_cim_t'],
      lw['s5_d'], lw['s5_gw'], lw['s5_gb'])


def _rope_tables(pos, reps):
    half = HEAD_DIM // 2
    inv = ROPE_THETA ** (-jnp.arange(half, dtype=F32) / half)
    ang = pos.astype(F32)[:, None] * inv[None, :]
    cos, sin = jnp.cos(ang), jnp.sin(ang)
    return (jnp.tile(jnp.concatenate([cos, cos], axis=1), (1, reps)),
            jnp.tile(jnp.concatenate([-sin, sin], axis=1), (1, reps)))


def make_consts(l_prompt, n_sample, past_len):
    cn = {}
    seg = np.arange(256) // HEAD_DIM
    cn['bd256'] = jnp.asarray(seg[:, None] == seg[None, :], BF16)
    cn['eye64'] = jnp.asarray(np.eye(HG_DK), BF16)
    cst, msk = _hgrn_constants()
    cn['hg_cst'] = jnp.asarray(np.concatenate([cst] * 3, axis=1), BF16)
    cn['hg_msk'] = jnp.asarray(msk, F32)
    tc = min(512, l_prompt)
    cn['tri'] = jnp.asarray(np.tril(np.ones((tc, tc))), BF16)
    n16 = l_prompt // CMP_STRIDE
    ci = np.arange(n16)[:, None]
    sj = np.arange(LANES)[None, :]
    cover = ((ci * CMP_STRIDE < sj * SEL_BLOCK + SEL_BLOCK) & (ci * CMP_STRIDE + CMP_LEN - 1 >= sj * SEL_BLOCK)
             & (ci < n16 - 1) & (sj < -(-l_prompt // SEL_BLOCK)))
    cn['cover'] = jnp.asarray(cover, BF16)
    cn['expand'] = jnp.asarray(np.arange(LANES)[:, None] == (np.arange(l_prompt)[None, :] // SEL_BLOCK), BF16)
    cn['cos_p'], cn['sin_p'] = _rope_tables(jnp.arange(l_prompt), 4)
    cn['cos_s'], cn['sin_s'] = _rope_tables(jnp.full((n_sample,), past_len), 4)
    n16s = past_len // CMP_STRIDE
    ci = np.arange(n16s)[:, None]
    cover_s = ((ci * CMP_STRIDE < sj * SEL_BLOCK + SEL_BLOCK) & (ci * CMP_STRIDE + CMP_LEN - 1 >= sj * SEL_BLOCK)
               & (ci < n16s - 1) & (sj < -(-(past_len + 1) // SEL_BLOCK)))
    cn['cover_s'] = jnp.asarray(cover_s, BF16)
    n_pages = past_len // PAGE_SIZE
    bpp = PAGE_SIZE // SEL_BLOCK
    cn['amask'] = jnp.asarray(np.arange(LANES)[None, :] // bpp == np.arange(n_pages)[:, None], F32)
    cn['ex'] = jnp.asarray(np.arange(LANES)[:, None] % bpp == np.arange(PAGE_SIZE)[None, :] // SEL_BLOCK, BF16)
    cn['eye128'] = jnp.asarray(np.eye(LANES), BF16)
    tp = np.arange(PAGE_SIZE)
    cn['usuf'] = jnp.asarray(tp[:, None] > tp[None, :], BF16)
    cn['ones128'] = jnp.ones((PAGE_SIZE, PAGE_SIZE), BF16)
    jp = np.arange(n_pages)
    cn['lsu'] = jnp.asarray(np.kron(np.eye(FOX_HEADS), jp[None, :] > jp[:, None]), BF16)
    return cn


def prep_layer(l, P, lb_all):
    lw = {}
    w = P['w_in'][l]
    pts = np.concatenate([[0], np.cumsum(IN_SIZES)])
    seg = lambda i: w[:, pts[i]:pts[i + 1]]
    z = lambda n: jnp.zeros((D_MODEL, n), w.dtype)
    lw['w_in'] = jnp.concatenate([seg(0), seg(1), seg(2), z(LANES - IN_SIZES[2])] + [seg(i) for i in range(3, 12)]
                                 + [z(LANES - IN_SIZES[11])], axis=1).astype(BF16)
    lw['wg'] = seg(12).astype(BF16)
    lw['attn_norm'] = P['attn_norm'][l].reshape(1, D_MODEL)
    lw['gq'] = jnp.tile(P['nsa_q_norm'][l], 4).reshape(1, 256)
    kn = P['nsa_k_norm'][l]
    lw['gk'] = jnp.concatenate([jnp.tile(kn[1], 4), jnp.tile(kn[2], 2)]).reshape(1, 384)
    lw['gc'] = kn[0].reshape(1, HEAD_DIM)
    lw['gfq'] = jnp.tile(P['fox_q_norm'][l], 4).reshape(1, 256)
    lw['gfk'] = jnp.tile(P['fox_k_norm'][l], 4).reshape(1, 256)
    lw['fbias'] = jnp.pad(P['fox_f_bias'][l], (0, LANES - FOX_HEADS)).reshape(1, LANES)
    wk = P['nsa_cmp_wk'][l].reshape(2, CMP_STRIDE, HEAD_DIM, HEAD_DIM)
    wv = P['nsa_cmp_wv'][l].reshape(2, CMP_STRIDE, HEAD_DIM, HEAD_DIM)
    we = jnp.zeros((CMP_STRIDE, 4, HEAD_DIM, 4, HEAD_DIM), F32)
    we = we.at[:, 0, :, 0, :].set(wk[0]).at[:, 1, :, 1, :].set(wv[0])
    we = we.at[:, 0, :, 2, :].set(wk[1]).at[:, 1, :, 3, :].set(wv[1])
    lw['we'] = we.reshape(CMP_STRIDE * 256, 256).astype(BF16)
    lw['wcmp'] = we.reshape(CMP_STRIDE, 256, 256)[:, 0:2 * HEAD_DIM, :].astype(BF16)
    pe = P['nsa_cmp_pe'][l].reshape(1, CMP_LEN * HEAD_DIM)
    pe_hi = pe.astype(BF16)
    pe_lo = (pe - pe_hi.astype(F32)).astype(BF16)
    lw['pe8'] = jnp.concatenate([pe_hi, pe_lo, jnp.zeros((SUBLANES - 2, CMP_LEN * HEAD_DIM), BF16)], axis=0)
    lw['wkv'] = jnp.concatenate([P['nsa_cmp_wk'][l], P['nsa_cmp_wv'][l]], axis=1).astype(BF16)
    lw['lb'] = lb_all[l].reshape(1, 256)
    lw['go'] = jnp.tile(P['hgrn_o_norm'][l], 4).reshape(1, 256)
    lw['lb_t'] = lb_all[l].reshape(HG_HEADS, HG_DK, 1)
    lw['go_t'] = P['hgrn_o_norm'][l].reshape(HG_DV, 1)
    dt = jnp.exp(P['s5_log_dt'][l].astype(F32))[:, None]
    ar, ai = P['s5_a_re'][l].astype(F32), P['s5_a_im'][l].astype(F32)
    mag = jnp.exp(dt * ar)
    abr, abi = mag * jnp.cos(dt * ai), mag * jnp.sin(dt * ai)
    den = ar * ar + ai * ai
    zr = ((abr - 1.0) * ar + abi * ai) / den
    zi = (abi * ar - (abr - 1.0) * ai) / den
    br_, bi_ = P['s5_b_re'][l].astype(F32), P['s5_b_im'][l].astype(F32)
    bbr = zr[..., None] * br_ - zi[..., None] * bi_
    bbi = zr[..., None] * bi_ + zi[..., None] * br_
    eye_g = jnp.eye(S5_GROUPS, dtype=F32)
    lw['s5_bre'] = jnp.einsum('gpc,gh->gchp', bbr, eye_g).reshape(MIX_W, S5_LANES).astype(BF16)
    lw['s5_bim'] = jnp.einsum('gpc,gh->gchp', bbi, eye_g).reshape(MIX_W, S5_LANES).astype(BF16)
    lw['s5_cre'] = jnp.einsum('gcp,gh->gphc', P['s5_c_re'][l].astype(F32), eye_g).reshape(S5_LANES, MIX_W).astype(BF16)
    lw['s5_cim'] = jnp.einsum('gcp,gh->gphc', P['s5_c_im'][l].astype(F32), eye_g).reshape(S5_LANES, MIX_W).astype(BF16)
    abr_f, abi_f = abr.reshape(1, S5_LANES), abi.reshape(1, S5_LANES)
    lw['s5_abr_c'], lw['s5_abi_c'] = abr.reshape(S5_LANES, 1), abi.reshape(S5_LANES, 1)
    for k in ('s5_bre', 's5_bim', 's5_cre', 's5_cim'):
        lw[k + '_t'] = lw[k].T
    lw['s5_ar8'] = jnp.tile(abr_f, (SUBLANES, 1))
    lw['s5_ai8'] = jnp.concatenate([jnp.tile(-abi_f, (S5_BATCH, 1)), jnp.tile(abi_f, (S5_BATCH, 1))], axis=0)
    lw['s5_d'] = P['s5_d'][l].reshape(1, MIX_W)
    lw['s5_gw'] = P['s5_glu_w'][l].astype(BF16)
    lw['s5_gb'] = P['s5_glu_b'][l].reshape(1, MIX_W)
    lw['wb'] = P['w_branch'][l].astype(BF16)
    lw['wo'] = P['w_out'][l].astype(BF16)
    lw['ffn_g'] = P['ffn_norm'][l].reshape(1, D_MODEL)
    lw['ffn_wu'] = P['ffn_w_up'][l].astype(BF16)
    lw['ffn_cw'] = P['ffn_conv_w'][l]
    lw['ffn_cb'] = P['ffn_conv_b'][l].reshape(1, D_FF)
    lw['ffn_wd'] = P['ffn_w_down'][l].astype(BF16)
    return lw


def prompt_layer(x, lw, cn, b, l):
    r3 = lambda a: a.reshape(b, l, a.shape[-1])
    (hb, qn, qr, rows, win, ng, hg, su, fq, foxr, flog, rows_b, win_b, fox_b, rows_t, win_t, fox_t, flog_t
     ) = in_proj(x, lw, cn['bd256'], cn['cos_p'], cn['sin_p'], min(512, l), b)
    pparts = matmul(rows.reshape(b * l // CMP_STRIDE, CMP_STRIDE * 256), lw['we']).reshape(b, l // CMP_STRIDE, 256)
    oc, mask = nsa_cmp_prompt(r3(qn), pparts, lw, cn['cover'])
    o_nsa = nsa_flash_prompt(r3(qr), r3(rows_b), r3(win_b), mask, cn['expand'], r3(ng), oc)
    c = cumsum_time(r3(flog), cn['tri'])
    c_t = jnp.swapaxes(c[:, :, 0:SUBLANES], 1, 2)
    o_fox = fox_flash_prompt(r3(fq), r3(fox_b), c, c_t)
    o_h, hg_new = hgrn_prompt(r3(hg), lw, cn)
    o_s5, s5h = s5_prompt(r3(su), lw)
    flat = lambda a: a.reshape(b * l, a.shape[-1])
    x1 = merge(x, hb, (flat(o_nsa), flat(o_h), flat(o_s5), flat(o_fox)), lw, 256)
    x2, conv_new = ffn_prompt(x1, lw, b, l)
    nw = min(WINDOW, l)
    s5_new = jnp.moveaxis(s5h.reshape(2, S5_BATCH, S5_GROUPS, S5_STATE), 0, -1)
    return (x2,) + _kv_outputs(rows_t, win_t[:, :, l - nw:], fox_t, flog_t) + (hg_new, s5_new, conv_new)


def _kv_outputs(rows_t, win_t, fox_t, flog_t):
    b, _, l = rows_t.shape
    return (jnp.transpose(rows_t.reshape(b, 4, HEAD_DIM, l), (0, 3, 1, 2)),
            jnp.transpose(win_t.reshape(b, 2, HEAD_DIM, win_t.shape[-1]), (0, 3, 1, 2)),
            jnp.transpose(fox_t.reshape(b, 2, FOX_HEADS, HEAD_DIM, l), (0, 4, 1, 2, 3)),
            jnp.transpose(flog_t[:, 0:FOX_HEADS, :], (0, 2, 1)))


def sample_layer(x, lw, cn, l, views, pt):
    n = x.shape[0]
    kv_t, win_t, fkv_t, flf_t, hg_t, s5_t, conv = views
    (hb, qn, qr, rows, win, ng, hg, su, fq, foxr, flog, _, _, _, rows_t, _, fox_t, flog_t
     ) = in_proj(x, lw, cn['bd256'], cn['cos_s'], cn['sin_s'], n, 1)
    o_nsa, nwin_t = nsa_step(pt, kv_t, win_t, l, qn, qr, rows, win, ng, lw, cn)
    o_fox = fox_step(pt, fkv_t, flf_t, l, fq, foxr, flog, cn)
    oh_t, hgn_t = hgrn_step(hg, hg_t, l, lw)
    o_s5, s5n_t = s5_step(su, s5_t, l, lw)
    x1 = merge(x, hb, (o_nsa, oh_t.T, o_s5, o_fox), lw, n)
    x2, conv_new = ffn_step(x1, conv, l, lw)
    wb = nwin_t.shape[-1]
    new_win = jnp.transpose(nwin_t.reshape(n, 2, HEAD_DIM, wb), (0, 3, 1, 2))
    hg_new = jnp.transpose(hgn_t, (3, 0, 1, 2))
    s5_new = jnp.transpose(s5n_t.reshape(S5_GROUPS, S5_STATE, 2, n), (3, 0, 1, 2))
    rows_o, _, fox_o, flog_o = _kv_outputs(rows_t, rows_t[:, 0:2 * HEAD_DIM], fox_t, flog_t)
    seq_first = lambda a: jnp.swapaxes(a, 0, 1)
    return (x2, seq_first(rows_o), new_win, seq_first(fox_o), seq_first(flog_o), hg_new, s5_new, conv_new)


def sample_views(caches, page_table):
    cache_nsa_kv, cache_nsa_win, cache_fox_kv, cache_fox_logf, state_hgrn, state_s5, state_conv = caches
    d, n_pool = cache_nsa_kv.shape[0:2]
    n = cache_nsa_win.shape[1]
    kv_t = jnp.transpose(cache_nsa_kv, (0, 1, 3, 4, 2)).reshape(d, n_pool, 4 * HEAD_DIM, PAGE_SIZE)
    win_t = jnp.transpose(cache_nsa_win, (0, 1, 3, 4, 2)).reshape(d, n, 2 * HEAD_DIM, cache_nsa_win.shape[2])
    fkv_t = jnp.transpose(cache_fox_kv, (0, 1, 3, 4, 5, 2)).reshape(d, n_pool, 2 * FOX_HEADS * HEAD_DIM, PAGE_SIZE)
    flf_t = jnp.transpose(cache_fox_logf, (0, 1, 3, 2))
    hg_t = jnp.transpose(state_hgrn, (0, 2, 3, 4, 1))
    s5_t = jnp.transpose(state_s5, (0, 2, 3, 4, 1)).reshape(d, S5_LANES, 2, n)
    pt = jnp.transpose(page_table).reshape(-1)
    return (kv_t, win_t, fkv_t, flf_t, hg_t, s5_t, state_conv), pt


PARAM_NAMES =('attn_norm', 'w_in', 'nsa_q_norm', 'nsa_k_norm', 'nsa_cmp_pe', 'nsa_cmp_wk', 'nsa_cmp_wv',
               'hgrn_o_norm', 's5_a_re', 's5_a_im', 's5_log_dt', 's5_b_re', 's5_b_im', 's5_c_re', 's5_c_im',
               's5_d', 's5_glu_w', 's5_glu_b', 'fox_q_norm', 'fox_k_norm', 'fox_f_bias', 'w_branch', 'w_out',
               'ffn_norm', 'ffn_w_up', 'ffn_conv_w', 'ffn_conv_b', 'ffn_w_down')


def trunk(x_prompt, x_sample, caches, page_table, hgrn_lb_logits, P):
    depth = P['w_in'].shape[0]
    bp, lp, _ = x_prompt.shape
    ns = x_sample.shape[0]
    past_len = page_table.shape[1] * PAGE_SIZE
    sm = jax.nn.softmax(hgrn_lb_logits.astype(F32), axis=0)
    lb_all = jnp.cumsum(sm, axis=0) - sm[0]
    cn = make_consts(lp, ns, past_len)
    yp = x_prompt.reshape(bp * lp, D_MODEL)
    ys = x_sample.reshape(ns, D_MODEL)
    outs_p, outs_s = [], []
    views, pt = sample_views(caches, page_table)
    for l in range(depth):
        lw = prep_layer(l, P, lb_all)
        res_p = prompt_layer(yp, lw, cn, bp, lp)
        res_s = sample_layer(ys, lw, cn, l, views, pt)
        yp, ys = res_p[0], res_s[0]
        outs_p.append(res_p[1:])
        outs_s.append(res_s[1:])
    sp = [jnp.stack([o[i] for o in outs_p], axis=0) for i in range(7)]
    ss = [jnp.stack([o[i] for o in outs_s], axis=0) for i in range(7)]
    nsa_kv_p, nsa_win_p, fox_kv_p, fox_logf_p, hgrn_p, s5_p, conv_p = sp
    nsa_kv_s, nsa_win_s, fox_kv_s, fox_logf_s, hgrn_s, s5_s, conv_s = ss
    return (yp.reshape(bp, lp, D_MODEL), ys.reshape(ns, 1, D_MODEL), nsa_kv_p, nsa_kv_s, nsa_win_p, nsa_win_s,
            fox_kv_p, fox_kv_s, fox_logf_p, fox_logf_s, hgrn_p, hgrn_s, s5_p, s5_s, conv_p, conv_s)


def kernel(x_prompt, x_sample, cache_nsa_kv, cache_nsa_win, cache_fox_kv, cache_fox_logf, state_hgrn, state_s5, state_ffn_conv, page_table, attn_norm, w_in, nsa_q_norm, nsa_k_norm, nsa_cmp_pe, nsa_cmp_wk, nsa_cmp_wv, hgrn_lb_logits, hgrn_o_norm, s5_a_re, s5_a_im, s5_log_dt, s5_b_re, s5_b_im, s5_c_re, s5_c_im, s5_d, s5_glu_w, s5_glu_b, fox_q_norm, fox_k_norm, fox_f_bias, w_branch, w_out, ffn_norm, ffn_w_up, ffn_conv_w, ffn_conv_b, ffn_w_down):
    P = dict(attn_norm=attn_norm, w_in=w_in, nsa_q_norm=nsa_q_norm, nsa_k_norm=nsa_k_norm,
             nsa_cmp_pe=nsa_cmp_pe, nsa_cmp_wk=nsa_cmp_wk, nsa_cmp_wv=nsa_cmp_wv, hgrn_o_norm=hgrn_o_norm,
             s5_a_re=s5_a_re, s5_a_im=s5_a_im, s5_log_dt=s5_log_dt, s5_b_re=s5_b_re, s5_b_im=s5_b_im,
             s5_c_re=s5_c_re, s5_c_im=s5_c_im, s5_d=s5_d, s5_glu_w=s5_glu_w, s5_glu_b=s5_glu_b,
             fox_q_norm=fox_q_norm, fox_k_norm=fox_k_norm, fox_f_bias=fox_f_bias, w_branch=w_branch,
             w_out=w_out, ffn_norm=ffn_norm, ffn_w_up=ffn_w_up, ffn_conv_w=ffn_conv_w,
             ffn_conv_b=ffn_conv_b, ffn_w_down=ffn_w_down)
    caches = (cache_nsa_kv, cache_nsa_win, cache_fox_kv, cache_fox_logf, state_hgrn, state_s5, state_ffn_conv)
    return trunk(x_prompt, x_sample, caches, page_table, hgrn_lb_logits, P)
```

```python
import math
from functools import partial

import numpy as np
import jax
import jax.numpy as jnp
from jax import lax
from jax.experimental import pallas as pl
from jax.experimental.pallas import tpu as pltpu

D_MODEL = 1024
DEPTH = 4
PAGE_SIZE = 128
HEAD_DIM = 64
N_BRANCH = 4
MIX_W = D_MODEL // N_BRANCH
NSA_HEADS = MIX_W // HEAD_DIM
CMP_LEN = 32
CMP_STRIDE = 16
SEL_BLOCK = 64
N_SEL = 16
WINDOW = 512
SEL_FORCE = 1000.0
HG_HEADS = 4
HG_DK = MIX_W // HG_HEADS
HG_DV = MIX_W // HG_HEADS
HG_CHUNK = 64
S5_GROUP = 16
S5_GROUPS = MIX_W // S5_GROUP
S5_STATE = 64
S5_LANES = S5_GROUPS * S5_STATE
FOX_HEADS = MIX_W // HEAD_DIM
Q_BLOCK = 128
D_FF = ((8 * D_MODEL // 3 + 127) // 128) * 128
CONV_W = 3
ROPE_THETA = 10000.0
EPS = 1e-6
SCALE = HEAD_DIM ** -0.5
LOG2E = math.log2(math.e)
NEG = -1e30

IN_SIZES = (NSA_HEADS * HEAD_DIM, 6 * HEAD_DIM, 3 * NSA_HEADS,
            HG_HEADS * HG_DK, HG_HEADS * HG_DK, HG_HEADS * HG_DV, HG_HEADS * HG_DV,
            MIX_W,
            FOX_HEADS * HEAD_DIM, FOX_HEADS * HEAD_DIM, FOX_HEADS * HEAD_DIM, FOX_HEADS,
            N_BRANCH * D_MODEL)

LANES = 128
SUBLANES = 8
VMEM_LIMIT_BYTES = 56 * 1024 * 1024

F32 = jnp.float32
BF16 = jnp.bfloat16

C_NQ, C_NKV, C_NG, C_HG, C_SU, C_FQ, C_FK, C_FV, C_FF, C_END = 0, 256, 640, 768, 1792, 2048, 2304, 2560, 2816, 2944


def _cparams(sem):
    return pltpu.CompilerParams(dimension_semantics=sem, vmem_limit_bytes=VMEM_LIMIT_BYTES)


def _split2(x):
    hi = x.astype(BF16)
    lo = (x - hi.astype(F32)).astype(BF16)
    return hi, lo


def _split3(x):
    hi = x.astype(BF16)
    r = x - hi.astype(F32)
    mid = r.astype(BF16)
    lo = (r - mid.astype(F32)).astype(BF16)
    return hi, mid, lo


def _dot(a, b):
    return jnp.dot(a, b, preferred_element_type=F32)


def _dot_nt(a, b):
    return lax.dot_general(a, b, (((1,), (1,)), ((), ())), preferred_element_type=F32)


def _dot_tn(a, b):
    return lax.dot_general(a, b, (((0,), (0,)), ((), ())), preferred_element_type=F32)


def _dot_exact_lhs(a01, x):
    hi, mid, lo = _split3(x)
    return _dot(a01, hi) + _dot(a01, mid) + _dot(a01, lo)


def _dot_exact_rhs(x, b01):
    hi, mid, lo = _split3(x)
    return _dot(hi, b01) + _dot(mid, b01) + _dot(lo, b01)


def _seg_rms(x, bd, gain):
    ms = _dot_exact_rhs(x * x, bd) * (1.0 / HEAD_DIM)
    return x * lax.rsqrt(ms + EPS) * gain


def _rope(x, cos, sin_signed):
    w = x.shape[-1]
    half = HEAD_DIM // 2
    lane = lax.broadcasted_iota(jnp.int32, x.shape, 1)
    first = (lane % HEAD_DIM) < half
    swapped = jnp.where(first, pltpu.roll(x, w - half, 1), pltpu.roll(x, half, 1))
    return x * cos + swapped * sin_signed


def _log_sigmoid(x):
    return jnp.minimum(x, 0.0) - jnp.log(1.0 + jnp.exp(-jnp.abs(x)))


def _sigmoid(x):
    return 1.0 / (1.0 + jnp.exp(-x))


def _silu(x):
    return x * _sigmoid(x)


def _mm_kernel(x_ref, w_ref, o_ref):
    o_ref[...] = _dot(x_ref[...].astype(BF16), w_ref[...])


def matmul(x, w, tm=256):
    m, k = x.shape
    n = w.shape[1]
    tm = min(tm, m)
    return pl.pallas_call(
        _mm_kernel,
        grid=(m // tm,),
        in_specs=[pl.BlockSpec((tm, k), lambda i: (i, 0)),
                  pl.BlockSpec((k, n), lambda i: (0, 0))],
        out_specs=pl.BlockSpec((tm, n), lambda i: (i, 0)),
        out_shape=jax.ShapeDtypeStruct((m, n), F32),
        compiler_params=_cparams(("arbitrary",)),
        name="matmul",
    )(x, w.astype(BF16))


def _in_proj_kernel(x_ref, g_ref, w_ref, cos_ref, sin_ref, bd_ref, gq_ref, gk_ref, gfq_ref, gfk_ref, fb_ref,
                    h_ref, qn_ref, qr_ref, rows_ref, win_ref, ng_ref, hg_ref, su_ref, fq_ref, fox_ref, flog_ref,
                    rows_b_ref, win_b_ref, fox_b_ref, rows_t_ref, win_t_ref, fox_t_ref, flog_t_ref):
    x = x_ref[...]
    h = x * lax.rsqrt(jnp.mean(x * x, axis=-1, keepdims=True) + EPS) * g_ref[...]
    hb = h.astype(BF16)
    h_ref[...] = hb
    p = _dot(hb, w_ref[...])
    bd = bd_ref[...]
    cos, sin = cos_ref[...], sin_ref[...]
    qn = _seg_rms(p[:, C_NQ:C_NKV], bd, gq_ref[...])
    qn_ref[...] = qn
    qr_ref[...] = _rope(qn, cos, sin)
    raw = p[:, C_NKV:C_NKV + 256]
    lane = lax.broadcasted_iota(jnp.int32, raw.shape, 1)
    nr = _rope(_seg_rms(raw, bd, gk_ref[:, 0:256]), cos, sin)
    rows = jnp.where((lane >= 2 * HEAD_DIM) & (lane < 3 * HEAD_DIM), nr, raw)
    raww = p[:, C_NKV + 256:C_NG]
    nw = _rope(_seg_rms(raww, bd[0:128, 0:128], gk_ref[:, 256:384]), cos[:, 0:128], sin[:, 0:128])
    lane_w = lax.broadcasted_iota(jnp.int32, raww.shape, 1)
    win = jnp.where(lane_w < HEAD_DIM, nw, raww)
    ng_ref[...] = p[:, C_NG:C_HG]
    hg_ref[...] = p[:, C_HG:C_SU]
    su_ref[...] = p[:, C_SU:C_FQ]
    fq_ref[...] = _seg_rms(p[:, C_FQ:C_FK], bd, gfq_ref[...])
    fk = _seg_rms(p[:, C_FK:C_FV], bd, gfk_ref[...])
    fv = p[:, C_FV:C_FF]
    flog = _log_sigmoid(p[:, C_FF:C_END] + fb_ref[...])
    rows_ref[...] = rows
    win_ref[...] = win
    fox_ref[:, 0:256] = fk
    fox_ref[:, 256:512] = fv
    flog_ref[...] = flog
    rows_b_ref[...] = rows.astype(BF16)
    win_b_ref[...] = win.astype(BF16)
    fox_b_ref[:, 0:256] = fk.astype(BF16)
    fox_b_ref[:, 256:512] = fv.astype(BF16)
    rows_t_ref[...] = rows.T
    win_t_ref[...] = win.T
    fox_t_ref[0:256, :] = fk.T
    fox_t_ref[256:512, :] = fv.T
    flog_t_ref[...] = flog.T[0:SUBLANES, :]


def in_proj(x, lw, bd256, cos_t, sin_t, tm, nb):
    t = x.shape[0]
    seg = t // nb
    nt = seg // tm
    row = lambda i: (i, 0)
    const = lambda i: (0, 0)
    tab = lambda i: (i % nt, 0)
    tmap = lambda i: (i // nt, 0, i % nt)
    widths = (256, 256, 256, 128, 128, 1024, 256, 256, 512, 128)
    bwidths = (256, 128, 512)
    twidths = (256, 128, 512, SUBLANES)
    out_shape = ([jax.ShapeDtypeStruct((t, D_MODEL), BF16)] + [jax.ShapeDtypeStruct((t, w), F32) for w in widths]
                 + [jax.ShapeDtypeStruct((t, w), BF16) for w in bwidths]
                 + [jax.ShapeDtypeStruct((nb, w, seg), F32) for w in twidths])
    out_specs = ([pl.BlockSpec((tm, D_MODEL), row)] + [pl.BlockSpec((tm, w), row) for w in widths]
                 + [pl.BlockSpec((tm, w), row) for w in bwidths]
                 + [pl.BlockSpec((None, w, tm), tmap) for w in twidths])
    return pl.pallas_call(
        _in_proj_kernel,
        grid=(t // tm,),
        in_specs=[pl.BlockSpec((tm, D_MODEL), row), pl.BlockSpec((1, D_MODEL), const),
                  pl.BlockSpec((D_MODEL, C_END), const), pl.BlockSpec((tm, 256), tab), pl.BlockSpec((tm, 256), tab),
                  pl.BlockSpec((256, 256), const), pl.BlockSpec((1, 256), const), pl.BlockSpec((1, 384), const),
                  pl.BlockSpec((1, 256), const), pl.BlockSpec((1, 256), const), pl.BlockSpec((1, 128), const)],
        out_specs=out_specs,
        out_shape=out_shape,
        compiler_params=_cparams(("arbitrary",)),
        name="in_proj",
    )(x, lw['attn_norm'], lw['w_in'], cos_t, sin_t, bd256, lw['gq'], lw['gk'], lw['gfq'], lw['gfk'], lw['fbias'])


def _topk_mask(score, k, n_cand):
    r = -(-n_cand // SUBLANES) * SUBLANES
    s = score.T[0:r, :]
    idx_f = lax.broadcasted_iota(jnp.int32, s.shape, 0).astype(F32)
    sel = jnp.zeros(s.shape, F32)
    for _ in range(k):
        m = jnp.max(s, axis=0, keepdims=True)
        first = jnp.min(jnp.where(s == m, idx_f, float(LANES)), axis=0, keepdims=True)
        pick = idx_f == first
        sel = jnp.where(pick, 1.0, sel)
        s = jnp.where(pick, NEG, s)
    if r < LANES:
        sel = jnp.concatenate([sel, jnp.zeros((LANES - r, s.shape[1]), F32)], axis=0)
    return sel.T


def _topk_rank(score_row, k, eye, ones):
    n = score_row.shape[1]
    row_b = jnp.broadcast_to(score_row, (n, n))
    col_b = _dot_exact_rhs(eye.astype(F32) * row_b, ones)
    ii = lax.broadcasted_iota(jnp.int32, (n, n), 0)
    jj = lax.broadcasted_iota(jnp.int32, (n, n), 1)
    beats = (row_b > col_b) | ((row_b == col_b) & (jj < ii))
    return _dot(jnp.where(beats, 1.0, 0.0).astype(BF16), ones) < k


def _pe_term(pe_ref, wkv_ref):
    pew = _dot(pe_ref[...], wkv_ref[...])
    return pew[0:1, :] + pew[1:2, :]


def _cmp_tokens(p, pew, gain_c, n_c):
    n16 = p.shape[0]
    nxt = pltpu.roll(p, n16 - 1, 0)
    kc = p[:, 0:64] + nxt[:, 128:192] + pew[0:1, 0:64]
    vc = p[:, 64:128] + nxt[:, 192:256] + pew[0:1, 64:128]
    kc = kc * lax.rsqrt(jnp.mean(kc * kc, axis=-1, keepdims=True) + EPS) * gain_c
    return kc, vc


def _cmp_attend(q, kc, vc, valid):
    qh, ql = _split2(q)
    kh, kl = _split2(kc)
    s = _dot_nt(qh, kh) + _dot_nt(ql, kh) + _dot_nt(qh, kl)
    s = jnp.where(valid, s, NEG)
    m = jnp.max(s, axis=1, keepdims=True)
    e = jnp.where(valid, jnp.exp(s - m), 0.0)
    pc = e / jnp.maximum(jnp.sum(e, axis=1, keepdims=True), 1e-30)
    return pc, _dot(pc.astype(BF16), vc.astype(BF16))


def _cmp_kernel(q_ref, p_ref, pe_ref, wkv_ref, gc_ref, cover_ref, oc_ref, mask_ref, kc_ref, vc_ref, *, tq, n_c):
    qi = pl.program_id(1)

    @pl.when(qi == 0)
    def _():
        kc, vc = _cmp_tokens(p_ref[...], _pe_term(pe_ref, wkv_ref), gc_ref[...], n_c)
        kc_ref[...] = kc
        vc_ref[...] = vc

    kc, vc = kc_ref[...], vc_ref[...]
    n16 = kc.shape[0]
    pos = qi * tq + lax.broadcasted_iota(jnp.int32, (tq, n16), 0)
    ci = lax.broadcasted_iota(jnp.int32, (tq, n16), 1)
    valid = (ci * CMP_STRIDE + CMP_LEN - 1 <= pos) & (ci < n_c)
    q = q_ref[...]
    pcs = jnp.zeros((tq, n16), F32)
    outs = []
    for h in range(NSA_HEADS):
        pc, o = _cmp_attend(q[:, h * HEAD_DIM:(h + 1) * HEAD_DIM] * SCALE, kc, vc, valid)
        pcs = pcs + pc
        outs.append(o)
    oc_ref[...] = jnp.concatenate(outs, axis=1)
    imp = _dot_exact_rhs(pcs, cover_ref[...])
    j = lax.broadcasted_iota(jnp.int32, (tq, LANES), 1)
    cur = (qi * tq + lax.broadcasted_iota(jnp.int32, (tq, LANES), 0)) // SEL_BLOCK
    forced = (j == 0) | (j == cur) | (j == cur - 1)
    score = jnp.where(j > cur, NEG, imp + jnp.where(forced, SEL_FORCE, 0.0))
    mask_ref[...] = _topk_mask(score, N_SEL, (n_c + 1) * CMP_STRIDE // SEL_BLOCK)


def nsa_cmp_prompt(qn, pparts, lw, cover, tq=512):
    b, l, _ = qn.shape
    tq = min(tq, l)
    n16 = l // CMP_STRIDE
    n_c = n16 - 1
    return pl.pallas_call(
        partial(_cmp_kernel, tq=tq, n_c=n_c),
        grid=(b, l // tq),
        in_specs=[pl.BlockSpec((None, tq, 256), lambda bi, qi: (bi, qi, 0)),
                  pl.BlockSpec((None, n16, 256), lambda bi, qi: (bi, 0, 0)),
                  pl.BlockSpec((SUBLANES, CMP_LEN * HEAD_DIM), lambda bi, qi: (0, 0)),
                  pl.BlockSpec((CMP_LEN * HEAD_DIM, 128), lambda bi, qi: (0, 0)),
                  pl.BlockSpec((1, HEAD_DIM), lambda bi, qi: (0, 0)),
                  pl.BlockSpec((n16, LANES), lambda bi, qi: (0, 0))],
        out_specs=[pl.BlockSpec((None, tq, 256), lambda bi, qi: (bi, qi, 0)),
                   pl.BlockSpec((None, tq, LANES), lambda bi, qi: (bi, qi, 0))],
        out_shape=[jax.ShapeDtypeStruct((b, l, 256), F32), jax.ShapeDtypeStruct((b, l, LANES), F32)],
        scratch_shapes=[pltpu.VMEM((n16, HEAD_DIM), F32), pltpu.VMEM((n16, HEAD_DIM), F32)],
        compiler_params=_cparams(("arbitrary", "arbitrary")),
        name="nsa_cmp",
    )(qn, pparts, lw['pe8'], lw['wkv'], lw['gc'], cover)


def _online_update(s, v, m_ref, l_ref, acc_ref, h):
    m_old = m_ref[h]
    m_new = jnp.maximum(m_old, jnp.max(s, axis=1, keepdims=True))
    alpha = jnp.exp2(m_old - m_new)
    p = jnp.exp2(s - jnp.concatenate([m_new] * (s.shape[1] // LANES), axis=1))
    l_ref[h] = alpha * l_ref[h] + jnp.sum(p, axis=1, keepdims=True)
    acc_ref[h] = alpha[:, 0:HEAD_DIM] * acc_ref[h] + _dot(p.astype(BF16), v)
    m_ref[h] = m_new


def _flash_init(m_refs, l_refs, acc_refs):
    for r in m_refs:
        r[...] = jnp.full(r.shape, NEG, F32)
    for r in l_refs + acc_refs:
        r[...] = jnp.zeros(r.shape, F32)


def _nsa_flash_kernel(q_ref, rows_ref, win_ref, mask_ref, e_ref, ng_ref, oc_ref, o_ref,
                      qs_ref, ms_ref, ls_ref, as_ref, mw_ref, lw_ref, aw_ref, *, tq, tk):
    qi, kj = pl.program_id(1), pl.program_id(2)
    nk = pl.num_programs(2)

    @pl.when(kj == 0)
    def _():
        _flash_init([ms_ref, mw_ref], [ls_ref, lw_ref], [as_ref, aw_ref])
        q = q_ref[...]
        for h in range(NSA_HEADS):
            qs_ref[h] = (q[:, h * HEAD_DIM:(h + 1) * HEAD_DIM] * (SCALE * LOG2E)).astype(BF16)

    q_lo = qi * tq
    k_lo = kj * tk
    qpos = q_lo + lax.broadcasted_iota(jnp.int32, (tq, tk), 0)
    kpos = k_lo + lax.broadcasted_iota(jnp.int32, (tq, tk), 1)
    causal = kpos <= qpos

    @pl.when(k_lo <= q_lo + tq - 1)
    def _():
        rows = rows_ref[...]
        ks = rows[:, 2 * HEAD_DIM:3 * HEAD_DIM]
        vs = rows[:, 3 * HEAD_DIM:4 * HEAD_DIM]
        sel = _dot(mask_ref[...].astype(BF16), e_ref[...]) > 0.5
        bias = jnp.where(sel & causal, 0.0, NEG)
        for h in range(NSA_HEADS):
            _online_update(_dot_nt(qs_ref[h], ks) + bias, vs, ms_ref, ls_ref, as_ref, h)

    @pl.when((k_lo <= q_lo + tq - 1) & (k_lo + tk - 1 > q_lo - WINDOW))
    def _():
        win = win_ref[...]
        kw = win[:, 0:HEAD_DIM]
        vw = win[:, HEAD_DIM:2 * HEAD_DIM]
        bias = jnp.where(causal & (qpos - kpos < WINDOW), 0.0, NEG)
        for h in range(NSA_HEADS):
            _online_update(_dot_nt(qs_ref[h], kw) + bias, vw, mw_ref, lw_ref, aw_ref, h)

    @pl.when(kj == nk - 1)
    def _():
        gs = _sigmoid(ng_ref[...])
        oc = oc_ref[...]
        outs = []
        for h in range(NSA_HEADS):
            o_s = as_ref[h] / ls_ref[h][:, 0:HEAD_DIM]
            o_w = aw_ref[h] / lw_ref[h][:, 0:HEAD_DIM]
            outs.append(gs[:, 3 * h:3 * h + 1] * oc[:, h * HEAD_DIM:(h + 1) * HEAD_DIM]
                        + gs[:, 3 * h + 1:3 * h + 2] * o_s + gs[:, 3 * h + 2:3 * h + 3] * o_w)
        o_ref[...] = jnp.concatenate(outs, axis=1)


def nsa_flash_prompt(qr, rows, win, mask, expand, ng, oc, tq=512, tk=512):
    b, l, _ = qr.shape
    tq, tk = min(tq, l), min(tk, l)
    nq, nk = l // tq, l // tk

    def kv_idx(bi, qi, kj):
        return (bi, jnp.minimum(kj, (qi * tq + tq - 1) // tk), 0)

    def win_idx(bi, qi, kj):
        hi = (qi * tq + tq - 1) // tk
        lo = jnp.maximum((qi * tq - WINDOW + 1) // tk, 0)
        return (bi, jnp.clip(kj, lo, hi), 0)

    qmap = lambda bi, qi, kj: (bi, qi, 0)
    hshape = (NSA_HEADS, tq, LANES)
    return pl.pallas_call(
        partial(_nsa_flash_kernel, tq=tq, tk=tk),
        grid=(b, nq, nk),
        in_specs=[pl.BlockSpec((None, tq, 256), qmap),
                  pl.BlockSpec((None, tk, 256), kv_idx),
                  pl.BlockSpec((None, tk, 128), win_idx),
                  pl.BlockSpec((None, tq, LANES), qmap),
                  pl.BlockSpec((LANES, tk), lambda bi, qi, kj: (0, jnp.minimum(kj, (qi * tq + tq - 1) // tk))),
                  pl.BlockSpec((None, tq, LANES), qmap),
                  pl.BlockSpec((None, tq, 256), qmap)],
        out_specs=pl.BlockSpec((None, tq, 256), qmap),
        out_shape=jax.ShapeDtypeStruct((b, l, 256), F32),
        scratch_shapes=[pltpu.VMEM((NSA_HEADS, tq, HEAD_DIM), BF16),
                        pltpu.VMEM(hshape, F32), pltpu.VMEM(hshape, F32), pltpu.VMEM((NSA_HEADS, tq, HEAD_DIM), F32),
                        pltpu.VMEM(hshape, F32), pltpu.VMEM(hshape, F32), pltpu.VMEM((NSA_HEADS, tq, HEAD_DIM), F32)],
        compiler_params=_cparams(("arbitrary", "arbitrary", "arbitrary")),
        name="nsa_flash",
    )(qr, rows, win, mask, expand, ng, oc)


def _cumsum_kernel(x_ref, tri_ref, o_ref, carry_ref):
    @pl.when(pl.program_id(1) == 0)
    def _():
        carry_ref[...] = jnp.zeros(carry_ref.shape, F32)

    c = _dot_exact_lhs(tri_ref[...], x_ref[...]) + carry_ref[0:1, :]
    o_ref[...] = c
    carry_ref[0:1, :] = c[c.shape[0] - 1:c.shape[0], :]


def cumsum_time(x, tri):
    b, l, w = x.shape
    tc = tri.shape[0]
    return pl.pallas_call(
        _cumsum_kernel,
        grid=(b, l // tc),
        in_specs=[pl.BlockSpec((None, tc, w), lambda bi, i: (bi, i, 0)),
                  pl.BlockSpec((tc, tc), lambda bi, i: (0, 0))],
        out_specs=pl.BlockSpec((None, tc, w), lambda bi, i: (bi, i, 0)),
        out_shape=jax.ShapeDtypeStruct((b, l, w), F32),
        scratch_shapes=[pltpu.VMEM((SUBLANES, w), F32)],
        compiler_params=_cparams(("arbitrary", "arbitrary")),
        name="cumsum_time",
    )(x, tri)


def _fox_flash_kernel(q_ref, kv_ref, cq_ref, ck_ref, o_ref, qs_ref, cqb_ref, m_ref, l_ref, acc_ref, *, tq, tk):
    qi, kj = pl.program_id(1), pl.program_id(2)
    nk = pl.num_programs(2)

    @pl.when(kj == 0)
    def _():
        _flash_init([m_ref], [l_ref], [acc_ref])
        q = q_ref[...]
        cq = cq_ref[...]
        for h in range(FOX_HEADS):
            qs_ref[h] = (q[:, h * HEAD_DIM:(h + 1) * HEAD_DIM] * (SCALE * LOG2E)).astype(BF16)
            cqb_ref[h] = jnp.broadcast_to(cq[:, h:h + 1] * LOG2E, (tq, LANES))

    q_lo = qi * tq
    k_lo = kj * tk

    def step(on_diagonal):
        kv = kv_ref[...]
        ck = ck_ref[...] * LOG2E
        if on_diagonal:
            qpos = q_lo + lax.broadcasted_iota(jnp.int32, (tq, tk), 0)
            kpos = k_lo + lax.broadcasted_iota(jnp.int32, (tq, tk), 1)
            causal = jnp.where(kpos <= qpos, 0.0, NEG)
        for h in range(FOX_HEADS):
            k = kv[:, h * HEAD_DIM:(h + 1) * HEAD_DIM]
            v = kv[:, 256 + h * HEAD_DIM:256 + (h + 1) * HEAD_DIM]
            bias = jnp.concatenate([cqb_ref[h]] * (tk // LANES), axis=1) - ck[h:h + 1, :]
            if on_diagonal:
                bias = bias + causal
            _online_update(_dot_nt(qs_ref[h], k) + bias, v, m_ref, l_ref, acc_ref, h)

    @pl.when(k_lo + tk - 1 <= q_lo)
    def _():
        step(False)

    @pl.when((k_lo <= q_lo + tq - 1) & (k_lo + tk - 1 > q_lo))
    def _():
        step(True)

    @pl.when(kj == nk - 1)
    def _():
        o_ref[...] = jnp.concatenate(
            [acc_ref[h] / l_ref[h][:, 0:HEAD_DIM] for h in range(FOX_HEADS)], axis=1)


def fox_flash_prompt(fq, fox_rows, c, c_t, tq=512, tk=512):
    b, l, _ = fq.shape
    tq, tk = min(tq, l), min(tk, l)
    nq, nk = l // tq, l // tk
    qmap = lambda bi, qi, kj: (bi, qi, 0)
    last = lambda qi: (qi * tq + tq - 1) // tk
    hshape = (FOX_HEADS, tq, LANES)
    return pl.pallas_call(
        partial(_fox_flash_kernel, tq=tq, tk=tk),
        grid=(b, nq, nk),
        in_specs=[pl.BlockSpec((None, tq, 256), qmap),
                  pl.BlockSpec((None, tk, 512), lambda bi, qi, kj: (bi, jnp.minimum(kj, last(qi)), 0)),
                  pl.BlockSpec((None, tq, LANES), qmap),
                  pl.BlockSpec((None, SUBLANES, tk), lambda bi, qi, kj: (bi, 0, jnp.minimum(kj, last(qi))))],
        out_specs=pl.BlockSpec((None, tq, 256), qmap),
        out_shape=jax.ShapeDtypeStruct((b, l, 256), F32),
        scratch_shapes=[pltpu.VMEM((FOX_HEADS, tq, HEAD_DIM), BF16), pltpu.VMEM(hshape, F32),
                        pltpu.VMEM(hshape, F32), pltpu.VMEM(hshape, F32), pltpu.VMEM((FOX_HEADS, tq, HEAD_DIM), F32)],
        compiler_params=_cparams(("arbitrary", "arbitrary", "arbitrary")),
        name="fox_flash",
    )(fq, fox_rows, c, c_t)


HG_LEVELS = int(math.log2(HG_CHUNK))


def _hgrn_constants():
    c = HG_CHUNK
    t = np.arange(c)[:, None]
    u = np.arange(c)[None, :]
    mats = [(u <= t), (u > t)]
    masks = []
    for lv in range(HG_LEVELS):
        bit = (t >> lv) & 1
        p_t = (t >> lv) << lv
        mats.append((bit == 1) & (u >= p_t) & (u <= t))
    for lv in range(HG_LEVELS):
        bit = (t >> lv) & 1
        p_s = ((t >> lv) + 1) << lv
        mats.append((bit == 0) & (u > t) & (u <= p_s - 1))
    for lv in range(HG_LEVELS):
        masks.append(((t >> (lv + 1)) == (u >> (lv + 1))) & (((t >> lv) & 1) == 1) & (((u >> lv) & 1) == 0))
    masks.append(t == u)
    cst = np.concatenate([m.astype(np.float32) for m in mats], axis=0)
    msk = np.stack([m.astype(np.float32) for m in masks], axis=0)
    return cst, msk


def _hgrn_kernel(hg_ref, lb_ref, cst_ref, msk_ref, eye_ref, bd_ref, go_ref, o_ref, st_ref, s_ref, *, tt):
    i = pl.program_id(1)
    c = HG_CHUNK

    @pl.when(i == 0)
    def _():
        s_ref[...] = jnp.zeros(s_ref.shape, F32)

    lb = lb_ref[...]
    nbb = hg_ref.shape[0]
    bd = bd_ref[...] > 0
    heads_bd = lambda x: jnp.where(bd, jnp.concatenate([x] * HG_HEADS, axis=0), jnp.zeros((), x.dtype))
    msk = [jnp.concatenate([msk_ref[lv]] * HG_HEADS, axis=1) for lv in range(HG_LEVELS + 1)]
    for ch, bb in [(ch, bb) for ch in range(tt // c) for bb in range(nbb)]:
        r = ch * c
        hq = hg_ref[bb, r:r + c, 0:256]
        hf = hg_ref[bb, r:r + c, 256:512]
        hv = hg_ref[bb, r:r + c, 512:768]
        hgate = hg_ref[bb, r:r + c, 768:1024]
        f = lb + (1.0 - lb) * _sigmoid(hf)
        lf = jnp.log(f)
        kk = 1.0 - f
        e = _dot(cst_ref[...], jnp.concatenate(_split3(lf), axis=0))
        eb = jnp.exp(e[0:c])
        esu = jnp.exp(e[c:2 * c])
        ebl = eb[c - 1:c, :]
        qf = [(hq * jnp.exp(e[(2 + lv) * c:(3 + lv) * c])).astype(BF16) for lv in range(HG_LEVELS)]
        kg = [(kk * jnp.exp(e[(2 + HG_LEVELS + lv) * c:(3 + HG_LEVELS + lv) * c])).astype(BF16)
              for lv in range(HG_LEVELS)]
        qb, kb, vb = hq.astype(BF16), kk.astype(BF16), hv.astype(BF16)
        qe, ke = (hq * eb).astype(BF16), (kk * esu).astype(BF16)
        att = msk[HG_LEVELS] * _dot_nt(qb, heads_bd(kb))
        for lv in range(HG_LEVELS):
            att = att + msk[lv] * _dot_nt(qf[lv], heads_bd(kg[lv]))
        st = s_ref[bb]
        o = _dot(att.astype(BF16), heads_bd(vb)) + _dot_nt(qe, st.astype(BF16))
        s_ref[bb] = st * ebl + jnp.where(bd, _dot_tn(vb, ke), 0.0)
        o_ref[bb, r:r + c, :] = _seg_rms(o, bd_ref[...], go_ref[...]) * _silu(hgate)

    @pl.when(i == pl.num_programs(1) - 1)
    def _():
        for bb in range(nbb):
            for h in range(HG_HEADS):
                blk = s_ref[bb, h * HG_DV:(h + 1) * HG_DV, h * HG_DK:(h + 1) * HG_DK]
                hi, mid, lo = _split3(blk)
                eye = eye_ref[...]
                st_ref[bb, h] = _dot_nt(eye, hi) + _dot_nt(eye, mid) + _dot_nt(eye, lo)


def hgrn_prompt(hg, lw, cn, tt=512, nbb=2):
    b, l, _ = hg.shape
    tt = min(tt, l)
    const2 = lambda bi, i: (0, 0)
    return pl.pallas_call(
        partial(_hgrn_kernel, tt=tt),
        grid=(b // nbb, l // tt),
        in_specs=[pl.BlockSpec((nbb, tt, 1024), lambda bi, i: (bi, i, 0)),
                  pl.BlockSpec((1, 256), const2),
                  pl.BlockSpec(cn['hg_cst'].shape, const2),
                  pl.BlockSpec(cn['hg_msk'].shape, lambda bi, i: (0, 0, 0)),
                  pl.BlockSpec((HG_DK, HG_DK), const2),
                  pl.BlockSpec((256, 256), const2),
                  pl.BlockSpec((1, 256), const2)],
        out_specs=[pl.BlockSpec((nbb, tt, 256), lambda bi, i: (bi, i, 0)),
                   pl.BlockSpec((nbb, HG_HEADS, HG_DK, HG_DV), lambda bi, i: (bi, 0, 0, 0))],
        out_shape=[jax.ShapeDtypeStruct((b, l, 256), F32),
                   jax.ShapeDtypeStruct((b, HG_HEADS, HG_DK, HG_DV), F32)],
        scratch_shapes=[pltpu.VMEM((nbb, HG_HEADS * HG_DV, HG_HEADS * HG_DK), F32)],
        compiler_params=_cparams(("arbitrary", "arbitrary")),
        name="hgrn",
    )(hg, lw['lb'], cn['hg_cst'], cn['hg_msk'], cn['eye64'], cn['bd256'], lw['go'])


S5_BATCH = 4


def _gelu_tanh(x):
    return 0.5 * x * (1.0 + jnp.tanh(math.sqrt(2.0 / math.pi) * (x + 0.044715 * (x * x * x))))


def _s5_kernel(u_ref, bre_ref, bim_ref, ar_ref, ai_ref, cre_ref, cim_ref, d_ref, gw_ref, gb_ref,
               o_ref, hout_ref, xs_ref, h_ref, *, tc):
    i = pl.program_id(0)

    @pl.when(i == 0)
    def _():
        h_ref[...] = jnp.zeros(h_ref.shape, F32)

    nlt = S5_LANES // LANES
    for b in range(S5_BATCH):
        ub = u_ref[b].astype(BF16)
        xre, xim = _dot(ub, bre_ref[...]), _dot(ub, bim_ref[...])
        for j in range(nlt):
            xs_ref[j, pl.ds(b, tc, stride=SUBLANES), :] = xre[:, j * LANES:(j + 1) * LANES]
            xs_ref[j, pl.ds(S5_BATCH + b, tc, stride=SUBLANES), :] = xim[:, j * LANES:(j + 1) * LANES]
    ar = [ar_ref[:, j * LANES:(j + 1) * LANES] for j in range(nlt)]
    ai = [ai_ref[:, j * LANES:(j + 1) * LANES] for j in range(nlt)]

    def body(t, hcur):
        off = pl.multiple_of(t * SUBLANES, SUBLANES)
        hnew = []
        for j in range(nlt):
            hj = ar[j] * hcur[j] + ai[j] * pltpu.roll(hcur[j], S5_BATCH, 0) + xs_ref[j, pl.ds(off, SUBLANES), :]
            xs_ref[j, pl.ds(off, SUBLANES), :] = hj
            hnew.append(hj)
        return tuple(hnew)

    h0 = tuple(h_ref[:, j * LANES:(j + 1) * LANES] for j in range(nlt))
    hfin = lax.fori_loop(0, tc, body, h0, unroll=8)
    for j in range(nlt):
        h_ref[:, j * LANES:(j + 1) * LANES] = hfin[j]
    for b in range(S5_BATCH):
        hre = jnp.concatenate([xs_ref[j, pl.ds(b, tc, stride=SUBLANES), :] for j in range(nlt)], axis=1)
        him = jnp.concatenate([xs_ref[j, pl.ds(S5_BATCH + b, tc, stride=SUBLANES), :] for j in range(nlt)], axis=1)
        y = _dot(hre.astype(BF16), cre_ref[...]) - _dot(him.astype(BF16), cim_ref[...]) + d_ref[...] * u_ref[b]
        z = _gelu_tanh(y)
        o_ref[b] = z * _sigmoid(_dot(z.astype(BF16), gw_ref[...]) + gb_ref[...])

    @pl.when(i == pl.num_programs(0) - 1)
    def _():
        hout_ref[...] = h_ref[...]


def s5_prompt(u, lw, tc=256):
    b, l, _ = u.shape
    tc = min(tc, l)
    assert b == S5_BATCH
    const = lambda i: (0, 0)
    return pl.pallas_call(
        partial(_s5_kernel, tc=tc),
        grid=(l // tc,),
        in_specs=[pl.BlockSpec((b, tc, 256), lambda i: (0, i, 0)),
                  pl.BlockSpec((256, S5_LANES), const), pl.BlockSpec((256, S5_LANES), const),
                  pl.BlockSpec((SUBLANES, S5_LANES), const), pl.BlockSpec((SUBLANES, S5_LANES), const),
                  pl.BlockSpec((S5_LANES, 256), const), pl.BlockSpec((S5_LANES, 256), const),
                  pl.BlockSpec((1, 256), const), pl.BlockSpec((256, 256), const), pl.BlockSpec((1, 256), const)],
        out_specs=[pl.BlockSpec((b, tc, 256), lambda i: (0, i, 0)),
                   pl.BlockSpec((SUBLANES, S5_LANES), const)],
        out_shape=[jax.ShapeDtypeStruct((b, l, 256), F32), jax.ShapeDtypeStruct((SUBLANES, S5_LANES), F32)],
        scratch_shapes=[pltpu.VMEM((S5_LANES // LANES, tc * SUBLANES, LANES), F32),
                        pltpu.VMEM((SUBLANES, S5_LANES), F32)],
        compiler_params=_cparams(("arbitrary",)),
        name="s5_scan",
    )(u, lw['s5_bre'], lw['s5_bim'], lw['s5_ar8'], lw['s5_ai8'], lw['s5_cre'], lw['s5_cim'],
      lw['s5_d'], lw['s5_gw'], lw['s5_gb'])


def _merge_kernel(x_ref, h_ref, b0_ref, b1_ref, b2_ref, b3_ref, wg_ref, wb_ref, wo_ref, o_ref):
    hb = h_ref[...]
    merged = None
    for n, br in enumerate((b0_ref, b1_ref, b2_ref, b3_ref)):
        g = _sigmoid(_dot(hb, wg_ref[:, n * D_MODEL:(n + 1) * D_MODEL]))
        term = g * _dot(br[...].astype(BF16), wb_ref[n])
        merged = term if merged is None else merged + term
    o_ref[...] = x_ref[...] + _dot(merged.astype(BF16), wo_ref[...])


def merge(x, hb, branches, lw, tm):
    t = x.shape[0]
    row = lambda i: (i, 0)
    const = lambda i: (0, 0)
    return pl.pallas_call(
        _merge_kernel,
        grid=(t // tm,),
        in_specs=[pl.BlockSpec((tm, D_MODEL), row), pl.BlockSpec((tm, D_MODEL), row)]
                 + [pl.BlockSpec((tm, MIX_W), row)] * N_BRANCH
                 + [pl.BlockSpec((D_MODEL, N_BRANCH * D_MODEL), const),
                    pl.BlockSpec((N_BRANCH, MIX_W, D_MODEL), lambda i: (0, 0, 0)),
                    pl.BlockSpec((D_MODEL, D_MODEL), const)],
        out_specs=pl.BlockSpec((tm, D_MODEL), row),
        out_shape=jax.ShapeDtypeStruct((t, D_MODEL), F32),
        compiler_params=_cparams(("arbitrary",)),
        name="merge",
    )(x, hb, *branches, lw['wg'], lw['wb'], lw['wo'])


def _ffn_core(x, g, wu, cw, cb, wd, a_m2, a_m1):
    h = x * lax.rsqrt(jnp.mean(x * x, axis=-1, keepdims=True) + EPS) * g
    up = _dot(h.astype(BF16), wu)
    a, bb = up[:, 0:D_FF], up[:, D_FF:2 * D_FF]
    ac = cb + a_m2(a) * cw[0:1, :] + a_m1(a) * cw[1:2, :] + a * cw[2:3, :]
    return a, x + _dot((_silu(ac) * bb).astype(BF16), wd)


def _ffn_seq_kernel(x_ref, g_ref, wu_ref, cw_ref, cb_ref, wd_ref, o_ref, cs_ref, prev_ref, *, tm):
    @pl.when(pl.program_id(1) == 0)
    def _():
        prev_ref[...] = jnp.zeros(prev_ref.shape, F32)

    prev = prev_ref[...]
    row = lax.broadcasted_iota(jnp.int32, (tm, D_FF), 0)
    p1, p2 = prev[SUBLANES - 1:SUBLANES, :], prev[SUBLANES - 2:SUBLANES - 1, :]
    a_m1 = lambda a: jnp.where(row == 0, p1, pltpu.roll(a, 1, 0))
    a_m2 = lambda a: jnp.where(row == 0, p2, jnp.where(row == 1, p1, pltpu.roll(a, 2, 0)))
    a, o = _ffn_core(x_ref[...], g_ref[...], wu_ref[...], cw_ref[...], cb_ref[...], wd_ref[...], a_m2, a_m1)
    o_ref[...] = o
    tail = a[tm - SUBLANES:tm, :]
    prev_ref[...] = tail
    cs_ref[...] = tail[SUBLANES - (CONV_W - 1):SUBLANES, :]


def _ffn_step_kernel(x_ref, cs_ref, g_ref, wu_ref, cw_ref, cb_ref, wd_ref, o_ref, cso_ref):
    b0, b1 = cs_ref[:, 0, :], cs_ref[:, 1, :]
    a, o = _ffn_core(x_ref[...], g_ref[...], wu_ref[...], cw_ref[...], cb_ref[...], wd_ref[...],
                     lambda a: b0, lambda a: b1)
    o_ref[...] = o
    cso_ref[:, 0, :] = b1
    cso_ref[:, 1, :] = a


def _ffn_weight_specs(const):
    return [pl.BlockSpec((1, D_MODEL), const), pl.BlockSpec((D_MODEL, 2 * D_FF), const),
            pl.BlockSpec((CONV_W, D_FF), const), pl.BlockSpec((1, D_FF), const), pl.BlockSpec((D_FF, D_MODEL), const)]


def ffn_prompt(x, lw, b, l, tm=256):
    nt = l // tm
    const = lambda bi, i: (0, 0)
    row = lambda bi, i: (bi * nt + i, 0)
    return pl.pallas_call(
        partial(_ffn_seq_kernel, tm=tm),
        grid=(b, nt),
        in_specs=[pl.BlockSpec((tm, D_MODEL), row)] + _ffn_weight_specs(const),
        out_specs=[pl.BlockSpec((tm, D_MODEL), row),
                   pl.BlockSpec((None, CONV_W - 1, D_FF), lambda bi, i: (bi, 0, 0))],
        out_shape=[jax.ShapeDtypeStruct((b * l, D_MODEL), F32), jax.ShapeDtypeStruct((b, CONV_W - 1, D_FF), F32)],
        scratch_shapes=[pltpu.VMEM((SUBLANES, D_FF), F32)],
        compiler_params=_cparams(("arbitrary", "arbitrary")),
        name="ffn_seq",
    )(x, lw['ffn_g'], lw['ffn_wu'], lw['ffn_cw'], lw['ffn_cb'], lw['ffn_wd'])


def ffn_step(x, conv_state, l, lw):
    n = x.shape[0]
    const = lambda i: (0, 0)
    return pl.pallas_call(
        _ffn_step_kernel,
        grid=(1,),
        in_specs=[pl.BlockSpec((n, D_MODEL), const),
                  pl.BlockSpec((None, n, CONV_W - 1, D_FF), lambda i: (l, 0, 0, 0))] + _ffn_weight_specs(const),
        out_specs=[pl.BlockSpec((n, D_MODEL), const), pl.BlockSpec((n, CONV_W - 1, D_FF), lambda i: (0, 0, 0))],
        out_shape=[jax.ShapeDtypeStruct((n, D_MODEL), F32), jax.ShapeDtypeStruct((n, CONV_W - 1, D_FF), F32)],
        compiler_params=_cparams(("arbitrary",)),
        name="ffn_step",
    )(x, conv_state, lw['ffn_g'], lw['ffn_wu'], lw['ffn_cw'], lw['ffn_cb'], lw['ffn_wd'])


def _heads_to_rows(row):
    return jnp.concatenate([row[:, h * HEAD_DIM:(h + 1) * HEAD_DIM] for h in range(NSA_HEADS)]
                           + [jnp.zeros((SUBLANES - NSA_HEADS, HEAD_DIM), F32)], axis=0)


def _rows_to_heads(o8):
    return jnp.concatenate([o8[h:h + 1, :] for h in range(NSA_HEADS)], axis=1)


def _col8(cells):
    return jnp.concatenate(list(cells) + [jnp.zeros((SUBLANES - len(cells), 1), F32)], axis=0)


def _all_max(x):
    return jnp.max(jnp.max(x, axis=1, keepdims=True), axis=0, keepdims=True)


def _all_sum(x):
    return jnp.sum(jnp.sum(x, axis=1, keepdims=True), axis=0, keepdims=True)


def _paged_softmax(s, valid, s_new):
    m = jnp.maximum(_all_max(jnp.where(valid, s, NEG)), s_new)
    e = jnp.where(valid, jnp.exp(s - m), 0.0)
    e_new = jnp.exp(s_new - m)
    inv = 1.0 / (_all_sum(e) + e_new)
    return e * inv, e_new * inv


SEQ_PER_STEP = 2


def _nsa_step_kernel(pt_ref, *refs, n_pages, wb, past_len):
    ns = SEQ_PER_STEP
    o = ns * n_pages
    winc_ref = refs[o]
    shared = refs[o + 1:o + 15]
    o_ref, nwin_ref = refs[o + 15], refs[o + 16]
    scr = refs[o + 17:]
    for s in range(ns):
        _nsa_step_one(pl.program_id(0) * ns + s, refs[s * n_pages:(s + 1) * n_pages], winc_ref.at[s], *shared,
                      o_ref, nwin_ref.at[s], *scr[3 * s:3 * s + 3], n_pages=n_pages, wb=wb, past_len=past_len)


def _nsa_step_one(b, kv, winc_ref, qn_ref, qr_ref, rows_ref, win_ref, ng_ref, wcmp_ref, pe_ref, wkv_ref, gc_ref,
                  cover_ref, amask_ref, ex_ref, eye_ref, ones_ref, o_ref, nwin_ref, xc_ref, sl_ref, pl_ref,
                  *, n_pages, wb, past_len):
    row = lambda ref: ref[pl.ds(b, 1), :]
    rnew, wnew = row(rows_ref), row(win_ref)
    gs = _sigmoid(row(ng_ref))
    for j in range(n_pages):
        xc_ref[j * PAGE_SIZE:(j + 1) * PAGE_SIZE, :] = kv[j][0:2 * HEAD_DIM, :].T
    n16 = n_pages * PAGE_SIZE // CMP_STRIDE
    p = jnp.zeros((n16, 256), F32)
    for r in range(CMP_STRIDE):
        p = p + _dot(xc_ref[pl.ds(r, n16, stride=CMP_STRIDE), :].astype(BF16), wcmp_ref[r])
    n_c = n16 - 1
    kc, vc = _cmp_tokens(p, _pe_term(pe_ref, wkv_ref), gc_ref[...], n_c)
    ci = lax.broadcasted_iota(jnp.int32, (SUBLANES, n16), 1)
    ri = lax.broadcasted_iota(jnp.int32, (SUBLANES, n16), 0)
    pc, o_c = _cmp_attend(_heads_to_rows(row(qn_ref)) * SCALE, kc, vc, ci < n_c)
    pcs = jnp.sum(jnp.where(ri < NSA_HEADS, pc, 0.0), axis=0, keepdims=True)
    imp = _dot_exact_rhs(jnp.broadcast_to(pcs, (SUBLANES, n16)), cover_ref[...])
    j_l = lax.broadcasted_iota(jnp.int32, (SUBLANES, LANES), 1)
    cur = past_len // SEL_BLOCK
    forced = (j_l == 0) | (j_l == cur) | (j_l == cur - 1)
    score = jnp.where(j_l > cur, NEG, imp + jnp.where(forced, SEL_FORCE, 0.0))
    sel_b = _topk_rank(score[0:1, :], N_SEL, eye_ref[...], ones_ref[...])
    selx = _dot(amask_ref[...].astype(BF16), jnp.where(sel_b, ex_ref[...], 0.0).astype(BF16)) > 0.5
    qr8 = _heads_to_rows(row(qr_ref)) * SCALE
    qrb = qr8.astype(BF16)
    for j in range(n_pages):
        sl_ref[:, j, :] = _dot(qrb, kv[j][2 * HEAD_DIM:3 * HEAD_DIM, :].astype(BF16))
    s_new = jnp.sum(qr8 * rnew[:, 2 * HEAD_DIM:3 * HEAD_DIM], axis=1, keepdims=True)
    p_new = []
    for h in range(NSA_HEADS):
        ph, pn = _paged_softmax(sl_ref[h], selx, s_new[h:h + 1, :])
        pl_ref[h] = ph
        p_new.append(pn)
    pl_ref[NSA_HEADS:SUBLANES] = jnp.zeros((SUBLANES - NSA_HEADS, n_pages, PAGE_SIZE), F32)
    o_s = _col8(p_new) * rnew[:, 3 * HEAD_DIM:4 * HEAD_DIM]
    for j in range(n_pages):
        o_s = o_s + _dot_nt(pl_ref[:, j, :].astype(BF16), kv[j][3 * HEAD_DIM:4 * HEAD_DIM, :].astype(BF16))
    wc = winc_ref[...]
    s_w = _dot(qrb, wc[0:HEAD_DIM, :].astype(BF16))
    valid_w = lax.broadcasted_iota(jnp.int32, s_w.shape, 1) > wb - WINDOW
    s_wn = jnp.sum(qr8 * wnew[:, 0:HEAD_DIM], axis=1, keepdims=True)
    m_w = jnp.maximum(jnp.max(jnp.where(valid_w, s_w, NEG), axis=1, keepdims=True), s_wn)
    e_w = jnp.where(valid_w, jnp.exp(s_w - m_w), 0.0)
    e_wn = jnp.exp(s_wn - m_w)
    o_w = (_dot_nt(e_w.astype(BF16), wc[HEAD_DIM:2 * HEAD_DIM, :].astype(BF16)) + e_wn * wnew[:, HEAD_DIM:2 * HEAD_DIM]
           ) / (jnp.sum(e_w, axis=1, keepdims=True) + e_wn)
    gate = lambda k: _col8([gs[:, 3 * h + k:3 * h + k + 1] for h in range(NSA_HEADS)])
    o_ref[pl.ds(b, 1), :] = _rows_to_heads(gate(0) * o_c + gate(1) * o_s + gate(2) * o_w)
    wn8 = jnp.broadcast_to(wnew, (SUBLANES, 2 * HEAD_DIM))
    hi, mid, lo = _split3(wn8)
    eye = eye_ref[...]
    col = (_dot_nt(eye, hi) + _dot_nt(eye, mid) + _dot_nt(eye, lo))[:, 0:1]
    lane = lax.broadcasted_iota(jnp.int32, wc.shape, 1)
    nwin_ref[...] = jnp.where(lane == wb - 1, col, pltpu.roll(wc, wb - 1, 1))


def nsa_step(pt, kv_t, win_t, l, qn, qr, rows, win, ng, lw, cn):
    n = qn.shape[0]
    wb = win_t.shape[-1]
    n_pages = pt.shape[0] // n
    past_len = n_pages * PAGE_SIZE
    assert wb == WINDOW
    ns = SEQ_PER_STEP
    res = lambda w: pl.BlockSpec((n, w), lambda b, pt: (0, 0))
    c2 = lambda shape: pl.BlockSpec(shape, lambda b, pt: (0,) * len(shape))
    page = lambda s, j: pl.BlockSpec((None, None, 256, PAGE_SIZE), lambda b, pt: (l, pt[j * n + b * ns + s], 0, 0))
    seqs = range(ns)
    return dict(
        static=dict(n_pages=n_pages, wb=wb, past_len=past_len),
        in_specs=[page(s, j) for s in seqs for j in range(n_pages)]
                 + [pl.BlockSpec((None, ns, 2 * HEAD_DIM, wb), lambda b, pt: (l, b, 0, 0)),
                    res(256), res(256), res(256), res(128), res(128),
                    c2((CMP_STRIDE, 128, 256)), c2((SUBLANES, CMP_LEN * HEAD_DIM)),
                    c2((CMP_LEN * HEAD_DIM, 128)), c2((1, HEAD_DIM)), c2((past_len // CMP_STRIDE, LANES)),
                    c2((n_pages, LANES)), c2((LANES, PAGE_SIZE)), c2((LANES, LANES)), c2((LANES, LANES))],
        out_specs=[pl.BlockSpec((n, 256), lambda b, pt: (0, 0)),
                   pl.BlockSpec((ns, 2 * HEAD_DIM, wb), lambda b, pt: (b, 0, 0))],
        scratch=[pltpu.VMEM((past_len, 2 * HEAD_DIM), F32),
                 pltpu.VMEM((SUBLANES, n_pages, PAGE_SIZE), F32),
                 pltpu.VMEM((SUBLANES, n_pages, PAGE_SIZE), F32)] * ns,
        out_shape=[jax.ShapeDtypeStruct((n, 256), F32), jax.ShapeDtypeStruct((n, 2 * HEAD_DIM, wb), F32)],
        args=[kv_t] * (ns * n_pages) + [win_t, qn, qr, rows, win, ng, lw['wcmp'], lw['pe8'], lw['wkv'],
                                        lw['gc'], cn['cover_s'], cn['amask'], cn['ex'], cn['eye128'], cn['ones128']])


def _fox_step_kernel(pt_ref, *refs, n_pages):
    ns = SEQ_PER_STEP
    o = 2 * ns * n_pages
    shared = refs[o:o + 7]
    scr = refs[o + 7:]
    for s in range(ns):
        _fox_step_one(pl.program_id(0) * ns + s, refs[s * n_pages:(s + 1) * n_pages],
                      refs[(ns + s) * n_pages:(ns + s + 1) * n_pages], *shared, *scr[3 * s:3 * s + 3],
                      n_pages=n_pages)


def _fox_step_one(b, kv, lf, fq_ref, fox_ref, flog_ref, usuf_ref, ones_ref, lsu_ref, o_ref, lfs_ref, sl_ref, pl_ref,
                  *, n_pages):
    fq = fq_ref[pl.ds(b, 1), :]
    kvnew = fox_ref[pl.ds(b, 1), :]
    fl = flog_ref[pl.ds(b, 1), :]
    ri = lax.broadcasted_iota(jnp.int32, (SUBLANES, 256), 0)
    li = lax.broadcasted_iota(jnp.int32, (SUBLANES, 256), 1)
    bdm = (li // HEAD_DIM) == ri
    qbd = jnp.where(bdm, jnp.broadcast_to(fq * SCALE, (SUBLANES, 256)), 0.0)
    qb = qbd.astype(BF16)
    for j in range(n_pages):
        lfs_ref[:, j, :] = lf[j][...]
        sl_ref[:, j, :] = _dot(qb, kv[j][0:256, :].astype(BF16))
    s_new = jnp.sum(qbd * kvnew[:, 0:256], axis=1, keepdims=True)
    always = jnp.full((n_pages, PAGE_SIZE), True)
    p_new = []
    lf_all = lfs_ref[...].reshape(FOX_HEADS * n_pages, PAGE_SIZE)
    suf = _dot_exact_rhs(lf_all, usuf_ref[...])
    tail = _dot_exact_lhs(lsu_ref[...], _dot_exact_rhs(lf_all, ones_ref[...]))
    bias_all = suf + tail
    for h in range(FOX_HEADS):
        bias = bias_all[h * n_pages:(h + 1) * n_pages, :] + fl[:, h:h + 1]
        ph, pn = _paged_softmax(sl_ref[h] + bias, always, s_new[h:h + 1, :])
        pl_ref[h] = ph
        p_new.append(pn)
    pl_ref[FOX_HEADS:SUBLANES] = jnp.zeros((SUBLANES - FOX_HEADS, n_pages, PAGE_SIZE), F32)
    acc = _col8(p_new) * kvnew[:, 256:512]
    for j in range(n_pages):
        acc = acc + _dot_nt(pl_ref[:, j, :].astype(BF16), kv[j][256:512, :].astype(BF16))
    o_ref[pl.ds(b, 1), :] = jnp.sum(jnp.where(bdm, acc, 0.0), axis=0, keepdims=True)


def fox_step(pt, kv_t, lf_t, l, fq, foxr, flog, cn):
    n = fq.shape[0]
    n_pages = pt.shape[0] // n
    ns = SEQ_PER_STEP
    res = lambda w: pl.BlockSpec((n, w), lambda b, pt: (0, 0))
    c2 = lambda shape: pl.BlockSpec(shape, lambda b, pt: (0,) * len(shape))
    pidx = lambda s, j: (lambda b, pt: (l, pt[j * n + b * ns + s], 0, 0))
    page = lambda s, j: pl.BlockSpec((None, None, 512, PAGE_SIZE), pidx(s, j))
    lpage = lambda s, j: pl.BlockSpec((None, None, FOX_HEADS, PAGE_SIZE), pidx(s, j))
    scr = lambda r: pltpu.VMEM((r, n_pages, PAGE_SIZE), F32)
    every = [(s, j) for s in range(ns) for j in range(n_pages)]
    return dict(
        static=dict(n_pages=n_pages),
        in_specs=[page(s, j) for s, j in every] + [lpage(s, j) for s, j in every]
                 + [res(256), res(512), res(128), c2((PAGE_SIZE, PAGE_SIZE)), c2((PAGE_SIZE, PAGE_SIZE)),
                    c2((FOX_HEADS * n_pages, FOX_HEADS * n_pages))],
        out_specs=[pl.BlockSpec((n, 256), lambda b, pt: (0, 0))],
        scratch=[scr(FOX_HEADS), scr(SUBLANES), scr(SUBLANES)] * ns,
        out_shape=[jax.ShapeDtypeStruct((n, 256), F32)],
        args=[kv_t] * (ns * n_pages) + [lf_t] * (ns * n_pages) + [fq, foxr, flog, cn['usuf'], cn['ones128'],
                                                                  cn['lsu']])


def _paged_step_kernel(pt_ref, *refs, counts, nsa_static, fox_static):
    (ni, fi), (no, fo), (nscr, fscr) = counts
    ins, outs, scr = refs[0:ni + fi], refs[ni + fi:ni + fi + no + fo], refs[ni + fi + no + fo:]
    _nsa_step_kernel(pt_ref, *ins[0:ni], *outs[0:no], *scr[0:nscr], **nsa_static)
    _fox_step_kernel(pt_ref, *ins[ni:], *outs[no:], *scr[nscr:], **fox_static)


def paged_step(pt, nsa, fox, n):
    counts = ((len(nsa['in_specs']), len(fox['in_specs'])), (len(nsa['out_specs']), len(fox['out_specs'])),
              (len(nsa['scratch']), len(fox['scratch'])))
    return pl.pallas_call(
        partial(_paged_step_kernel, counts=counts, nsa_static=nsa['static'], fox_static=fox['static']),
        grid_spec=pltpu.PrefetchScalarGridSpec(
            num_scalar_prefetch=1,
            grid=(n // SEQ_PER_STEP,),
            in_specs=nsa['in_specs'] + fox['in_specs'],
            out_specs=nsa['out_specs'] + fox['out_specs'],
            scratch_shapes=nsa['scratch'] + fox['scratch']),
        out_shape=nsa['out_shape'] + fox['out_shape'],
        compiler_params=_cparams(("arbitrary",)),
        name="paged_step",
    )(pt, *nsa['args'], *fox['args'])


def _hgrn_step_kernel(q_ref, f_ref, v_ref, g_ref, lb_ref, go_ref, s_ref, o_ref, so_ref, t_ref):
    off = pl.multiple_of((pl.program_id(0) % 2) * HG_DK, HG_DK)

    def head_t(ref):
        t_ref[...] = ref[...].T
        return t_ref[pl.ds(off, HG_DK), :]

    q, hf, v, hgate = head_t(q_ref), head_t(f_ref), head_t(v_ref), head_t(g_ref)
    lb = lb_ref[...]
    f = lb + (1.0 - lb) * _sigmoid(hf)
    sn = f[:, None, :] * s_ref[...] + (1.0 - f)[:, None, :] * v[None, :, :]
    so_ref[...] = sn
    o = jnp.sum(q[:, None, :] * sn, axis=0)
    o = o * lax.rsqrt(jnp.mean(o * o, axis=0, keepdims=True) + EPS) * go_ref[...]
    o_ref[...] = o * _silu(hgate)


def hgrn_step(hg, state_t, l, lw):
    n = hg.shape[0]
    col = lambda base: pl.BlockSpec((n, LANES), lambda h: (0, base + h // 2))
    return pl.pallas_call(
        _hgrn_step_kernel,
        grid=(HG_HEADS,),
        in_specs=[col(0), col(2), col(4), col(6),
                  pl.BlockSpec((None, HG_DK, 1), lambda h: (h, 0, 0)),
                  pl.BlockSpec((HG_DV, 1), lambda h: (0, 0)),
                  pl.BlockSpec((None, None, HG_DK, HG_DV, n), lambda h: (l, h, 0, 0, 0))],
        out_specs=[pl.BlockSpec((HG_DV, n), lambda h: (h, 0)),
                   pl.BlockSpec((None, HG_DK, HG_DV, n), lambda h: (h, 0, 0, 0))],
        out_shape=[jax.ShapeDtypeStruct((HG_HEADS * HG_DV, n), F32),
                   jax.ShapeDtypeStruct((HG_HEADS, HG_DK, HG_DV, n), F32)],
        scratch_shapes=[pltpu.VMEM((LANES, n), F32)],
        compiler_params=_cparams(("arbitrary",)),
        name="hgrn_step",
    )(hg, hg, hg, hg, lw['lb_t'], lw['go_t'], state_t)


def _t2(x):
    return jnp.concatenate([x[:, 0:LANES].T, x[:, LANES:2 * LANES].T], axis=0)


def _s5_step_kernel(u_ref, h_ref, bre_ref, bim_ref, ar_ref, ai_ref, cre_ref, cim_ref, d_ref, gw_ref, gb_ref,
                    o_ref, ho_ref):
    u = u_ref[...]
    ut = _t2(u).astype(BF16)
    h0r, h0i = h_ref[:, 0, :], h_ref[:, 1, :]
    ar, ai = ar_ref[...], ai_ref[...]
    hr = ar * h0r - ai * h0i + _dot(bre_ref[...], ut)
    hi = ar * h0i + ai * h0r + _dot(bim_ref[...], ut)
    ho_ref[:, 0, :] = hr
    ho_ref[:, 1, :] = hi
    yt = _dot(cre_ref[...], hr.astype(BF16)) - _dot(cim_ref[...], hi.astype(BF16))
    y = jnp.concatenate([yt[0:LANES, :].T, yt[LANES:2 * LANES, :].T], axis=1) + d_ref[...] * u
    z = _gelu_tanh(y)
    o_ref[...] = z * _sigmoid(_dot(z.astype(BF16), gw_ref[...]) + gb_ref[...])


def s5_step(u, state_t, l, lw):
    n = u.shape[0]
    c2 = lambda shape: pl.BlockSpec(shape, lambda i: (0,) * len(shape))
    return pl.pallas_call(
        _s5_step_kernel,
        grid=(1,),
        in_specs=[c2((n, MIX_W)), pl.BlockSpec((None, S5_LANES, 2, n), lambda i: (l, 0, 0, 0)),
                  c2((S5_LANES, MIX_W)), c2((S5_LANES, MIX_W)), c2((S5_LANES, 1)), c2((S5_LANES, 1)),
                  c2((MIX_W, S5_LANES)), c2((MIX_W, S5_LANES)), c2((1, MIX_W)), c2((MIX_W, MIX_W)), c2((1, MIX_W))],
        out_specs=[c2((n, MIX_W)), c2((S5_LANES, 2, n))],
        out_shape=[jax.ShapeDtypeStruct((n, MIX_W), F32), jax.ShapeDtypeStruct((S5_LANES, 2, n), F32)],
        compiler_params=_cparams(("arbitrary",)),
        name="s5_step",
    )(u, state_t, lw['s5_bre_t'], lw['s5_bim_t'], lw['s5_abr_c'], lw['s5_abi_c'], lw['s5_cre_t'], lw['s5_cim_t'],
      lw['s5_d'], lw['s5_gw'], lw['s5_gb'])


def _rope_tables(pos, reps):
    half = HEAD_DIM // 2
    inv = ROPE_THETA ** (-jnp.arange(half, dtype=F32) / half)
    ang = pos.astype(F32)[:, None] * inv[None, :]
    cos, sin = jnp.cos(ang), jnp.sin(ang)
    return (jnp.tile(jnp.concatenate([cos, cos], axis=1), (1, reps)),
            jnp.tile(jnp.concatenate([-sin, sin], axis=1), (1, reps)))


def make_consts(l_prompt, n_sample, past_len):
    cn = {}
    seg = np.arange(256) // HEAD_DIM
    cn['bd256'] = jnp.asarray(seg[:, None] == seg[None, :], BF16)
    cn['eye64'] = jnp.asarray(np.eye(HG_DK), BF16)
    cst, msk = _hgrn_constants()
    cn['hg_cst'] = jnp.asarray(np.concatenate([cst] * 3, axis=1), BF16)
    cn['hg_msk'] = jnp.asarray(msk, F32)
    tc = min(512, l_prompt)
    cn['tri'] = jnp.asarray(np.tril(np.ones((tc, tc))), BF16)
    n16 = l_prompt // CMP_STRIDE
    ci = np.arange(n16)[:, None]
    sj = np.arange(LANES)[None, :]
    cover = ((ci * CMP_STRIDE < sj * SEL_BLOCK + SEL_BLOCK) & (ci * CMP_STRIDE + CMP_LEN - 1 >= sj * SEL_BLOCK)
             & (ci < n16 - 1) & (sj < -(-l_prompt // SEL_BLOCK)))
    cn['cover'] = jnp.asarray(cover, BF16)
    cn['expand'] = jnp.asarray(np.arange(LANES)[:, None] == (np.arange(l_prompt)[None, :] // SEL_BLOCK), BF16)
    cn['cos_p'], cn['sin_p'] = _rope_tables(jnp.arange(l_prompt), 4)
    cn['cos_s'], cn['sin_s'] = _rope_tables(jnp.full((n_sample,), past_len), 4)
    n16s = past_len // CMP_STRIDE
    ci = np.arange(n16s)[:, None]
    cover_s = ((ci * CMP_STRIDE < sj * SEL_BLOCK + SEL_BLOCK) & (ci * CMP_STRIDE + CMP_LEN - 1 >= sj * SEL_BLOCK)
               & (ci < n16s - 1) & (sj < -(-(past_len + 1) // SEL_BLOCK)))
    cn['cover_s'] = jnp.asarray(cover_s, BF16)
    n_pages = past_len // PAGE_SIZE
    bpp = PAGE_SIZE // SEL_BLOCK
    cn['amask'] = jnp.asarray(np.arange(LANES)[None, :] // bpp == np.arange(n_pages)[:, None], F32)
    cn['ex'] = jnp.asarray(np.arange(LANES)[:, None] % bpp == np.arange(PAGE_SIZE)[None, :] // SEL_BLOCK, BF16)
    cn['eye128'] = jnp.asarray(np.eye(LANES), BF16)
    tp = np.arange(PAGE_SIZE)
    cn['usuf'] = jnp.asarray(tp[:, None] > tp[None, :], BF16)
    cn['ones128'] = jnp.ones((PAGE_SIZE, PAGE_SIZE), BF16)
    jp = np.arange(n_pages)
    cn['lsu'] = jnp.asarray(np.kron(np.eye(FOX_HEADS), jp[None, :] > jp[:, None]), BF16)
    return cn


def prep_layer(l, P, lb_all):
    lw = {}
    w = P['w_in'][l]
    pts = np.concatenate([[0], np.cumsum(IN_SIZES)])
    seg = lambda i: w[:, pts[i]:pts[i + 1]]
    z = lambda n: jnp.zeros((D_MODEL, n), w.dtype)
    lw['w_in'] = jnp.concatenate([seg(0), seg(1), seg(2), z(LANES - IN_SIZES[2])] + [seg(i) for i in range(3, 12)]
                                 + [z(LANES - IN_SIZES[11])], axis=1).astype(BF16)
    lw['wg'] = seg(12).astype(BF16)
    lw['attn_norm'] = P['attn_norm'][l].reshape(1, D_MODEL)
    lw['gq'] = jnp.tile(P['nsa_q_norm'][l], 4).reshape(1, 256)
    kn = P['nsa_k_norm'][l]
    lw['gk'] = jnp.concatenate([jnp.tile(kn[1], 4), jnp.tile(kn[2], 2)]).reshape(1, 384)
    lw['gc'] = kn[0].reshape(1, HEAD_DIM)
    lw['gfq'] = jnp.tile(P['fox_q_norm'][l], 4).reshape(1, 256)
    lw['gfk'] = jnp.tile(P['fox_k_norm'][l], 4).reshape(1, 256)
    lw['fbias'] = jnp.pad(P['fox_f_bias'][l], (0, LANES - FOX_HEADS)).reshape(1, LANES)
    wk = P['nsa_cmp_wk'][l].reshape(2, CMP_STRIDE, HEAD_DIM, HEAD_DIM)
    wv = P['nsa_cmp_wv'][l].reshape(2, CMP_STRIDE, HEAD_DIM, HEAD_DIM)
    we = jnp.zeros((CMP_STRIDE, 4, HEAD_DIM, 4, HEAD_DIM), F32)
    we = we.at[:, 0, :, 0, :].set(wk[0]).at[:, 1, :, 1, :].set(wv[0])
    we = we.at[:, 0, :, 2, :].set(wk[1]).at[:, 1, :, 3, :].set(wv[1])
    lw['we'] = we.reshape(CMP_STRIDE * 256, 256).astype(BF16)
    lw['wcmp'] = we.reshape(CMP_STRIDE, 256, 256)[:, 0:2 * HEAD_DIM, :].astype(BF16)
    pe = P['nsa_cmp_pe'][l].reshape(1, CMP_LEN * HEAD_DIM)
    pe_hi = pe.astype(BF16)
    pe_lo = (pe - pe_hi.astype(F32)).astype(BF16)
    lw['pe8'] = jnp.concatenate([pe_hi, pe_lo, jnp.zeros((SUBLANES - 2, CMP_LEN * HEAD_DIM), BF16)], axis=0)
    lw['wkv'] = jnp.concatenate([P['nsa_cmp_wk'][l], P['nsa_cmp_wv'][l]], axis=1).astype(BF16)
    lw['lb'] = lb_all[l].reshape(1, 256)
    lw['go'] = jnp.tile(P['hgrn_o_norm'][l], 4).reshape(1, 256)
    lw['lb_t'] = lb_all[l].reshape(HG_HEADS, HG_DK, 1)
    lw['go_t'] = P['hgrn_o_norm'][l].reshape(HG_DV, 1)
    dt = jnp.exp(P['s5_log_dt'][l].astype(F32))[:, None]
    ar, ai = P['s5_a_re'][l].astype(F32), P['s5_a_im'][l].astype(F32)
    mag = jnp.exp(dt * ar)
    abr, abi = mag * jnp.cos(dt * ai), mag * jnp.sin(dt * ai)
    den = ar * ar + ai * ai
    zr = ((abr - 1.0) * ar + abi * ai) / den
    zi = (abi * ar - (abr - 1.0) * ai) / den
    br_, bi_ = P['s5_b_re'][l].astype(F32), P['s5_b_im'][l].astype(F32)
    bbr = zr[..., None] * br_ - zi[..., None] * bi_
    bbi = zr[..., None] * bi_ + zi[..., None] * br_
    eye_g = jnp.eye(S5_GROUPS, dtype=F32)
    lw['s5_bre'] = jnp.einsum('gpc,gh->gchp', bbr, eye_g).reshape(MIX_W, S5_LANES).astype(BF16)
    lw['s5_bim'] = jnp.einsum('gpc,gh->gchp', bbi, eye_g).reshape(MIX_W, S5_LANES).astype(BF16)
    lw['s5_cre'] = jnp.einsum('gcp,gh->gphc', P['s5_c_re'][l].astype(F32), eye_g).reshape(S5_LANES, MIX_W).astype(BF16)
    lw['s5_cim'] = jnp.einsum('gcp,gh->gphc', P['s5_c_im'][l].astype(F32), eye_g).reshape(S5_LANES, MIX_W).astype(BF16)
    abr_f, abi_f = abr.reshape(1, S5_LANES), abi.reshape(1, S5_LANES)
    lw['s5_abr_c'], lw['s5_abi_c'] = abr.reshape(S5_LANES, 1), abi.reshape(S5_LANES, 1)
    for k in ('s5_bre', 's5_bim', 's5_cre', 's5_cim'):
        lw[k + '_t'] = lw[k].T
    lw['s5_ar8'] = jnp.tile(abr_f, (SUBLANES, 1))
    lw['s5_ai8'] = jnp.concatenate([jnp.tile(-abi_f, (S5_BATCH, 1)), jnp.tile(abi_f, (S5_BATCH, 1))], axis=0)
    lw['s5_d'] = P['s5_d'][l].reshape(1, MIX_W)
    lw['s5_gw'] = P['s5_glu_w'][l].astype(BF16)
    lw['s5_gb'] = P['s5_glu_b'][l].reshape(1, MIX_W)
    lw['wb'] = P['w_branch'][l].astype(BF16)
    lw['wo'] = P['w_out'][l].astype(BF16)
    lw['ffn_g'] = P['ffn_norm'][l].reshape(1, D_MODEL)
    lw['ffn_wu'] = P['ffn_w_up'][l].astype(BF16)
    lw['ffn_cw'] = P['ffn_conv_w'][l]
    lw['ffn_cb'] = P['ffn_conv_b'][l].reshape(1, D_FF)
    lw['ffn_wd'] = P['ffn_w_down'][l].astype(BF16)
    return lw


def prompt_layer(x, lw, cn, b, l):
    r3 = lambda a: a.reshape(b, l, a.shape[-1])
    (hb, qn, qr, rows, win, ng, hg, su, fq, foxr, flog, rows_b, win_b, fox_b, rows_t, win_t, fox_t, flog_t
     ) = in_proj(x, lw, cn['bd256'], cn['cos_p'], cn['sin_p'], min(512, l), b)
    pparts = matmul(rows.reshape(b * l // CMP_STRIDE, CMP_STRIDE * 256), lw['we']).reshape(b, l // CMP_STRIDE, 256)
    oc, mask = nsa_cmp_prompt(r3(qn), pparts, lw, cn['cover'])
    o_nsa = nsa_flash_prompt(r3(qr), r3(rows_b), r3(win_b), mask, cn['expand'], r3(ng), oc)
    c = cumsum_time(r3(flog), cn['tri'])
    c_t = jnp.swapaxes(c[:, :, 0:SUBLANES], 1, 2)
    o_fox = fox_flash_prompt(r3(fq), r3(fox_b), c, c_t)
    o_h, hg_new = hgrn_prompt(r3(hg), lw, cn)
    o_s5, s5h = s5_prompt(r3(su), lw)
    flat = lambda a: a.reshape(b * l, a.shape[-1])
    x1 = merge(x, hb, (flat(o_nsa), flat(o_h), flat(o_s5), flat(o_fox)), lw, 256)
    x2, conv_new = ffn_prompt(x1, lw, b, l)
    nw = min(WINDOW, l)
    s5_new = jnp.moveaxis(s5h.reshape(2, S5_BATCH, S5_GROUPS, S5_STATE), 0, -1)
    return (x2,) + _kv_outputs(rows_t, win_t[:, :, l - nw:], fox_t, flog_t) + (hg_new, s5_new, conv_new)


def _kv_outputs(rows_t, win_t, fox_t, flog_t):
    b, _, l = rows_t.shape
    return (jnp.transpose(rows_t.reshape(b, 4, HEAD_DIM, l), (0, 3, 1, 2)),
            jnp.transpose(win_t.reshape(b, 2, HEAD_DIM, win_t.shape[-1]), (0, 3, 1, 2)),
            jnp.transpose(fox_t.reshape(b, 2, FOX_HEADS, HEAD_DIM, l), (0, 4, 1, 2, 3)),
            jnp.transpose(flog_t[:, 0:FOX_HEADS, :], (0, 2, 1)))


def sample_layer(x, lw, cn, l, views, pt):
    n = x.shape[0]
    kv_t, win_t, fkv_t, flf_t, hg_t, s5_t, conv = views
    (hb, qn, qr, rows, win, ng, hg, su, fq, foxr, flog, _, _, _, rows_t, _, fox_t, flog_t
     ) = in_proj(x, lw, cn['bd256'], cn['cos_s'], cn['sin_s'], n, 1)
    o_nsa, nwin_t, o_fox = paged_step(pt, nsa_step(pt, kv_t, win_t, l, qn, qr, rows, win, ng, lw, cn),
                                      fox_step(pt, fkv_t, flf_t, l, fq, foxr, flog, cn), n)
    oh_t, hgn_t = hgrn_step(hg, hg_t, l, lw)
    o_s5, s5n_t = s5_step(su, s5_t, l, lw)
    x1 = merge(x, hb, (o_nsa, oh_t.T, o_s5, o_fox), lw, n)
    x2, conv_new = ffn_step(x1, conv, l, lw)
    wb = nwin_t.shape[-1]
    new_win = jnp.transpose(nwin_t.reshape(n, 2, HEAD_DIM, wb), (0, 3, 1, 2))
    hg_new = jnp.transpose(hgn_t, (3, 0, 1, 2))
    s5_new = jnp.transpose(s5n_t.reshape(S5_GROUPS, S5_STATE, 2, n), (3, 0, 1, 2))
    rows_o, _, fox_o, flog_o = _kv_outputs(rows_t, rows_t[:, 0:2 * HEAD_DIM], fox_t, flog_t)
    seq_first = lambda a: jnp.swapaxes(a, 0, 1)
    return (x2, seq_first(rows_o), new_win, seq_first(fox_o), seq_first(flog_o), hg_new, s5_new, conv_new)


def sample_views(caches, page_table):
    cache_nsa_kv, cache_nsa_win, cache_fox_kv, cache_fox_logf, state_hgrn, state_s5, state_conv = caches
    d, n_pool = cache_nsa_kv.shape[0:2]
    n = cache_nsa_win.shape[1]
    kv_t = jnp.transpose(cache_nsa_kv, (0, 1, 3, 4, 2)).reshape(d, n_pool, 4 * HEAD_DIM, PAGE_SIZE)
    win_t = jnp.transpose(cache_nsa_win, (0, 1, 3, 4, 2)).reshape(d, n, 2 * HEAD_DIM, cache_nsa_win.shape[2])
    fkv_t = jnp.transpose(cache_fox_kv, (0, 1, 3, 4, 5, 2)).reshape(d, n_pool, 2 * FOX_HEADS * HEAD_DIM, PAGE_SIZE)
    flf_t = jnp.transpose(cache_fox_logf, (0, 1, 3, 2))
    hg_t = jnp.transpose(state_hgrn, (0, 2, 3, 4, 1))
    s5_t = jnp.transpose(state_s5, (0, 2, 3, 4, 1)).reshape(d, S5_LANES, 2, n)
    pt = jnp.transpose(page_table).reshape(-1)
    return (kv_t, win_t, fkv_t, flf_t, hg_t, s5_t, state_conv), pt


PARAM_NAMES =('attn_norm', 'w_in', 'nsa_q_norm', 'nsa_k_norm', 'nsa_cmp_pe', 'nsa_cmp_wk', 'nsa_cmp_wv',
               'hgrn_o_norm', 's5_a_re', 's5_a_im', 's5_log_dt', 's5_b_re', 's5_b_im', 's5_c_re', 's5_c_im',
               's5_d', 's5_glu_w', 's5_glu_b', 'fox_q_norm', 'fox_k_norm', 'fox_f_bias', 'w_branch', 'w_out',
               'ffn_norm', 'ffn_w_up', 'ffn_conv_w', 'ffn_conv_b', 'ffn_w_down')


def trunk(x_prompt, x_sample, caches, page_table, hgrn_lb_logits, P):
    depth = P['w_in'].shape[0]
    bp, lp, _ = x_prompt.shape
    ns = x_sample.shape[0]
    past_len = page_table.shape[1] * PAGE_SIZE
    sm = jax.nn.softmax(hgrn_lb_logits.astype(F32), axis=0)
    lb_all = jnp.cumsum(sm, axis=0) - sm[0]
    cn = make_consts(lp, ns, past_len)
    yp = x_prompt.reshape(bp * lp, D_MODEL)
    ys = x_sample.reshape(ns, D_MODEL)
    outs_p, outs_s = [], []
    views, pt = sample_views(caches, page_table)
    for l in range(depth):
        lw = prep_layer(l, P, lb_all)
        res_p = prompt_layer(yp, lw, cn, bp, lp)
        res_s = sample_layer(ys, lw, cn, l, views, pt)
        yp, ys = res_p[0], res_s[0]
        outs_p.append(res_p[1:])
        outs_s.append(res_s[1:])
    sp = [jnp.stack([o[i] for o in outs_p], axis=0) for i in range(7)]
    ss = [jnp.stack([o[i] for o in outs_s], axis=0) for i in range(7)]
    nsa_kv_p, nsa_win_p, fox_kv_p, fox_logf_p, hgrn_p, s5_p, conv_p = sp
    nsa_kv_s, nsa_win_s, fox_kv_s, fox_logf_s, hgrn_s, s5_s, conv_s = ss
    return (yp.reshape(bp, lp, D_MODEL), ys.reshape(ns, 1, D_MODEL), nsa_kv_p, nsa_kv_s, nsa_win_p, nsa_win_s,
            fox_kv_p, fox_kv_s, fox_logf_p, fox_logf_s, hgrn_p, hgrn_s, s5_p, s5_s, conv_p, conv_s)


def kernel(x_prompt, x_sample, cache_nsa_kv, cache_nsa_win, cache_fox_kv, cache_fox_logf, state_hgrn, state_s5, state_ffn_conv, page_table, attn_norm, w_in, nsa_q_norm, nsa_k_norm, nsa_cmp_pe, nsa_cmp_wk, nsa_cmp_wv, hgrn_lb_logits, hgrn_o_norm, s5_a_re, s5_a_im, s5_log_dt, s5_b_re, s5_b_im, s5_c_re, s5_c_im, s5_d, s5_glu_w, s5_glu_b, fox_q_norm, fox_k_norm, fox_f_bias, w_branch, w_out, ffn_norm, ffn_w_up, ffn_conv_w, ffn_conv_b, ffn_w_down):
    P = dict(attn_norm=attn_norm, w_in=w_in, nsa_q_norm=nsa_q_norm, nsa_k_norm=nsa_k_norm,
             nsa_cmp_pe=nsa_cmp_pe, nsa_cmp_wk=nsa_cmp_wk, nsa_cmp_wv=nsa_cmp_wv, hgrn_o_norm=hgrn_o_norm,
             s5_a_re=s5_a_re, s5_a_im=s5_a_im, s5_log_dt=s5_log_dt, s5_b_re=s5_b_re, s5_b_im=s5_b_im,
             s5_c_re=s5_c_re, s5_c_im=s5_c_im, s5_d=s5_d, s5_glu_w=s5_glu_w, s5_glu_b=s5_glu_b,
             fox_q_norm=fox_q_norm, fox_k_norm=fox_k_norm, fox_f_bias=fox_f_bias, w_branch=w_branch,
             w_out=w_out, ffn_norm=ffn_norm, ffn_w_up=ffn_w_up, ffn_conv_w=ffn_conv_w,
             ffn_conv_b=ffn_conv_b, ffn_w_down=ffn_w_down)
    caches = (cache_nsa_kv, cache_nsa_win, cache_fox_kv, cache_fox_logf, state_hgrn, state_s5, state_ffn_conv)
    return trunk(x_prompt, x_sample, caches, page_table, hgrn_lb_logits, P)
```
